```python
import math
import jax, jax.numpy as jnp
from jax import lax
import numpy as np

D_MODEL = 1024
BATCH = 4
SEQ = 8192
DEPTH = 4

MIX_WIDTH = D_MODEL
PLE_DIM = 256
QBLOCK = 128
EPS = 1e-6
W_A = MIX_WIDTH // 4
POOL_WINDOWS = (2, 4, 8, 16)
N_POOL_GROUPS = 4
POOL_GC = W_A // N_POOL_GROUPS
W_B = MIX_WIDTH - W_A
DV_B = 128
H_B = W_B // DV_B
DK_B = DV_B // 2
B_ROT = DK_B // 4
ROPE_THETA = 500000.0
W_C = MIX_WIDTH // 2
DV_C = 128
H_C = W_C // DV_C
D_NOPE = 128
D_ROPE = 64
DQK_C = D_NOPE + D_ROPE
Q_LORA = 256
KV_LORA = 128
MLA_THETA = 10000.0
W_D = MIX_WIDTH - W_C
DK_D = 128
DV_D = 128
H_D = W_D // DV_D
CONV_K = 4
CHUNK = 64
CONV_CH = H_D * (2 * DK_D + DV_D)
EVEN_WIDTHS = (W_A, W_A, 2 * H_B * DK_B, 2 * H_B * DK_B, H_B * DV_B, W_B)
ODD_WIDTHS = (Q_LORA, KV_LORA, D_ROPE, W_C, CONV_CH, H_D, H_D, W_D)

kernel_name = "hybrid_pool_diffattn_mla_gdn_trunk"


def _split(z, widths):
    idx = np.cumsum(widths)[:-1].tolist()
    return jnp.split(z, idx, axis=-1)


def rmsnorm(x, g):
    xf = x.astype(jnp.float32)
    y = xf * lax.rsqrt(jnp.mean(xf * xf, axis=-1, keepdims=True) + EPS)
    return (y * g.astype(jnp.float32)).astype(x.dtype)


def l2norm(x):
    xf = x.astype(jnp.float32)
    return xf * lax.rsqrt(jnp.sum(xf * xf, axis=-1, keepdims=True) + EPS)


def rope(x, positions, theta):
    r = x.shape[-1]
    half = r // 2
    inv = jnp.power(jnp.float32(theta), -jnp.arange(half, dtype=jnp.float32) * (2.0 / r))
    ang = positions.astype(jnp.float32)[..., None] * inv
    ang = ang.reshape(ang.shape[:2] + (1,) * (x.ndim - 3) + (half,))
    c, s = jnp.cos(ang), jnp.sin(ang)
    xf = x.astype(jnp.float32)
    x1, x2 = xf[..., :half], xf[..., half:]
    return jnp.concatenate([x1 * c - x2 * s, x2 * c + x1 * s], axis=-1).astype(x.dtype)


def partial_rope(x, positions, rot_dim, theta):
    return jnp.concatenate([rope(x[..., :rot_dim], positions, theta), x[..., rot_dim:]], axis=-1)


def _sweep_query_blocks(block_fn, q):
    B, T = q.shape[:2]
    nb = T // QBLOCK
    qb = jnp.moveaxis(q.reshape((B, nb, QBLOCK) + q.shape[2:]), 1, 0)
    starts = jnp.arange(nb, dtype=jnp.int32) * QBLOCK
    out = lax.map(block_fn, (qb, starts))
    return jnp.moveaxis(out, 0, 1).reshape((B, T) + out.shape[3:])


def multiscale_pool(x):
    B, T, _ = x.shape
    xf = x.astype(jnp.float32).reshape(B, T, N_POOL_GROUPS, POOL_GC)
    S = jnp.pad(jnp.cumsum(xf, axis=1), ((0, 0), (1, 0), (0, 0), (0, 0)))
    t = jnp.arange(T)
    outs = []
    for gi, w in enumerate(POOL_WINDOWS):
        upper = S[:, 1:, gi]
        lower = jnp.pad(S[:, :T + 1 - w, gi], ((0, 0), (w - 1, 0), (0, 0)))
        cnt = jnp.minimum(t + 1, w).astype(jnp.float32)[None, :, None]
        outs.append((upper - lower) / cnt - xf[:, :, gi])
    return jnp.stack(outs, axis=2).astype(x.dtype)


def diff_attention(q, k, v, lam):
    T = k.shape[1]
    scale = q.shape[-1] ** -0.5
    kpos = jnp.arange(T)

    def block(args):
        qb, start = args
        s = jnp.einsum('bqhmd,bkhmd->bhmqk', qb, k).astype(jnp.float32) * scale
        mask = kpos[None, :] <= (start + jnp.arange(QBLOCK))[:, None]
        pr = jax.nn.softmax(jnp.where(mask, s, -jnp.inf), axis=-1)
        a = pr[:, :, 0] - lam * pr[:, :, 1]
        return jnp.einsum('bhqk,bkhe->bqhe', a.astype(v.dtype), v)

    return _sweep_query_blocks(block, q)


def causal_attention(q, k, v):
    T = k.shape[1]
    scale = q.shape[-1] ** -0.5
    kpos = jnp.arange(T)

    def block(args):
        qb, start = args
        s = jnp.einsum('bqhd,bkhd->bhqk', qb, k).astype(jnp.float32) * scale
        mask = kpos[None, :] <= (start + jnp.arange(QBLOCK))[:, None]
        pr = jax.nn.softmax(jnp.where(mask, s, -jnp.inf), axis=-1)
        return jnp.einsum('bhqk,bkhe->bqhe', pr.astype(v.dtype), v)

    return _sweep_query_blocks(block, q)


def causal_depthwise_conv(x, w):
    K, C = w.shape
    return lax.conv_general_dilated(
        x, w[:, None, :].astype(x.dtype), window_strides=(1,), padding=[(K - 1, 0)],
        dimension_numbers=('NWC', 'WIO', 'NWC'), feature_group_count=C)


def gated_delta_rule(q, k, v, g, beta):
    B, T, H, DK = q.shape
    DV = v.shape[-1]
    N = T // CHUNK
    f32 = jnp.float32

    def to_chunks(a):
        a = a.astype(f32).reshape((B, N, CHUNK, H) + a.shape[3:])
        return jnp.moveaxis(a, 3, 1)

    q, k, v, g, beta = (to_chunks(a) for a in (q, k, v, g, beta))
    gc = jnp.cumsum(g, axis=-1)
    i = jnp.arange(CHUNK)
    incl = i[:, None] >= i[None, :]
    strict = i[:, None] > i[None, :]
    gamma = jnp.exp(jnp.where(incl, gc[..., :, None] - gc[..., None, :], -jnp.inf))
    kk = jnp.einsum('bhncd,bhnsd->bhncs', k, k)
    a_mat = jnp.eye(CHUNK, dtype=f32) + jnp.where(strict, kk * gamma * beta[..., :, None], 0.0)
    rhs = jnp.concatenate([v * beta[..., None], k * (beta * jnp.exp(gc))[..., None]], axis=-1)
    sol = lax.linalg.triangular_solve(a_mat, rhs, left_side=True, lower=True, unit_diagonal=True)
    u, w = sol[..., :DV], sol[..., DV:]
    qk = jnp.einsum('bhncd,bhnsd->bhncs', q, k) * gamma
    q_dec = q * jnp.exp(gc)[..., None]
    k_dec = k * jnp.exp(gc[..., -1:] - gc)[..., None]
    g_last = jnp.exp(gc[..., -1])

    def step(S, inp):
        qk_c, qd_c, kd_c, u_c, w_c, gl_c = inp
        v_new = u_c - jnp.einsum('bhcd,bhde->bhce', w_c, S)
        o = jnp.einsum('bhcd,bhde->bhce', qd_c, S) + jnp.einsum('bhcs,bhse->bhce', qk_c, v_new)
        S = S * gl_c[..., None, None] + jnp.einsum('bhcd,bhce->bhde', kd_c, v_new)
        return S, o

    xs = tuple(jnp.moveaxis(a, 2, 0) for a in (qk, q_dec, k_dec, u, w, g_last))
    _, o = lax.scan(step, jnp.zeros((B, H, DK, DV), f32), xs)
    o = jnp.moveaxis(o, 0, 2)
    return jnp.moveaxis(o, 1, 3).reshape(B, T, H, DV)


def even_layer(h, layer_idx, positions, w_in, pool_w, pool_scale, q_norm, k_norm,
               lam_vec, subln, w_out):
    B, T, _ = h.shape
    z = h @ w_in
    a_in, a_gate, q, k, v, b_gate = _split(z, EVEN_WIDTHS)
    a = multiscale_pool(a_in)
    a = jnp.einsum('btgc,gce->btge', a, pool_w).reshape(B, T, W_A) * pool_scale
    a = a * jax.nn.silu(a_gate)
    q = q.reshape(B, T, H_B, 2, DK_B)
    k = k.reshape(B, T, H_B, 2, DK_B)
    v = v.reshape(B, T, H_B, DV_B)
    q = partial_rope(rmsnorm(q, q_norm), positions, B_ROT, ROPE_THETA)
    k = partial_rope(rmsnorm(k, k_norm), positions, B_ROT, ROPE_THETA)
    lam_init = 0.8 - 0.6 * math.exp(-0.3 * layer_idx)
    lv = lam_vec.astype(jnp.float32)
    lam = jnp.exp(jnp.sum(lv[0] * lv[1])) - jnp.exp(jnp.sum(lv[2] * lv[3])) + lam_init
    o = diff_attention(q, k, v, lam)
    o = rmsnorm(o, subln) * (1.0 - lam_init)
    o = o.reshape(B, T, W_B) * jax.nn.silu(b_gate)
    return jnp.concatenate([a, o], axis=-1) @ w_out


def odd_layer(h, positions, w_in, q_a_norm, w_uq, kv_a_norm, w_ukv, q_norm, k_norm,
              conv_w, a_log, dt_bias, o_norm, w_out):
    B, T, _ = h.shape
    z = h @ w_in
    cq, ckv, k_rope, c_gate, qkv, d_b, d_a, d_gate = _split(z, ODD_WIDTHS)
    q = (rmsnorm(cq, q_a_norm) @ w_uq).reshape(B, T, H_C, DQK_C)
    kv = (rmsnorm(ckv, kv_a_norm) @ w_ukv).reshape(B, T, H_C, D_NOPE + DV_C)
    k_nope, v_c = kv[..., :D_NOPE], kv[..., D_NOPE:]
    k = jnp.concatenate([k_nope, jnp.broadcast_to(k_rope[:, :, None, :], (B, T, H_C, D_ROPE))], axis=-1)
    q = rmsnorm(q, q_norm)
    k = rmsnorm(k, k_norm)
    q = jnp.concatenate([q[..., :D_NOPE], rope(q[..., D_NOPE:], positions, MLA_THETA)], axis=-1)
    k = jnp.concatenate([k[..., :D_NOPE], rope(k[..., D_NOPE:], positions, MLA_THETA)], axis=-1)
    oc = causal_attention(q, k, v_c).reshape(B, T, W_C) * jax.nn.silu(c_gate)
    qkv = jax.nn.silu(causal_depthwise_conv(qkv, conv_w))
    dq = qkv[..., :H_D * DK_D].reshape(B, T, H_D, DK_D)
    dk = qkv[..., H_D * DK_D:2 * H_D * DK_D].reshape(B, T, H_D, DK_D)
    dv = qkv[..., 2 * H_D * DK_D:].reshape(B, T, H_D, DV_D)
    dq = l2norm(dq) * (DK_D ** -0.5)
    dk = l2norm(dk)
    beta = jax.nn.sigmoid(d_b.astype(jnp.float32))
    g = -jnp.exp(a_log.astype(jnp.float32)) * jax.nn.softplus(
        d_a.astype(jnp.float32) + dt_bias.astype(jnp.float32))
    od = gated_delta_rule(dq, dk, dv, g, beta).astype(h.dtype)
    od = rmsnorm(od, o_norm).reshape(B, T, W_D) * jax.nn.silu(d_gate)
    return jnp.concatenate([oc, od], axis=-1) @ w_out


def setup_inputs(seed: int = 0) -> dict:
    key = jax.random.key(seed)
    ks = jax.random.split(key, 32)
    f32 = jnp.float32
    n_even = (DEPTH + 1) // 2
    n_odd = DEPTH // 2

    def nrm(k, shape, fan_in):
        return jax.random.normal(k, shape, f32) * (fan_in ** -0.5)

    def gain(k, shape):
        return 1.0 + 0.02 * jax.random.normal(k, shape, f32)

    x = jax.random.normal(ks[0], (BATCH, SEQ, D_MODEL), f32)
    p = jax.random.normal(ks[1], (DEPTH, BATCH, SEQ, PLE_DIM), f32)
    positions = jnp.broadcast_to(jnp.arange(SEQ, dtype=jnp.int32), (BATCH, SEQ))
    dt = jnp.exp(jax.random.uniform(ks[24], (n_odd, H_D), f32, math.log(1e-3), math.log(1e-1)))
    return {
        "x": x,
        "p": p,
        "positions": positions,
        "norm_g": gain(ks[2], (DEPTH, D_MODEL)),
        "ple_w_gate": nrm(ks[3], (DEPTH, D_MODEL, D_MODEL), D_MODEL),
        "ple_w_proj": nrm(ks[4], (DEPTH, PLE_DIM, D_MODEL), PLE_DIM),
        "ev_w_in": nrm(ks[5], (n_even, D_MODEL, sum(EVEN_WIDTHS)), D_MODEL),
        "ev_pool_w": nrm(ks[6], (n_even, N_POOL_GROUPS, POOL_GC, POOL_GC), POOL_GC),
        "ev_pool_scale": gain(ks[7], (n_even, W_A)),
        "ev_q_norm": gain(ks[8], (n_even, DK_B)),
        "ev_k_norm": gain(ks[9], (n_even, DK_B)),
        "ev_lambda": 0.1 * jax.random.normal(ks[10], (n_even, 4, DK_B), f32),
        "ev_subln": gain(ks[11], (n_even, DV_B)),
        "ev_w_out": nrm(ks[12], (n_even, MIX_WIDTH, D_MODEL), MIX_WIDTH),
        "od_w_in": nrm(ks[13], (n_odd, D_MODEL, sum(ODD_WIDTHS)), D_MODEL),
        "od_q_a_norm": gain(ks[14], (n_odd, Q_LORA)),
        "od_w_uq": nrm(ks[15], (n_odd, Q_LORA, H_C * DQK_C), Q_LORA),
        "od_kv_a_norm": gain(ks[16], (n_odd, KV_LORA)),
        "od_w_ukv": nrm(ks[17], (n_odd, KV_LORA, H_C * (D_NOPE + DV_C)), KV_LORA),
        "od_q_norm": gain(ks[18], (n_odd, DQK_C)),
        "od_k_norm": gain(ks[19], (n_odd, DQK_C)),
        "od_conv_w": nrm(ks[20], (n_odd, CONV_K, CONV_CH), CONV_K),
        "od_a_log": jnp.log(jax.random.uniform(ks[21], (n_odd, H_D), f32, 1.0, 16.0)),
        "od_dt_bias": dt + jnp.log(-jnp.expm1(-dt)),
        "od_o_norm": gain(ks[22], (n_odd, DV_D)),
        "od_w_out": nrm(ks[23], (n_odd, MIX_WIDTH, D_MODEL), MIX_WIDTH),
    }


def reference(x, p, positions, norm_g, ple_w_gate, ple_w_proj,
              ev_w_in, ev_pool_w, ev_pool_scale, ev_q_norm, ev_k_norm, ev_lambda, ev_subln, ev_w_out,
              od_w_in, od_q_a_norm, od_w_uq, od_kv_a_norm, od_w_ukv, od_q_norm, od_k_norm,
              od_conv_w, od_a_log, od_dt_bias, od_o_norm, od_w_out):
    h = x
    for i in range(DEPTH):
        hn = rmsnorm(h, norm_g[i])
        j = i // 2
        if i % 2 == 0:
            m = even_layer(hn, i, positions, ev_w_in[j], ev_pool_w[j], ev_pool_scale[j],
                           ev_q_norm[j], ev_k_norm[j], ev_lambda[j], ev_subln[j], ev_w_out[j])
        else:
            m = odd_layer(hn, positions, od_w_in[j], od_q_a_norm[j], od_w_uq[j], od_kv_a_norm[j],
                          od_w_ukv[j], od_q_norm[j], od_k_norm[j], od_conv_w[j], od_a_log[j],
                          od_dt_bias[j], od_o_norm[j], od_w_out[j])
        h = h + m
        h = h + jax.nn.sigmoid(h @ ple_w_gate[i]) * (p[i] @ ple_w_proj[i])
    return h
```

```python
import functools
import math

import numpy as np
import jax
import jax.numpy as jnp
from jax import lax
from jax.experimental import pallas as pl
from jax.experimental.pallas import tpu as pltpu

F32 = jnp.float32
BF16 = jnp.bfloat16

D_MODEL = 1024
PLE_DIM = 256
EPS = 1e-6
W_A = 256
POOL_WINDOWS = (2, 4, 8, 16)
POOL_GC = 64
W_B = 768
DV_B = 128
H_B = 6
DK_B = 64
B_ROT = 16
ROPE_THETA = 500000.0
W_C = 512
DV_C = 128
H_C = 4
D_NOPE = 128
D_ROPE = 64
DQK_C = 192
Q_LORA = 256
KV_LORA = 128
MLA_THETA = 10000.0
W_D = 512
DK_D = 128
DV_D = 128
H_D = 4
CONV_K = 4
CHUNK = 64
CONV_CH = H_D * (2 * DK_D + DV_D)

LANES = 128
VMEM_LIMIT = 48 * 1024 * 1024
NEG = -1e30

ROW_TILE = 512
ATT_TK = 512
POOL_HALO = 32
CONV_HALO = 16
GDN_TILE = 512


def _cparams(sem):
    return pltpu.CompilerParams(dimension_semantics=sem, vmem_limit_bytes=VMEM_LIMIT)


def _silu(x):
    return x * (1.0 / (1.0 + jnp.exp(-x)))


def _sigmoid(x):
    return 1.0 / (1.0 + jnp.exp(-x))


def _bdot(a, b):
    return jnp.dot(a.astype(BF16), b.astype(BF16), preferred_element_type=F32)


def _bdot_nt(a, b):
    return lax.dot_general(a.astype(BF16), b.astype(BF16), (((1,), (1,)), ((), ())),
                           preferred_element_type=F32)


def _bdot_tn(a, b):
    return lax.dot_general(a.astype(BF16), b.astype(BF16), (((0,), (0,)), ((), ())),
                           preferred_element_type=F32)


def _split3_dot(a_exact, b):
    b0 = b.astype(BF16)
    r1 = b - b0.astype(F32)
    b1 = r1.astype(BF16)
    b2 = (r1 - b1.astype(F32)).astype(BF16)
    a = a_exact.astype(BF16)
    return (jnp.dot(a, b0, preferred_element_type=F32) + jnp.dot(a, b1, preferred_element_type=F32)
            + jnp.dot(a, b2, preferred_element_type=F32))


def _split3_dot_l(b, a_exact):
    b0 = b.astype(BF16)
    r1 = b - b0.astype(F32)
    b1 = r1.astype(BF16)
    b2 = (r1 - b1.astype(F32)).astype(BF16)
    a = a_exact.astype(BF16)
    return (jnp.dot(b0, a, preferred_element_type=F32) + jnp.dot(b1, a, preferred_element_type=F32)
            + jnp.dot(b2, a, preferred_element_type=F32))


def _rope_table_kernel(pos_ref, inv_ref, sgn_ref, c_ref, s_ref):
    ang = pos_ref[...] * inv_ref[...]
    c_ref[...] = jnp.cos(ang)
    s_ref[...] = jnp.sin(ang) * sgn_ref[...]


def _rope_tables(pos_b, inv, sgn):
    n = pos_b.shape[0]
    tm = min(1024, n)
    row = pl.BlockSpec((tm, LANES), lambda i: (i, 0))
    par = pl.BlockSpec((1, LANES), lambda i: (0, 0))
    return pl.pallas_call(
        _rope_table_kernel,
        grid=(n // tm,),
        in_specs=[row, par, par],
        out_specs=[row, row],
        out_shape=[jax.ShapeDtypeStruct((n, LANES), F32)] * 2,
        compiler_params=_cparams(("parallel",)),
        name="rope_tables",
    )(pos_b, inv, sgn)


def _rope_block(x, c, s, half):
    fwd = pltpu.roll(x, LANES - half, 1)
    bwd = pltpu.roll(x, half, 1)
    lane = lax.broadcasted_iota(jnp.int32, x.shape, 1)
    rot = jnp.where((lane % (2 * half)) < half, fwd, bwd)
    return x * c + rot * s


def _even_in_kernel(h_ref, g_ref, w_ref, gm_ref, qg_ref, kg_ref, c_ref, s_ref,
                    ain_ref, agate_ref, q_ref, k_ref, v_ref, bgate_ref):
    x = h_ref[...]
    ms = jnp.mean(x * x, axis=-1, keepdims=True)
    yb = (x * lax.rsqrt(ms + EPS) * g_ref[...]).astype(BF16)

    def seg(lo, hi):
        return jnp.dot(yb, w_ref[:, lo:hi], preferred_element_type=F32)

    ain_ref[...] = seg(0, W_A)
    agate_ref[...] = seg(W_A, 2 * W_A).astype(BF16)
    c = c_ref[...]
    s = s_ref[...]

    def qk_prep(z, gain_ref, out_ref):
        sq = z * z
        hi = sq.astype(BF16)
        lo = (sq - hi.astype(F32)).astype(BF16)
        gm = gm_ref[...]
        msq = jnp.dot(hi, gm, preferred_element_type=F32) + jnp.dot(lo, gm, preferred_element_type=F32)
        zn = z * lax.rsqrt(msq + EPS) * gain_ref[...]
        for j in range(W_B // LANES):
            blk = zn[:, j * LANES:(j + 1) * LANES]
            out_ref[:, j * LANES:(j + 1) * LANES] = _rope_block(blk, c, s, B_ROT // 2).astype(BF16)

    o = 2 * W_A
    qk_prep(seg(o, o + W_B), qg_ref, q_ref)
    qk_prep(seg(o + W_B, o + 2 * W_B), kg_ref, k_ref)
    v_ref[...] = seg(o + 2 * W_B, o + 3 * W_B).astype(BF16)
    bgate_ref[...] = seg(o + 3 * W_B, o + 4 * W_B).astype(BF16)


def _even_in(h, g, w_in, gm, qg, kg, c_tab, s_tab):
    n = h.shape[0]
    tm = min(ROW_TILE, n)
    wtot = w_in.shape[1]
    row = lambda w: pl.BlockSpec((tm, w), lambda i: (i, 0))
    full = lambda a: pl.BlockSpec(a.shape, lambda i: (0,) * a.ndim)
    outs = [(W_A, F32), (W_A, BF16), (W_B, BF16), (W_B, BF16), (W_B, BF16), (W_B, BF16)]
    return pl.pallas_call(
        _even_in_kernel,
        grid=(n // tm,),
        in_specs=[row(D_MODEL), full(g), full(w_in), full(gm), full(qg), full(kg), row(LANES), row(LANES)],
        out_specs=[row(w) for w, _ in outs],
        out_shape=[jax.ShapeDtypeStruct((n, w), dt) for w, dt in outs],
        compiler_params=_cparams(("parallel",)),
        name="even_in_proj",
    )(h, g, w_in, gm, qg, kg, c_tab, s_tab)


def _flash_loop(qs_ref, k_ref, v_ref, m_scr, l_scr, acc_scr, qi, tq, tk, n_rows):
    q0 = qi * tq
    n_full = q0 // tk
    n_tot = (q0 + tq + tk - 1) // tk
    m_scr[...] = jnp.full(m_scr.shape, NEG, F32)
    l_scr[...] = jnp.zeros(l_scr.shape, F32)
    acc_scr[...] = jnp.zeros(acc_scr.shape, F32)

    def step(j, masked):
        ks = pl.multiple_of(j * tk, tk)
        kb = k_ref[pl.ds(ks, tk), :]
        vb = v_ref[pl.ds(ks, tk), :]
        s = lax.dot_general(qs_ref[...], kb, (((1,), (1,)), ((), ())), preferred_element_type=F32)
        if masked:
            row = q0 + lax.broadcasted_iota(jnp.int32, s.shape, 0) % tq
            col = ks + lax.broadcasted_iota(jnp.int32, s.shape, 1)
            s = jnp.where(col <= row, s, NEG)
        m_prev = m_scr[...]
        m_new = jnp.maximum(m_prev, jnp.max(s, axis=1, keepdims=True))
        alpha = jnp.exp(m_prev - m_new)
        p = jnp.exp(s - jnp.tile(m_new, (1, tk // LANES)))
        l_scr[...] = alpha * l_scr[...] + jnp.sum(p, axis=1, keepdims=True)
        acc_scr[...] = alpha * acc_scr[...] + jnp.dot(p.astype(BF16), vb, preferred_element_type=F32)
        m_scr[...] = m_new

    def full_body(j, carry):
        step(j, False)
        return carry

    def diag_body(j, carry):
        step(j, True)
        return carry

    lax.fori_loop(0, n_full, full_body, 0)
    lax.fori_loop(n_full, n_tot, diag_body, 0)


def _diff_attn_kernel(q_ref, k_ref, v_ref, bg_ref, lam_ref, sub_ref, o_ref,
                      qs_scr, m_scr, l_scr, acc_scr, *, tq, tk, lam_init):
    qi = pl.program_id(2)
    q = q_ref[...]
    lane = lax.broadcasted_iota(jnp.int32, q.shape, 1)
    zero = jnp.zeros_like(q)
    qs_scr[0:tq, :] = jnp.where(lane < DK_B, q, zero)
    qs_scr[tq:2 * tq, :] = jnp.where(lane >= DK_B, q, zero)
    _flash_loop(qs_scr, k_ref, v_ref, m_scr, l_scr, acc_scr, qi, tq, tk, 2 * tq)
    lv = lam_ref[...]
    lam = (jnp.exp(jnp.sum(lv[0:1] * lv[1:2], axis=1, keepdims=True))
           - jnp.exp(jnp.sum(lv[2:3] * lv[3:4], axis=1, keepdims=True)) + lam_init)
    o1 = acc_scr[0:tq, :] / l_scr[0:tq, :]
    o2 = acc_scr[tq:2 * tq, :] / l_scr[tq:2 * tq, :]
    o = o1 - lam * o2
    ms = jnp.mean(o * o, axis=-1, keepdims=True)
    o = o * lax.rsqrt(ms + EPS) * sub_ref[...] * (1.0 - lam_init)
    o_ref[...] = (o * _silu(bg_ref[...].astype(F32))).astype(BF16)


def _diff_attention(q, k, v, bgate, lam_vec, subln, batch, seq, lam_init):
    n = q.shape[0]
    tq = min(256, seq)
    tk = min(ATT_TK, seq)
    nq = seq // tq
    qspec = pl.BlockSpec((tq, DV_B), lambda b, h, i: (b * nq + i, h))
    kvspec = pl.BlockSpec((seq, DV_B), lambda b, h, i: (b, h))
    full = lambda a: pl.BlockSpec(a.shape, lambda b, h, i: (0,) * a.ndim)
    return pl.pallas_call(
        functools.partial(_diff_attn_kernel, tq=tq, tk=tk, lam_init=lam_init),
        grid=(batch, H_B, nq),
        in_specs=[qspec, kvspec, kvspec, qspec, full(lam_vec), full(subln)],
        out_specs=qspec,
        out_shape=jax.ShapeDtypeStruct((n, W_B), BF16),
        scratch_shapes=[pltpu.VMEM((2 * tq, DV_B), BF16), pltpu.VMEM((2 * tq, LANES), F32),
                        pltpu.VMEM((2 * tq, LANES), F32), pltpu.VMEM((2 * tq, DV_B), F32)],
        compiler_params=_cparams(("parallel", "parallel", "arbitrary")),
        name="diff_attention",
    )(q, k, v, bgate, lam_vec, subln)


def _mla_attn_kernel(q_ref, k_ref, v_ref, cg_ref, o_ref, m_scr, l_scr, acc_scr, *, tq, tk):
    qi = pl.program_id(2)
    _flash_loop(q_ref, k_ref, v_ref, m_scr, l_scr, acc_scr, qi, tq, tk, tq)
    o = acc_scr[...] / l_scr[...]
    o_ref[...] = (o * _silu(cg_ref[...].astype(F32))).astype(BF16)


def _mla_attention(q, k, v, cgate, batch, seq):
    n = q.shape[0]
    tq = min(512, seq)
    tk = min(ATT_TK, seq)
    nq = seq // tq
    dqk = q.shape[1] // H_C
    qspec = pl.BlockSpec((tq, dqk), lambda b, h, i: (b * nq + i, h))
    kspec = pl.BlockSpec((seq, dqk), lambda b, h, i: (b, h))
    vspec = pl.BlockSpec((seq, DV_C), lambda b, h, i: (b, h))
    ospec = pl.BlockSpec((tq, DV_C), lambda b, h, i: (b * nq + i, h))
    return pl.pallas_call(
        functools.partial(_mla_attn_kernel, tq=tq, tk=tk),
        grid=(batch, H_C, nq),
        in_specs=[qspec, kspec, vspec, ospec],
        out_specs=ospec,
        out_shape=jax.ShapeDtypeStruct((n, W_C), BF16),
        scratch_shapes=[pltpu.VMEM((tq, LANES), F32), pltpu.VMEM((tq, LANES), F32),
                        pltpu.VMEM((tq, DV_C), F32)],
        compiler_params=_cparams(("parallel", "parallel", "arbitrary")),
        name="mla_attention",
    )(q, k, v, cgate)


def _pool_kernel(x_ref, halo_ref, gate_ref, pw_ref, ps_ref, o_ref, buf_a, buf_b, *, tm, seq):
    i = pl.program_id(0)
    t0 = (i * tm) % seq
    x = x_ref[...]
    keep = (t0 > 0).astype(F32)
    buf_a[0:POOL_HALO, :] = halo_ref[...] * keep
    buf_a[POOL_HALO:POOL_HALO + tm, :] = x
    tot = tm + POOL_HALO
    src, dst = buf_a, buf_b
    levels = {}
    start = 0
    for w in (1, 2, 4, 8):
        start += 8
        cur = src[start:tot, :] + src[start - w:tot - w, :]
        dst[start:tot, :] = cur
        levels[2 * w] = cur[POOL_HALO - start:, :]
        src, dst = dst, src
    lane = lax.broadcasted_iota(jnp.int32, (tm, W_A), 1)
    grp = lane // POOL_GC
    win = jnp.where(grp == 0, levels[2], jnp.where(grp == 1, levels[4],
                    jnp.where(grp == 2, levels[8], levels[16])))
    wlane = jnp.where(grp == 0, 2, jnp.where(grp == 1, 4, jnp.where(grp == 2, 8, 16)))
    tpos = t0 + lax.broadcasted_iota(jnp.int32, (tm, W_A), 0)
    cnt = jnp.minimum(tpos + 1, wlane).astype(F32)
    pooled = win / cnt - x
    a = jnp.dot(pooled.astype(BF16), pw_ref[...], preferred_element_type=F32) * ps_ref[...]
    o_ref[...] = (a * _silu(gate_ref[...].astype(F32))).astype(BF16)


def _pool_mixer(a_in, a_gate, pool_w_bd, pool_scale, seq):
    n = a_in.shape[0]
    tm = min(ROW_TILE, seq)
    hb = tm // POOL_HALO
    row = lambda w: pl.BlockSpec((tm, w), lambda i: (i, 0))
    halo = pl.BlockSpec((POOL_HALO, W_A), lambda i: (jnp.maximum(i * hb - 1, 0), 0))
    full = lambda a: pl.BlockSpec(a.shape, lambda i: (0,) * a.ndim)
    return pl.pallas_call(
        functools.partial(_pool_kernel, tm=tm, seq=seq),
        grid=(n // tm,),
        in_specs=[row(W_A), halo, row(W_A), full(pool_w_bd), full(pool_scale)],
        out_specs=row(W_A),
        out_shape=jax.ShapeDtypeStruct((n, W_A), BF16),
        scratch_shapes=[pltpu.VMEM((tm + POOL_HALO, W_A), F32)] * 2,
        compiler_params=_cparams(("parallel",)),
        name="pool_mixer",
    )(a_in, a_in, a_gate, pool_w_bd, pool_scale)


def _out_kernel(h_ref, ma_ref, mb_ref, wo_ref, wg_ref, p_ref, wp_ref, o_ref, *, wa):
    m = (jnp.dot(ma_ref[...], wo_ref[0:wa, :], preferred_element_type=F32)
         + jnp.dot(mb_ref[...], wo_ref[wa:, :], preferred_element_type=F32))
    h1 = h_ref[...] + m
    gate = _sigmoid(jnp.dot(h1.astype(BF16), wg_ref[...], preferred_element_type=F32))
    pp = jnp.dot(p_ref[...].astype(BF16), wp_ref[...], preferred_element_type=F32)
    o_ref[...] = h1 + gate * pp


def _out_proj(h, mix_a, mix_b, w_out, w_gate, p, w_proj):
    n = h.shape[0]
    tm = min(ROW_TILE, n)
    wa = mix_a.shape[1]
    row = lambda w: pl.BlockSpec((tm, w), lambda i: (i, 0))
    full = lambda a: pl.BlockSpec(a.shape, lambda i: (0,) * a.ndim)
    return pl.pallas_call(
        functools.partial(_out_kernel, wa=wa),
        grid=(n // tm,),
        in_specs=[row(D_MODEL), row(wa), row(mix_b.shape[1]), full(w_out), full(w_gate),
                  row(PLE_DIM), full(w_proj)],
        out_specs=row(D_MODEL),
        out_shape=jax.ShapeDtypeStruct((n, D_MODEL), F32),
        compiler_params=_cparams(("parallel",)),
        name="out_proj_ple",
    )(h, mix_a, mix_b, w_out, w_gate, p, w_proj)


O_CQ = 0
O_CKV = O_CQ + Q_LORA
O_KR = O_CKV + KV_LORA
O_CG = O_KR + LANES
O_QKV = O_CG + W_C
O_DG = O_QKV + CONV_CH
O_BA = O_DG + W_D
O_TOT = O_BA + LANES
HEAD_PAD = 2 * LANES


def _odd_in_kernel(h_ref, g_ref, w_ref, qan_ref, wuq_ref, kvan_ref, wukv_ref, qn_ref, kn_ref,
                   c_ref, s_ref, alog_ref, dtb_ref,
                   q_ref, k_ref, v_ref, cg_ref, qkv_ref, dg_ref, gb_ref):
    x = h_ref[...]
    ms = jnp.mean(x * x, axis=-1, keepdims=True)
    yb = (x * lax.rsqrt(ms + EPS) * g_ref[...]).astype(BF16)

    def seg(lo, hi):
        return jnp.dot(yb, w_ref[:, lo:hi], preferred_element_type=F32)

    cg_ref[...] = seg(O_CG, O_QKV).astype(BF16)
    qkv_ref[...] = seg(O_QKV, O_DG).astype(BF16)
    dg_ref[...] = seg(O_DG, O_BA).astype(BF16)

    ba = seg(O_BA, O_TOT)
    beta = _sigmoid(ba)
    g = -jnp.exp(alog_ref[...]) * jax.nn.softplus(ba + dtb_ref[...])
    lane = lax.broadcasted_iota(jnp.int32, ba.shape, 1)
    gb_ref[...] = jnp.where(lane < H_D, beta, g)

    c = c_ref[...]
    s = s_ref[...]
    qn = qn_ref[...]
    kn = kn_ref[...]
    scale = DQK_C ** -0.5

    cq = seg(O_CQ, O_CKV)
    cqn = cq * lax.rsqrt(jnp.mean(cq * cq, axis=-1, keepdims=True) + EPS) * qan_ref[...]
    qu = jnp.dot(cqn.astype(BF16), wuq_ref[...], preferred_element_type=F32)
    for hh in range(H_C):
        nope = qu[:, hh * D_NOPE:(hh + 1) * D_NOPE]
        rope = qu[:, H_C * D_NOPE + hh * LANES:H_C * D_NOPE + (hh + 1) * LANES]
        ss = jnp.sum(nope * nope, axis=-1, keepdims=True) + jnp.sum(rope * rope, axis=-1, keepdims=True)
        r = lax.rsqrt(ss * (1.0 / DQK_C) + EPS) * scale
        q_ref[:, hh * HEAD_PAD:hh * HEAD_PAD + LANES] = (nope * r * qn[:, 0:LANES]).astype(BF16)
        q_ref[:, hh * HEAD_PAD + LANES:(hh + 1) * HEAD_PAD] = _rope_block(
            rope * r * qn[:, LANES:2 * LANES], c, s, D_ROPE // 2).astype(BF16)

    ckv = seg(O_CKV, O_KR)
    ckvn = ckv * lax.rsqrt(jnp.mean(ckv * ckv, axis=-1, keepdims=True) + EPS) * kvan_ref[...]
    kvu = jnp.dot(ckvn.astype(BF16), wukv_ref[...], preferred_element_type=F32)
    kr = seg(O_KR, O_CG)
    kr_ss = jnp.sum(kr * kr, axis=-1, keepdims=True)
    v_ref[...] = kvu[:, H_C * D_NOPE:].astype(BF16)
    for hh in range(H_C):
        nope = kvu[:, hh * D_NOPE:(hh + 1) * D_NOPE]
        ss = jnp.sum(nope * nope, axis=-1, keepdims=True) + kr_ss
        r = lax.rsqrt(ss * (1.0 / DQK_C) + EPS)
        k_ref[:, hh * HEAD_PAD:hh * HEAD_PAD + LANES] = (nope * r * kn[:, 0:LANES]).astype(BF16)
        k_ref[:, hh * HEAD_PAD + LANES:(hh + 1) * HEAD_PAD] = _rope_block(
            kr * r * kn[:, LANES:2 * LANES], c, s, D_ROPE // 2).astype(BF16)


def _odd_in(h, g, w_in, qan, wuq, kvan, wukv, qn, kn, c_tab, s_tab, alog, dtb):
    n = h.shape[0]
    tm = min(ROW_TILE, n)
    row = lambda w: pl.BlockSpec((tm, w), lambda i: (i, 0))
    full = lambda a: pl.BlockSpec(a.shape, lambda i: (0,) * a.ndim)
    outs = [(H_C * HEAD_PAD, BF16), (H_C * HEAD_PAD, BF16), (W_C, BF16), (W_C, BF16),
            (CONV_CH, BF16), (W_D, BF16), (LANES, F32)]
    return pl.pallas_call(
        _odd_in_kernel,
        grid=(n // tm,),
        in_specs=[row(D_MODEL), full(g), full(w_in), full(qan), full(wuq), full(kvan), full(wukv),
                  full(qn), full(kn), row(LANES), row(LANES), full(alog), full(dtb)],
        out_specs=[row(w) for w, _ in outs],
        out_shape=[jax.ShapeDtypeStruct((n, w), dt) for w, dt in outs],
        compiler_params=_cparams(("parallel",)),
        name="odd_in_proj",
    )(h, g, w_in, qan, wuq, kvan, wukv, qn, kn, c_tab, s_tab, alog, dtb)


def _gdn_prep_kernel(x_ref, halo_ref, cw_ref, q_ref, k_ref, v_ref, buf, *, tm, seq):
    i = pl.program_id(0)
    t0 = (i * tm) % seq
    keep = (t0 > 0).astype(F32)
    buf[0:CONV_HALO, :] = halo_ref[...].astype(F32) * keep
    buf[CONV_HALO:CONV_HALO + tm, :] = x_ref[...].astype(F32)
    cw = cw_ref[...]
    y = None
    for j in range(CONV_K):
        off = CONV_HALO - (CONV_K - 1) + j
        term = buf[off:off + tm, :] * cw[j:j + 1, :]
        y = term if y is None else y + term
    y = _silu(y)
    for hh in range(H_D):
        qh = y[:, hh * DK_D:(hh + 1) * DK_D]
        kh = y[:, H_D * DK_D + hh * DK_D:H_D * DK_D + (hh + 1) * DK_D]
        qn = qh * lax.rsqrt(jnp.sum(qh * qh, axis=-1, keepdims=True) + EPS) * (DK_D ** -0.5)
        kn = kh * lax.rsqrt(jnp.sum(kh * kh, axis=-1, keepdims=True) + EPS)
        q_ref[:, hh * DK_D:(hh + 1) * DK_D] = qn.astype(BF16)
        k_ref[:, hh * DK_D:(hh + 1) * DK_D] = kn.astype(BF16)
    v_ref[...] = y[:, 2 * H_D * DK_D:].astype(BF16)


def _gdn_prep(qkv, conv_w, seq):
    n = qkv.shape[0]
    tm = min(ROW_TILE, seq)
    hb = tm // CONV_HALO
    row = lambda w: pl.BlockSpec((tm, w), lambda i: (i, 0))
    halo = pl.BlockSpec((CONV_HALO, CONV_CH), lambda i: (jnp.maximum(i * hb - 1, 0), 0))
    full = lambda a: pl.BlockSpec(a.shape, lambda i: (0,) * a.ndim)
    return pl.pallas_call(
        functools.partial(_gdn_prep_kernel, tm=tm, seq=seq),
        grid=(n // tm,),
        in_specs=[row(CONV_CH), halo, full(conv_w)],
        out_specs=[row(W_D)] * 3,
        out_shape=[jax.ShapeDtypeStruct((n, W_D), BF16)] * 3,
        scratch_shapes=[pltpu.VMEM((tm + CONV_HALO, CONV_CH), F32)],
        compiler_params=_cparams(("parallel",)),
        name="gdn_conv_norm",
    )(qkv, qkv, conv_w)


def _gdn_kernel(q_ref, k_ref, v_ref, gbc_ref, gbr_ref, dg_ref, on_ref, o_ref, s_scr, *, tc):
    @pl.when(pl.program_id(1) == 0)
    def _():
        s_scr[...] = jnp.zeros(s_scr.shape, F32)

    ii = lax.broadcasted_iota(jnp.int32, (CHUNK, CHUNK), 0)
    jj = lax.broadcasted_iota(jnp.int32, (CHUNK, CHUNK), 1)
    incl = ii >= jj
    strict = ii > jj
    eye = jnp.where(ii == jj, 1.0, 0.0).astype(F32)
    low = jnp.where(incl, 1.0, 0.0).astype(F32)
    upp = jnp.where(ii <= jj, 1.0, 0.0).astype(F32)
    on = on_ref[...]

    def chunk_body(c, carry):
        r0 = pl.multiple_of(c * CHUNK, CHUNK)
        rows = pl.ds(r0, CHUNK)
        gbc = gbc_ref[rows, :]
        gcs_col = _split3_dot(low, gbc)
        gcs_row = _split3_dot_l(gbr_ref[c], upp)
        for hh in range(H_D):
            l0 = hh * DK_D
            q = q_ref[rows, l0:l0 + DK_D]
            k = k_ref[rows, l0:l0 + DK_D]
            v = v_ref[rows, l0:l0 + DV_D]
            kf = k.astype(F32)
            beta = gbc[:, hh:hh + 1]
            gcol = gcs_col[:, H_D + hh:H_D + hh + 1]
            grow = gcs_row[H_D + hh:H_D + hh + 1, :]
            glast = gcol[CHUNK - 1:CHUNK, :]
            gamma = jnp.exp(jnp.where(incl, gcol - grow, -jnp.inf))
            kk = _bdot_nt(k, k)
            a = jnp.where(strict, kk * gamma * beta, 0.0)
            t = eye - a
            pw = a
            for _ in range(5):
                pw = _bdot(pw, pw)
                t = t + _bdot(t, pw)
            egc = jnp.exp(gcol)
            u = _bdot(t, v.astype(F32) * beta)
            w = _bdot(t, kf * (beta * egc))
            qk = _bdot_nt(q, k) * gamma
            q_dec = q.astype(F32) * egc
            k_dec = kf * jnp.exp(glast - gcol)
            st = s_scr[hh]
            v_new = u - _bdot(w, st)
            o = _bdot(q_dec, st) + _bdot(qk, v_new)
            s_scr[hh] = st * jnp.exp(glast) + _bdot_tn(k_dec, v_new)
            ms = jnp.mean(o * o, axis=-1, keepdims=True)
            o = o * lax.rsqrt(ms + EPS) * on
            gate = dg_ref[rows, l0:l0 + DV_D].astype(F32)
            o_ref[rows, l0:l0 + DV_D] = (o * _silu(gate)).astype(BF16)
        return carry

    lax.fori_loop(0, tc // CHUNK, chunk_body, 0)


def _gdn(dq, dk, dv, gb, gb_row, d_gate, o_norm, batch, seq):
    n = dq.shape[0]
    tc = min(GDN_TILE, seq)
    nt = seq // tc
    row = pl.BlockSpec((tc, W_D), lambda b, i: (b * nt + i, 0))
    colspec = pl.BlockSpec((tc, LANES), lambda b, i: (b * nt + i, 0))
    rowspec = pl.BlockSpec((tc // CHUNK, 2 * H_D, CHUNK), lambda b, i: (b * nt + i, 0, 0))
    full = lambda a: pl.BlockSpec(a.shape, lambda b, i: (0,) * a.ndim)
    return pl.pallas_call(
        functools.partial(_gdn_kernel, tc=tc),
        grid=(batch, nt),
        in_specs=[row, row, row, colspec, rowspec, row, full(o_norm)],
        out_specs=row,
        out_shape=jax.ShapeDtypeStruct((n, W_D), BF16),
        scratch_shapes=[pltpu.VMEM((H_D, DK_D, DV_D), F32)],
        compiler_params=_cparams(("parallel", "arbitrary")),
        name="gated_delta_rule",
    )(dq, dk, dv, gb, gb_row, d_gate, o_norm)


def _rope_patterns(rot_dim, theta, period):
    half = rot_dim // 2
    inv = jnp.power(jnp.float32(theta), -jnp.arange(half, dtype=F32) * (2.0 / rot_dim))
    lane = np.arange(LANES)
    in_rot = (lane % period) < rot_dim
    idx = jnp.asarray(lane % half)
    inv_l = jnp.where(jnp.asarray(in_rot), inv[idx], 0.0).astype(F32)[None, :]
    sgn = np.where(in_rot, np.where((lane % period) < half, -1.0, 1.0), 0.0).astype(np.float32)[None, :]
    return inv_l, jnp.asarray(sgn)


def _pad_cols(w, width):
    return jnp.pad(w, ((0, 0), (0, width - w.shape[1])))


def _odd_w_in_layout(w):
    cq, ckv, kr, cg, qkv, db, da, dg = jnp.split(
        w, np.cumsum((Q_LORA, KV_LORA, D_ROPE, W_C, CONV_CH, H_D, H_D, W_D))[:-1].tolist(), axis=1)
    return jnp.concatenate([cq, ckv, _pad_cols(kr, LANES), cg, qkv, dg,
                            _pad_cols(jnp.concatenate([db, da], axis=1), LANES)], axis=1)


def _wuq_layout(w):
    w = w.reshape(Q_LORA, H_C, DQK_C)
    nope = w[:, :, :D_NOPE].reshape(Q_LORA, H_C * D_NOPE)
    rope = jnp.pad(w[:, :, D_NOPE:], ((0, 0), (0, 0), (0, LANES - D_ROPE))).reshape(Q_LORA, H_C * LANES)
    return jnp.concatenate([nope, rope], axis=1)


def _wukv_layout(w):
    w = w.reshape(KV_LORA, H_C, D_NOPE + DV_C)
    return jnp.concatenate([w[:, :, :D_NOPE].reshape(KV_LORA, H_C * D_NOPE),
                            w[:, :, D_NOPE:].reshape(KV_LORA, H_C * DV_C)], axis=1)


def _head_gain_layout(g):
    return jnp.pad(g, (0, HEAD_PAD - DQK_C))[None, :].astype(F32)


def kernel(x, p, positions, norm_g, ple_w_gate, ple_w_proj, ev_w_in, ev_pool_w, ev_pool_scale, ev_q_norm, ev_k_norm, ev_lambda, ev_subln, ev_w_out, od_w_in, od_q_a_norm, od_w_uq, od_kv_a_norm, od_w_ukv, od_q_norm, od_k_norm, od_conv_w, od_a_log, od_dt_bias, od_o_norm, od_w_out):
    batch, seq, _ = x.shape
    depth = p.shape[0]
    n = batch * seq
    h = x.reshape(n, D_MODEL)
    pos_b = jnp.broadcast_to(positions.astype(F32).reshape(n, 1), (n, LANES))

    inv_e, sgn_e = _rope_patterns(B_ROT, ROPE_THETA, DK_B)
    ce, se = _rope_tables(pos_b, inv_e, sgn_e)
    inv_o, sgn_o = _rope_patterns(D_ROPE, MLA_THETA, LANES)
    co, so = _rope_tables(pos_b, inv_o, sgn_o)

    gidx = np.arange(W_B) // DK_B
    gmean = jnp.asarray((gidx[:, None] == gidx[None, :]).astype(np.float32) / DK_B).astype(BF16)

    for i in range(depth):
        j = i // 2
        g = norm_g[i][None, :]
        if i % 2 == 0:
            w_in = ev_w_in[j].astype(BF16)
            qg = (jnp.tile(ev_q_norm[j], 2 * H_B) * (DK_B ** -0.5))[None, :]
            kg = jnp.tile(ev_k_norm[j], 2 * H_B)[None, :]
            a_in, a_gate, q, k, v, b_gate = _even_in(h, g, w_in, gmean, qg, kg, ce, se)
            lam_init = 0.8 - 0.6 * math.exp(-0.3 * i)
            ob = _diff_attention(q, k, v, b_gate, ev_lambda[j], ev_subln[j][None, :], batch, seq, lam_init)
            pw = jax.scipy.linalg.block_diag(*[ev_pool_w[j][gi] for gi in range(len(POOL_WINDOWS))]).astype(BF16)
            oa = _pool_mixer(a_in, a_gate, pw, ev_pool_scale[j][None, :], seq)
            w_out = ev_w_out[j].astype(BF16)
        else:
            w_in = _odd_w_in_layout(od_w_in[j]).astype(BF16)
            alog = jnp.zeros((LANES,), F32).at[H_D:2 * H_D].set(od_a_log[j])[None, :]
            dtb = jnp.zeros((LANES,), F32).at[H_D:2 * H_D].set(od_dt_bias[j])[None, :]
            q, k, v, c_gate, qkv, d_gate, gb = _odd_in(
                h, g, w_in, od_q_a_norm[j][None, :], _wuq_layout(od_w_uq[j]).astype(BF16),
                od_kv_a_norm[j][None, :], _wukv_layout(od_w_ukv[j]).astype(BF16),
                _head_gain_layout(od_q_norm[j]), _head_gain_layout(od_k_norm[j]), co, so, alog, dtb)
            oa = _mla_attention(q, k, v, c_gate, batch, seq)
            dq, dk, dv = _gdn_prep(qkv, od_conv_w[j], seq)
            gb_row = gb[:, :2 * H_D].reshape(n // CHUNK, CHUNK, 2 * H_D).transpose(0, 2, 1)
            ob = _gdn(dq, dk, dv, gb, gb_row, d_gate, od_o_norm[j][None, :], batch, seq)
            w_out = od_w_out[j].astype(BF16)
        h = _out_proj(h, oa, ob, w_out, ple_w_gate[i].astype(BF16), p[i].reshape(n, PLE_DIM),
                      ple_w_proj[i].astype(BF16))
    return h.reshape(batch, seq, D_MODEL)
```

```python
import functools
import math

import numpy as np
import jax
import jax.numpy as jnp
from jax import lax
from jax.experimental import pallas as pl
from jax.experimental.pallas import tpu as pltpu

F32 = jnp.float32
BF16 = jnp.bfloat16

D_MODEL = 1024
PLE_DIM = 256
EPS = 1e-6
W_A = 256
POOL_WINDOWS = (2, 4, 8, 16)
POOL_GC = 64
W_B = 768
DV_B = 128
H_B = 6
DK_B = 64
B_ROT = 16
ROPE_THETA = 500000.0
W_C = 512
DV_C = 128
H_C = 4
D_NOPE = 128
D_ROPE = 64
DQK_C = 192
Q_LORA = 256
KV_LORA = 128
MLA_THETA = 10000.0
W_D = 512
DK_D = 128
DV_D = 128
H_D = 4
CONV_K = 4
CHUNK = 64
CONV_CH = H_D * (2 * DK_D + DV_D)

LANES = 128
VMEM_LIMIT = 48 * 1024 * 1024
NEG = -1e30

ROW_TILE = 512
ATT_TK = 512
POOL_HALO = 32
CONV_HALO = 16
GDN_TILE = 512
DIFF_HEADS_PER_STEP = 2
MLA_HEADS_PER_STEP = 2


def _cparams(sem):
    return pltpu.CompilerParams(dimension_semantics=sem, vmem_limit_bytes=VMEM_LIMIT)


def _silu(x):
    return x * (1.0 / (1.0 + jnp.exp(-x)))


def _sigmoid(x):
    return 1.0 / (1.0 + jnp.exp(-x))


def _bdot(a, b):
    return jnp.dot(a.astype(BF16), b.astype(BF16), preferred_element_type=F32)


def _bdot_nt(a, b):
    return lax.dot_general(a.astype(BF16), b.astype(BF16), (((1,), (1,)), ((), ())),
                           preferred_element_type=F32)


def _bdot_tn(a, b):
    return lax.dot_general(a.astype(BF16), b.astype(BF16), (((0,), (0,)), ((), ())),
                           preferred_element_type=F32)


def _split3_dot(a_exact, b):
    b0 = b.astype(BF16)
    r1 = b - b0.astype(F32)
    b1 = r1.astype(BF16)
    b2 = (r1 - b1.astype(F32)).astype(BF16)
    a = a_exact.astype(BF16)
    return (jnp.dot(a, b0, preferred_element_type=F32) + jnp.dot(a, b1, preferred_element_type=F32)
            + jnp.dot(a, b2, preferred_element_type=F32))


def _split3_dot_l(b, a_exact):
    b0 = b.astype(BF16)
    r1 = b - b0.astype(F32)
    b1 = r1.astype(BF16)
    b2 = (r1 - b1.astype(F32)).astype(BF16)
    a = a_exact.astype(BF16)
    return (jnp.dot(b0, a, preferred_element_type=F32) + jnp.dot(b1, a, preferred_element_type=F32)
            + jnp.dot(b2, a, preferred_element_type=F32))


def _rope_table_kernel(pos_ref, inv_ref, sgn_ref, c_ref, s_ref):
    ang = pos_ref[...] * inv_ref[...]
    c_ref[...] = jnp.cos(ang)
    s_ref[...] = jnp.sin(ang) * sgn_ref[...]


def _rope_tables(pos_b, inv, sgn):
    n = pos_b.shape[0]
    tm = min(1024, n)
    row = pl.BlockSpec((tm, LANES), lambda i: (i, 0))
    par = pl.BlockSpec((1, LANES), lambda i: (0, 0))
    return pl.pallas_call(
        _rope_table_kernel,
        grid=(n // tm,),
        in_specs=[row, par, par],
        out_specs=[row, row],
        out_shape=[jax.ShapeDtypeStruct((n, LANES), F32)] * 2,
        compiler_params=_cparams(("parallel",)),
        name="rope_tables",
    )(pos_b, inv, sgn)


def _rope_block(x, c, s, half):
    fwd = pltpu.roll(x, LANES - half, 1)
    bwd = pltpu.roll(x, half, 1)
    lane = lax.broadcasted_iota(jnp.int32, x.shape, 1)
    rot = jnp.where((lane % (2 * half)) < half, fwd, bwd)
    return x * c + rot * s


def _even_in_kernel(h_ref, g_ref, w_ref, gm_ref, qg_ref, kg_ref, c_ref, s_ref,
                    ain_ref, agate_ref, q_ref, k_ref, v_ref, bgate_ref):
    x = h_ref[...]
    ms = jnp.mean(x * x, axis=-1, keepdims=True)
    yb = (x * lax.rsqrt(ms + EPS) * g_ref[...]).astype(BF16)

    def seg(lo, hi):
        return jnp.dot(yb, w_ref[:, lo:hi], preferred_element_type=F32)

    ain_ref[...] = seg(0, W_A)
    agate_ref[...] = seg(W_A, 2 * W_A).astype(BF16)
    c = c_ref[...]
    s = s_ref[...]

    def qk_prep(z, gain_ref, out_ref):
        sq = z * z
        hi = sq.astype(BF16)
        lo = (sq - hi.astype(F32)).astype(BF16)
        gm = gm_ref[...]
        msq = jnp.dot(hi, gm, preferred_element_type=F32) + jnp.dot(lo, gm, preferred_element_type=F32)
        zn = z * lax.rsqrt(msq + EPS) * gain_ref[...]
        for j in range(W_B // LANES):
            blk = zn[:, j * LANES:(j + 1) * LANES]
            out_ref[:, j * LANES:(j + 1) * LANES] = _rope_block(blk, c, s, B_ROT // 2).astype(BF16)

    o = 2 * W_A
    qk_prep(seg(o, o + W_B), qg_ref, q_ref)
    qk_prep(seg(o + W_B, o + 2 * W_B), kg_ref, k_ref)
    v_ref[...] = seg(o + 2 * W_B, o + 3 * W_B).astype(BF16)
    bgate_ref[...] = seg(o + 3 * W_B, o + 4 * W_B).astype(BF16)


def _even_in(h, g, w_in, gm, qg, kg, c_tab, s_tab):
    n = h.shape[0]
    tm = min(ROW_TILE, n)
    wtot = w_in.shape[1]
    row = lambda w: pl.BlockSpec((tm, w), lambda i: (i, 0))
    full = lambda a: pl.BlockSpec(a.shape, lambda i: (0,) * a.ndim)
    outs = [(W_A, F32), (W_A, BF16), (W_B, BF16), (W_B, BF16), (W_B, BF16), (W_B, BF16)]
    return pl.pallas_call(
        _even_in_kernel,
        grid=(n // tm,),
        in_specs=[row(D_MODEL), full(g), full(w_in), full(gm), full(qg), full(kg), row(LANES), row(LANES)],
        out_specs=[row(w) for w, _ in outs],
        out_shape=[jax.ShapeDtypeStruct((n, w), dt) for w, dt in outs],
        compiler_params=_cparams(("parallel",)),
        name="even_in_proj",
    )(h, g, w_in, gm, qg, kg, c_tab, s_tab)


def _flash_loop(qs_ref, k_ref, v_ref, m_scr, l_scr, acc_scr, qi, tq, tk, heads, dqk, dv):
    q0 = qi * tq
    n_full = q0 // tk
    n_tot = (q0 + tq + tk - 1) // tk
    m_scr[...] = jnp.full(m_scr.shape, NEG, F32)
    l_scr[...] = jnp.zeros(l_scr.shape, F32)
    acc_scr[...] = jnp.zeros(acc_scr.shape, F32)

    def step(j, masked):
        ks = pl.multiple_of(j * tk, tk)
        for g in range(heads):
            kb = k_ref[pl.ds(ks, tk), g * dqk:(g + 1) * dqk]
            vb = v_ref[pl.ds(ks, tk), g * dv:(g + 1) * dv]
            s = lax.dot_general(qs_ref[g], kb, (((1,), (1,)), ((), ())), preferred_element_type=F32)
            if masked:
                row = q0 + lax.broadcasted_iota(jnp.int32, s.shape, 0) % tq
                col = ks + lax.broadcasted_iota(jnp.int32, s.shape, 1)
                s = jnp.where(col <= row, s, NEG)
            m_prev = m_scr[g]
            m_new = jnp.maximum(m_prev, jnp.max(s, axis=1, keepdims=True))
            alpha = jnp.exp(m_prev - m_new)
            p = jnp.exp(s - jnp.tile(m_new, (1, tk // LANES)))
            l_scr[g] = alpha * l_scr[g] + jnp.sum(p, axis=1, keepdims=True)
            acc_scr[g] = alpha * acc_scr[g] + jnp.dot(p.astype(BF16), vb, preferred_element_type=F32)
            m_scr[g] = m_new

    def full_body(j, carry):
        step(j, False)
        return carry

    def diag_body(j, carry):
        step(j, True)
        return carry

    lax.fori_loop(0, n_full, full_body, 0)
    lax.fori_loop(n_full, n_tot, diag_body, 0)


def _diff_attn_kernel(q_ref, k_ref, v_ref, bg_ref, lam_ref, sub_ref, o_ref,
                      qs_scr, m_scr, l_scr, acc_scr, *, tq, tk, heads, lam_init):
    qi = pl.program_id(2)
    for g in range(heads):
        q = q_ref[:, g * DV_B:(g + 1) * DV_B]
        lane = lax.broadcasted_iota(jnp.int32, q.shape, 1)
        zero = jnp.zeros_like(q)
        qs_scr[g, 0:tq, :] = jnp.where(lane < DK_B, q, zero)
        qs_scr[g, tq:2 * tq, :] = jnp.where(lane >= DK_B, q, zero)
    _flash_loop(qs_scr, k_ref, v_ref, m_scr, l_scr, acc_scr, qi, tq, tk, heads, DV_B, DV_B)
    lv = lam_ref[...]
    lam = (jnp.exp(jnp.sum(lv[0:1] * lv[1:2], axis=1, keepdims=True))
           - jnp.exp(jnp.sum(lv[2:3] * lv[3:4], axis=1, keepdims=True)) + lam_init)
    for g in range(heads):
        o1 = acc_scr[g, 0:tq, :] / l_scr[g, 0:tq, :]
        o2 = acc_scr[g, tq:2 * tq, :] / l_scr[g, tq:2 * tq, :]
        o = o1 - lam * o2
        ms = jnp.mean(o * o, axis=-1, keepdims=True)
        o = o * lax.rsqrt(ms + EPS) * sub_ref[...] * (1.0 - lam_init)
        gate = bg_ref[:, g * DV_B:(g + 1) * DV_B].astype(F32)
        o_ref[:, g * DV_B:(g + 1) * DV_B] = (o * _silu(gate)).astype(BF16)


def _diff_attention(q, k, v, bgate, lam_vec, subln, batch, seq, lam_init):
    n = q.shape[0]
    tq = min(256, seq)
    tk = min(ATT_TK, seq)
    nq = seq // tq
    heads = DIFF_HEADS_PER_STEP
    wid = heads * DV_B
    qspec = pl.BlockSpec((tq, wid), lambda b, h, i: (b * nq + i, h))
    kvspec = pl.BlockSpec((seq, wid), lambda b, h, i: (b, h))
    full = lambda a: pl.BlockSpec(a.shape, lambda b, h, i: (0,) * a.ndim)
    return pl.pallas_call(
        functools.partial(_diff_attn_kernel, tq=tq, tk=tk, heads=heads, lam_init=lam_init),
        grid=(batch, H_B // heads, nq),
        in_specs=[qspec, kvspec, kvspec, qspec, full(lam_vec), full(subln)],
        out_specs=qspec,
        out_shape=jax.ShapeDtypeStruct((n, W_B), BF16),
        scratch_shapes=[pltpu.VMEM((heads, 2 * tq, DV_B), BF16), pltpu.VMEM((heads, 2 * tq, LANES), F32),
                        pltpu.VMEM((heads, 2 * tq, LANES), F32), pltpu.VMEM((heads, 2 * tq, DV_B), F32)],
        compiler_params=_cparams(("parallel", "parallel", "arbitrary")),
        name="diff_attention",
    )(q, k, v, bgate, lam_vec, subln)


def _mla_attn_kernel(q_ref, k_ref, v_ref, cg_ref, o_ref, qs_scr, m_scr, l_scr, acc_scr, *, tq, tk, heads):
    qi = pl.program_id(2)
    for g in range(heads):
        qs_scr[g] = q_ref[:, g * HEAD_PAD:(g + 1) * HEAD_PAD]
    _flash_loop(qs_scr, k_ref, v_ref, m_scr, l_scr, acc_scr, qi, tq, tk, heads, HEAD_PAD, DV_C)
    for g in range(heads):
        o = acc_scr[g] / l_scr[g]
        gate = cg_ref[:, g * DV_C:(g + 1) * DV_C].astype(F32)
        o_ref[:, g * DV_C:(g + 1) * DV_C] = (o * _silu(gate)).astype(BF16)


def _mla_attention(q, k, v, cgate, batch, seq):
    n = q.shape[0]
    tq = min(512, seq)
    tk = min(ATT_TK, seq)
    nq = seq // tq
    heads = MLA_HEADS_PER_STEP
    qspec = pl.BlockSpec((tq, heads * HEAD_PAD), lambda b, h, i: (b * nq + i, h))
    kspec = pl.BlockSpec((seq, heads * HEAD_PAD), lambda b, h, i: (b, h))
    vspec = pl.BlockSpec((seq, heads * DV_C), lambda b, h, i: (b, h))
    ospec = pl.BlockSpec((tq, heads * DV_C), lambda b, h, i: (b * nq + i, h))
    return pl.pallas_call(
        functools.partial(_mla_attn_kernel, tq=tq, tk=tk, heads=heads),
        grid=(batch, H_C // heads, nq),
        in_specs=[qspec, kspec, vspec, ospec],
        out_specs=ospec,
        out_shape=jax.ShapeDtypeStruct((n, W_C), BF16),
        scratch_shapes=[pltpu.VMEM((heads, tq, HEAD_PAD), BF16), pltpu.VMEM((heads, tq, LANES), F32),
                        pltpu.VMEM((heads, tq, LANES), F32), pltpu.VMEM((heads, tq, DV_C), F32)],
        compiler_params=_cparams(("parallel", "parallel", "arbitrary")),
        name="mla_attention",
    )(q, k, v, cgate)


def _pool_kernel(x_ref, halo_ref, gate_ref, pw_ref, ps_ref, o_ref, buf_a, buf_b, *, tm, seq):
    i = pl.program_id(0)
    t0 = (i * tm) % seq
    x = x_ref[...]
    keep = (t0 > 0).astype(F32)
    buf_a[0:POOL_HALO, :] = halo_ref[...] * keep
    buf_a[POOL_HALO:POOL_HALO + tm, :] = x
    tot = tm + POOL_HALO
    src, dst = buf_a, buf_b
    levels = {}
    start = 0
    for w in (1, 2, 4, 8):
        start += 8
        cur = src[start:tot, :] + src[start - w:tot - w, :]
        dst[start:tot, :] = cur
        levels[2 * w] = cur[POOL_HALO - start:, :]
        src, dst = dst, src
    lane = lax.broadcasted_iota(jnp.int32, (tm, W_A), 1)
    grp = lane // POOL_GC
    win = jnp.where(grp == 0, levels[2], jnp.where(grp == 1, levels[4],
                    jnp.where(grp == 2, levels[8], levels[16])))
    wlane = jnp.where(grp == 0, 2, jnp.where(grp == 1, 4, jnp.where(grp == 2, 8, 16)))
    tpos = t0 + lax.broadcasted_iota(jnp.int32, (tm, W_A), 0)
    cnt = jnp.minimum(tpos + 1, wlane).astype(F32)
    pooled = win / cnt - x
    a = jnp.dot(pooled.astype(BF16), pw_ref[...], preferred_element_type=F32) * ps_ref[...]
    o_ref[...] = (a * _silu(gate_ref[...].astype(F32))).astype(BF16)


def _pool_mixer(a_in, a_gate, pool_w_bd, pool_scale, seq):
    n = a_in.shape[0]
    tm = min(ROW_TILE, seq)
    hb = tm // POOL_HALO
    row = lambda w: pl.BlockSpec((tm, w), lambda i: (i, 0))
    halo = pl.BlockSpec((POOL_HALO, W_A), lambda i: (jnp.maximum(i * hb - 1, 0), 0))
    full = lambda a: pl.BlockSpec(a.shape, lambda i: (0,) * a.ndim)
    return pl.pallas_call(
        functools.partial(_pool_kernel, tm=tm, seq=seq),
        grid=(n // tm,),
        in_specs=[row(W_A), halo, row(W_A), full(pool_w_bd), full(pool_scale)],
        out_specs=row(W_A),
        out_shape=jax.ShapeDtypeStruct((n, W_A), BF16),
        scratch_shapes=[pltpu.VMEM((tm + POOL_HALO, W_A), F32)] * 2,
        compiler_params=_cparams(("parallel",)),
        name="pool_mixer",
    )(a_in, a_in, a_gate, pool_w_bd, pool_scale)


def _out_kernel(h_ref, ma_ref, mb_ref, wo_ref, wg_ref, p_ref, wp_ref, o_ref, *, wa):
    m = (jnp.dot(ma_ref[...], wo_ref[0:wa, :], preferred_element_type=F32)
         + jnp.dot(mb_ref[...], wo_ref[wa:, :], preferred_element_type=F32))
    h1 = h_ref[...] + m
    gate = _sigmoid(jnp.dot(h1.astype(BF16), wg_ref[...], preferred_element_type=F32))
    pp = jnp.dot(p_ref[...].astype(BF16), wp_ref[...], preferred_element_type=F32)
    o_ref[...] = h1 + gate * pp


def _out_proj(h, mix_a, mix_b, w_out, w_gate, p, w_proj):
    n = h.shape[0]
    tm = min(ROW_TILE, n)
    wa = mix_a.shape[1]
    row = lambda w: pl.BlockSpec((tm, w), lambda i: (i, 0))
    full = lambda a: pl.BlockSpec(a.shape, lambda i: (0,) * a.ndim)
    return pl.pallas_call(
        functools.partial(_out_kernel, wa=wa),
        grid=(n // tm,),
        in_specs=[row(D_MODEL), row(wa), row(mix_b.shape[1]), full(w_out), full(w_gate),
                  row(PLE_DIM), full(w_proj)],
        out_specs=row(D_MODEL),
        out_shape=jax.ShapeDtypeStruct((n, D_MODEL), F32),
        compiler_params=_cparams(("parallel",)),
        name="out_proj_ple",
    )(h, mix_a, mix_b, w_out, w_gate, p, w_proj)


O_CQ = 0
O_CKV = O_CQ + Q_LORA
O_KR = O_CKV + KV_LORA
O_CG = O_KR + LANES
O_QKV = O_CG + W_C
O_DG = O_QKV + CONV_CH
O_BA = O_DG + W_D
O_TOT = O_BA + LANES
HEAD_PAD = 2 * LANES


def _odd_in_kernel(h_ref, g_ref, w_ref, qan_ref, wuq_ref, kvan_ref, wukv_ref, qn_ref, kn_ref,
                   c_ref, s_ref, alog_ref, dtb_ref,
                   q_ref, k_ref, v_ref, cg_ref, qkv_ref, dg_ref, gb_ref):
    x = h_ref[...]
    ms = jnp.mean(x * x, axis=-1, keepdims=True)
    yb = (x * lax.rsqrt(ms + EPS) * g_ref[...]).astype(BF16)

    def seg(lo, hi):
        return jnp.dot(yb, w_ref[:, lo:hi], preferred_element_type=F32)

    cg_ref[...] = seg(O_CG, O_QKV).astype(BF16)
    qkv_ref[...] = seg(O_QKV, O_DG).astype(BF16)
    dg_ref[...] = seg(O_DG, O_BA).astype(BF16)

    ba = seg(O_BA, O_TOT)
    beta = _sigmoid(ba)
    g = -jnp.exp(alog_ref[...]) * jax.nn.softplus(ba + dtb_ref[...])
    lane = lax.broadcasted_iota(jnp.int32, ba.shape, 1)
    gb_ref[...] = jnp.where(lane < H_D, beta, g)

    c = c_ref[...]
    s = s_ref[...]
    qn = qn_ref[...]
    kn = kn_ref[...]
    scale = DQK_C ** -0.5

    cq = seg(O_CQ, O_CKV)
    cqn = cq * lax.rsqrt(jnp.mean(cq * cq, axis=-1, keepdims=True) + EPS) * qan_ref[...]
    qu = jnp.dot(cqn.astype(BF16), wuq_ref[...], preferred_element_type=F32)
    for hh in range(H_C):
        nope = qu[:, hh * D_NOPE:(hh + 1) * D_NOPE]
        rope = qu[:, H_C * D_NOPE + hh * LANES:H_C * D_NOPE + (hh + 1) * LANES]
        ss = jnp.sum(nope * nope, axis=-1, keepdims=True) + jnp.sum(rope * rope, axis=-1, keepdims=True)
        r = lax.rsqrt(ss * (1.0 / DQK_C) + EPS) * scale
        q_ref[:, hh * HEAD_PAD:hh * HEAD_PAD + LANES] = (nope * r * qn[:, 0:LANES]).astype(BF16)
        q_ref[:, hh * HEAD_PAD + LANES:(hh + 1) * HEAD_PAD] = _rope_block(
            rope * r * qn[:, LANES:2 * LANES], c, s, D_ROPE // 2).astype(BF16)

    ckv = seg(O_CKV, O_KR)
    ckvn = ckv * lax.rsqrt(jnp.mean(ckv * ckv, axis=-1, keepdims=True) + EPS) * kvan_ref[...]
    kvu = jnp.dot(ckvn.astype(BF16), wukv_ref[...], preferred_element_type=F32)
    kr = seg(O_KR, O_CG)
    kr_ss = jnp.sum(kr * kr, axis=-1, keepdims=True)
    v_ref[...] = kvu[:, H_C * D_NOPE:].astype(BF16)
    for hh in range(H_C):
        nope = kvu[:, hh * D_NOPE:(hh + 1) * D_NOPE]
        ss = jnp.sum(nope * nope, axis=-1, keepdims=True) + kr_ss
        r = lax.rsqrt(ss * (1.0 / DQK_C) + EPS)
        k_ref[:, hh * HEAD_PAD:hh * HEAD_PAD + LANES] = (nope * r * kn[:, 0:LANES]).astype(BF16)
        k_ref[:, hh * HEAD_PAD + LANES:(hh + 1) * HEAD_PAD] = _rope_block(
            kr * r * kn[:, LANES:2 * LANES], c, s, D_ROPE // 2).astype(BF16)


def _odd_in(h, g, w_in, qan, wuq, kvan, wukv, qn, kn, c_tab, s_tab, alog, dtb):
    n = h.shape[0]
    tm = min(ROW_TILE, n)
    row = lambda w: pl.BlockSpec((tm, w), lambda i: (i, 0))
    full = lambda a: pl.BlockSpec(a.shape, lambda i: (0,) * a.ndim)
    outs = [(H_C * HEAD_PAD, BF16), (H_C * HEAD_PAD, BF16), (W_C, BF16), (W_C, BF16),
            (CONV_CH, BF16), (W_D, BF16), (LANES, F32)]
    return pl.pallas_call(
        _odd_in_kernel,
        grid=(n // tm,),
        in_specs=[row(D_MODEL), full(g), full(w_in), full(qan), full(wuq), full(kvan), full(wukv),
                  full(qn), full(kn), row(LANES), row(LANES), full(alog), full(dtb)],
        out_specs=[row(w) for w, _ in outs],
        out_shape=[jax.ShapeDtypeStruct((n, w), dt) for w, dt in outs],
        compiler_params=_cparams(("parallel",)),
        name="odd_in_proj",
    )(h, g, w_in, qan, wuq, kvan, wukv, qn, kn, c_tab, s_tab, alog, dtb)


def _gdn_prep_kernel(x_ref, halo_ref, cw_ref, q_ref, k_ref, v_ref, buf, *, tm, seq):
    i = pl.program_id(0)
    t0 = (i * tm) % seq
    keep = (t0 > 0).astype(F32)
    buf[0:CONV_HALO, :] = halo_ref[...].astype(F32) * keep
    buf[CONV_HALO:CONV_HALO + tm, :] = x_ref[...].astype(F32)
    cw = cw_ref[...]
    y = None
    for j in range(CONV_K):
        off = CONV_HALO - (CONV_K - 1) + j
        term = buf[off:off + tm, :] * cw[j:j + 1, :]
        y = term if y is None else y + term
    y = _silu(y)
    for hh in range(H_D):
        qh = y[:, hh * DK_D:(hh + 1) * DK_D]
        kh = y[:, H_D * DK_D + hh * DK_D:H_D * DK_D + (hh + 1) * DK_D]
        qn = qh * lax.rsqrt(jnp.sum(qh * qh, axis=-1, keepdims=True) + EPS) * (DK_D ** -0.5)
        kn = kh * lax.rsqrt(jnp.sum(kh * kh, axis=-1, keepdims=True) + EPS)
        q_ref[:, hh * DK_D:(hh + 1) * DK_D] = qn.astype(BF16)
        k_ref[:, hh * DK_D:(hh + 1) * DK_D] = kn.astype(BF16)
    v_ref[...] = y[:, 2 * H_D * DK_D:].astype(BF16)


def _gdn_prep(qkv, conv_w, seq):
    n = qkv.shape[0]
    tm = min(ROW_TILE, seq)
    hb = tm // CONV_HALO
    row = lambda w: pl.BlockSpec((tm, w), lambda i: (i, 0))
    halo = pl.BlockSpec((CONV_HALO, CONV_CH), lambda i: (jnp.maximum(i * hb - 1, 0), 0))
    full = lambda a: pl.BlockSpec(a.shape, lambda i: (0,) * a.ndim)
    return pl.pallas_call(
        functools.partial(_gdn_prep_kernel, tm=tm, seq=seq),
        grid=(n // tm,),
        in_specs=[row(CONV_CH), halo, full(conv_w)],
        out_specs=[row(W_D)] * 3,
        out_shape=[jax.ShapeDtypeStruct((n, W_D), BF16)] * 3,
        scratch_shapes=[pltpu.VMEM((tm + CONV_HALO, CONV_CH), F32)],
        compiler_params=_cparams(("parallel",)),
        name="gdn_conv_norm",
    )(qkv, qkv, conv_w)


def _gdn_kernel(q_ref, k_ref, v_ref, gbc_ref, gbr_ref, dg_ref, on_ref, o_ref, s_scr, *, tc):
    @pl.when(pl.program_id(1) == 0)
    def _():
        s_scr[...] = jnp.zeros(s_scr.shape, F32)

    ii = lax.broadcasted_iota(jnp.int32, (CHUNK, CHUNK), 0)
    jj = lax.broadcasted_iota(jnp.int32, (CHUNK, CHUNK), 1)
    incl = ii >= jj
    strict = ii > jj
    eye = jnp.where(ii == jj, 1.0, 0.0).astype(F32)
    low = jnp.where(incl, 1.0, 0.0).astype(F32)
    upp = jnp.where(ii <= jj, 1.0, 0.0).astype(F32)
    on = on_ref[...]

    def chunk_body(c, carry):
        r0 = pl.multiple_of(c * CHUNK, CHUNK)
        rows = pl.ds(r0, CHUNK)
        gbc = gbc_ref[rows, :]
        gcs_col = _split3_dot(low, gbc)
        gcs_row = _split3_dot_l(gbr_ref[c], upp)
        for hh in range(H_D):
            l0 = hh * DK_D
            q = q_ref[rows, l0:l0 + DK_D]
            k = k_ref[rows, l0:l0 + DK_D]
            v = v_ref[rows, l0:l0 + DV_D]
            kf = k.astype(F32)
            beta = gbc[:, hh:hh + 1]
            gcol = gcs_col[:, H_D + hh:H_D + hh + 1]
            grow = gcs_row[H_D + hh:H_D + hh + 1, :]
            glast = gcol[CHUNK - 1:CHUNK, :]
            gamma = jnp.exp(jnp.where(incl, gcol - grow, -jnp.inf))
            kk = _bdot_nt(k, k)
            a = jnp.where(strict, kk * gamma * beta, 0.0)
            t = eye - a
            pw = a
            for _ in range(5):
                pw = _bdot(pw, pw)
                t = t + _bdot(t, pw)
            egc = jnp.exp(gcol)
            u = _bdot(t, v.astype(F32) * beta)
            w = _bdot(t, kf * (beta * egc))
            qk = _bdot_nt(q, k) * gamma
            q_dec = q.astype(F32) * egc
            k_dec = kf * jnp.exp(glast - gcol)
            st = s_scr[hh]
            v_new = u - _bdot(w, st)
            o = _bdot(q_dec, st) + _bdot(qk, v_new)
            s_scr[hh] = st * jnp.exp(glast) + _bdot_tn(k_dec, v_new)
            ms = jnp.mean(o * o, axis=-1, keepdims=True)
            o = o * lax.rsqrt(ms + EPS) * on
            gate = dg_ref[rows, l0:l0 + DV_D].astype(F32)
            o_ref[rows, l0:l0 + DV_D] = (o * _silu(gate)).astype(BF16)
        return carry

    lax.fori_loop(0, tc // CHUNK, chunk_body, 0)


def _gdn(dq, dk, dv, gb, gb_row, d_gate, o_norm, batch, seq):
    n = dq.shape[0]
    tc = min(GDN_TILE, seq)
    nt = seq // tc
    row = pl.BlockSpec((tc, W_D), lambda b, i: (b * nt + i, 0))
    colspec = pl.BlockSpec((tc, LANES), lambda b, i: (b * nt + i, 0))
    rowspec = pl.BlockSpec((tc // CHUNK, 2 * H_D, CHUNK), lambda b, i: (b * nt + i, 0, 0))
    full = lambda a: pl.BlockSpec(a.shape, lambda b, i: (0,) * a.ndim)
    return pl.pallas_call(
        functools.partial(_gdn_kernel, tc=tc),
        grid=(batch, nt),
        in_specs=[row, row, row, colspec, rowspec, row, full(o_norm)],
        out_specs=row,
        out_shape=jax.ShapeDtypeStruct((n, W_D), BF16),
        scratch_shapes=[pltpu.VMEM((H_D, DK_D, DV_D), F32)],
        compiler_params=_cparams(("parallel", "arbitrary")),
        name="gated_delta_rule",
    )(dq, dk, dv, gb, gb_row, d_gate, o_norm)


def _rope_patterns(rot_dim, theta, period):
    half = rot_dim // 2
    inv = jnp.power(jnp.float32(theta), -jnp.arange(half, dtype=F32) * (2.0 / rot_dim))
    lane = np.arange(LANES)
    in_rot = (lane % period) < rot_dim
    idx = jnp.asarray(lane % half)
    inv_l = jnp.where(jnp.asarray(in_rot), inv[idx], 0.0).astype(F32)[None, :]
    sgn = np.where(in_rot, np.where((lane % period) < half, -1.0, 1.0), 0.0).astype(np.float32)[None, :]
    return inv_l, jnp.asarray(sgn)


def _pad_cols(w, width):
    return jnp.pad(w, ((0, 0), (0, width - w.shape[1])))


def _odd_w_in_layout(w):
    cq, ckv, kr, cg, qkv, db, da, dg = jnp.split(
        w, np.cumsum((Q_LORA, KV_LORA, D_ROPE, W_C, CONV_CH, H_D, H_D, W_D))[:-1].tolist(), axis=1)
    return jnp.concatenate([cq, ckv, _pad_cols(kr, LANES), cg, qkv, dg,
                            _pad_cols(jnp.concatenate([db, da], axis=1), LANES)], axis=1)


def _wuq_layout(w):
    w = w.reshape(Q_LORA, H_C, DQK_C)
    nope = w[:, :, :D_NOPE].reshape(Q_LORA, H_C * D_NOPE)
    rope = jnp.pad(w[:, :, D_NOPE:], ((0, 0), (0, 0), (0, LANES - D_ROPE))).reshape(Q_LORA, H_C * LANES)
    return jnp.concatenate([nope, rope], axis=1)


def _wukv_layout(w):
    w = w.reshape(KV_LORA, H_C, D_NOPE + DV_C)
    return jnp.concatenate([w[:, :, :D_NOPE].reshape(KV_LORA, H_C * D_NOPE),
                            w[:, :, D_NOPE:].reshape(KV_LORA, H_C * DV_C)], axis=1)


def _head_gain_layout(g):
    return jnp.pad(g, (0, HEAD_PAD - DQK_C))[None, :].astype(F32)


def kernel(x, p, positions, norm_g, ple_w_gate, ple_w_proj, ev_w_in, ev_pool_w, ev_pool_scale, ev_q_norm, ev_k_norm, ev_lambda, ev_subln, ev_w_out, od_w_in, od_q_a_norm, od_w_uq, od_kv_a_norm, od_w_ukv, od_q_norm, od_k_norm, od_conv_w, od_a_log, od_dt_bias, od_o_norm, od_w_out):
    batch, seq, _ = x.shape
    depth = p.shape[0]
    n = batch * seq
    h = x.reshape(n, D_MODEL)
    pos_b = jnp.broadcast_to(positions.astype(F32).reshape(n, 1), (n, LANES))

    inv_e, sgn_e = _rope_patterns(B_ROT, ROPE_THETA, DK_B)
    ce, se = _rope_tables(pos_b, inv_e, sgn_e)
    inv_o, sgn_o = _rope_patterns(D_ROPE, MLA_THETA, LANES)
    co, so = _rope_tables(pos_b, inv_o, sgn_o)

    gidx = np.arange(W_B) // DK_B
    gmean = jnp.asarray((gidx[:, None] == gidx[None, :]).astype(np.float32) / DK_B).astype(BF16)

    for i in range(depth):
        j = i // 2
        g = norm_g[i][None, :]
        if i % 2 == 0:
            w_in = ev_w_in[j].astype(BF16)
            qg = (jnp.tile(ev_q_norm[j], 2 * H_B) * (DK_B ** -0.5))[None, :]
            kg = jnp.tile(ev_k_norm[j], 2 * H_B)[None, :]
            a_in, a_gate, q, k, v, b_gate = _even_in(h, g, w_in, gmean, qg, kg, ce, se)
            lam_init = 0.8 - 0.6 * math.exp(-0.3 * i)
            ob = _diff_attention(q, k, v, b_gate, ev_lambda[j], ev_subln[j][None, :], batch, seq, lam_init)
            pw = jax.scipy.linalg.block_diag(*[ev_pool_w[j][gi] for gi in range(len(POOL_WINDOWS))]).astype(BF16)
            oa = _pool_mixer(a_in, a_gate, pw, ev_pool_scale[j][None, :], seq)
            w_out = ev_w_out[j].astype(BF16)
        else:
            w_in = _odd_w_in_layout(od_w_in[j]).astype(BF16)
            alog = jnp.zeros((LANES,), F32).at[H_D:2 * H_D].set(od_a_log[j])[None, :]
            dtb = jnp.zeros((LANES,), F32).at[H_D:2 * H_D].set(od_dt_bias[j])[None, :]
            q, k, v, c_gate, qkv, d_gate, gb = _odd_in(
                h, g, w_in, od_q_a_norm[j][None, :], _wuq_layout(od_w_uq[j]).astype(BF16),
                od_kv_a_norm[j][None, :], _wukv_layout(od_w_ukv[j]).astype(BF16),
                _head_gain_layout(od_q_norm[j]), _head_gain_layout(od_k_norm[j]), co, so, alog, dtb)
            oa = _mla_attention(q, k, v, c_gate, batch, seq)
            dq, dk, dv = _gdn_prep(qkv, od_conv_w[j], seq)
            gb_row = gb[:, :2 * H_D].reshape(n // CHUNK, CHUNK, 2 * H_D).transpose(0, 2, 1)
            ob = _gdn(dq, dk, dv, gb, gb_row, d_gate, od_o_norm[j][None, :], batch, seq)
            w_out = od_w_out[j].astype(BF16)
        h = _out_proj(h, oa, ob, w_out, ple_w_gate[i].astype(BF16), p[i].reshape(n, PLE_DIM),
                      ple_w_proj[i].astype(BF16))
    return h.reshape(batch, seq, D_MODEL)
```

```python
import functools
import math

import numpy as np
import jax
import jax.numpy as jnp
from jax import lax
from jax.experimental import pallas as pl
from jax.experimental.pallas import tpu as pltpu

F32 = jnp.float32
BF16 = jnp.bfloat16

D_MODEL = 1024
PLE_DIM = 256
EPS = 1e-6
W_A = 256
POOL_WINDOWS = (2, 4, 8, 16)
POOL_GC = 64
W_B = 768
DV_B = 128
H_B = 6
DK_B = 64
B_ROT = 16
ROPE_THETA = 500000.0
W_C = 512
DV_C = 128
H_C = 4
D_NOPE = 128
D_ROPE = 64
DQK_C = 192
Q_LORA = 256
KV_LORA = 128
MLA_THETA = 10000.0
W_D = 512
DK_D = 128
DV_D = 128
H_D = 4
CONV_K = 4
CHUNK = 64
CONV_CH = H_D * (2 * DK_D + DV_D)

LANES = 128
VMEM_LIMIT = 48 * 1024 * 1024
NEG = -1e30
LOG2E = math.log2(math.e)

ROW_TILE = 512
ATT_TK = 512
POOL_HALO = 32
CONV_HALO = 16
GDN_PREP_TILE = 256
GDN_SCAN_TILE = 256
DIFF_HEADS_PER_STEP = 2
MLA_HEADS_PER_STEP = 2
DIFF_TQ = 512
MLA_TQ = 1024


def _cparams(sem):
    return pltpu.CompilerParams(dimension_semantics=sem, vmem_limit_bytes=VMEM_LIMIT)


def _silu(x):
    return x * (1.0 / (1.0 + jnp.exp(-x)))


def _sigmoid(x):
    return 1.0 / (1.0 + jnp.exp(-x))


def _bdot(a, b):
    return jnp.dot(a.astype(BF16), b.astype(BF16), preferred_element_type=F32)


def _bdot_nt(a, b):
    return lax.dot_general(a.astype(BF16), b.astype(BF16), (((1,), (1,)), ((), ())),
                           preferred_element_type=F32)


def _split3_dot(a_exact, b):
    b0 = b.astype(BF16)
    r1 = b - b0.astype(F32)
    b1 = r1.astype(BF16)
    b2 = (r1 - b1.astype(F32)).astype(BF16)
    a = a_exact.astype(BF16)
    return (jnp.dot(a, b0, preferred_element_type=F32) + jnp.dot(a, b1, preferred_element_type=F32)
            + jnp.dot(a, b2, preferred_element_type=F32))


def _split3_dot_nt(a_exact, b):
    b0 = b.astype(BF16)
    r1 = b - b0.astype(F32)
    b1 = r1.astype(BF16)
    b2 = (r1 - b1.astype(F32)).astype(BF16)
    return _bdot_nt(a_exact, b0) + _bdot_nt(a_exact, b1) + _bdot_nt(a_exact, b2)


def _rope_table_kernel(pos_ref, inv_ref, sgn_ref, c_ref, s_ref):
    ang = pos_ref[...] * inv_ref[...]
    c_ref[...] = jnp.cos(ang)
    s_ref[...] = jnp.sin(ang) * sgn_ref[...]


def _rope_tables(pos_b, inv, sgn):
    n = pos_b.shape[0]
    tm = min(1024, n)
    row = pl.BlockSpec((tm, LANES), lambda i: (i, 0))
    par = pl.BlockSpec((1, LANES), lambda i: (0, 0))
    return pl.pallas_call(
        _rope_table_kernel,
        grid=(n // tm,),
        in_specs=[row, par, par],
        out_specs=[row, row],
        out_shape=[jax.ShapeDtypeStruct((n, LANES), F32)] * 2,
        compiler_params=_cparams(("parallel",)),
        name="rope_tables",
    )(pos_b, inv, sgn)


def _rope_block(x, c, s, half):
    fwd = pltpu.roll(x, LANES - half, 1)
    bwd = pltpu.roll(x, half, 1)
    lane = lax.broadcasted_iota(jnp.int32, x.shape, 1)
    rot = jnp.where((lane % (2 * half)) < half, fwd, bwd)
    return x * c + rot * s


def _even_in_kernel(h_ref, g_ref, w_ref, gm_ref, qg_ref, kg_ref, c_ref, s_ref,
                    ain_ref, agate_ref, q_ref, k_ref, v_ref, bgate_ref):
    x = h_ref[...]
    ms = jnp.mean(x * x, axis=-1, keepdims=True)
    yb = (x * lax.rsqrt(ms + EPS) * g_ref[...]).astype(BF16)

    def seg(lo, hi):
        return jnp.dot(yb, w_ref[:, lo:hi], preferred_element_type=F32)

    ain_ref[...] = seg(0, W_A)
    agate_ref[...] = seg(W_A, 2 * W_A).astype(BF16)
    c = c_ref[...]
    s = s_ref[...]

    def qk_prep(z, gain_ref, out_ref):
        sq = z * z
        hi = sq.astype(BF16)
        lo = (sq - hi.astype(F32)).astype(BF16)
        gm = gm_ref[...]
        msq = jnp.dot(hi, gm, preferred_element_type=F32) + jnp.dot(lo, gm, preferred_element_type=F32)
        zn = z * lax.rsqrt(msq + EPS) * gain_ref[...]
        for j in range(W_B // LANES):
            blk = zn[:, j * LANES:(j + 1) * LANES]
            out_ref[:, j * LANES:(j + 1) * LANES] = _rope_block(blk, c, s, B_ROT // 2).astype(BF16)

    o = 2 * W_A
    qk_prep(seg(o, o + W_B), qg_ref, q_ref)
    qk_prep(seg(o + W_B, o + 2 * W_B), kg_ref, k_ref)
    v_ref[...] = seg(o + 2 * W_B, o + 3 * W_B).astype(BF16)
    bgate_ref[...] = seg(o + 3 * W_B, o + 4 * W_B).astype(BF16)


def _even_in(h, g, w_in, gm, qg, kg, c_tab, s_tab):
    n = h.shape[0]
    tm = min(ROW_TILE, n)
    row = lambda w: pl.BlockSpec((tm, w), lambda i: (i, 0))
    col = lambda w: pl.BlockSpec((w, tm), lambda i: (0, i))
    full = lambda a: pl.BlockSpec(a.shape, lambda i: (0,) * a.ndim)
    sds = jax.ShapeDtypeStruct
    return pl.pallas_call(
        _even_in_kernel,
        grid=(n // tm,),
        in_specs=[row(D_MODEL), full(g), full(w_in), full(gm), full(qg), full(kg), row(LANES), row(LANES)],
        out_specs=[row(W_A), row(W_A), row(W_B), row(W_B), row(W_B), row(W_B)],
        out_shape=[sds((n, W_A), F32), sds((n, W_A), BF16), sds((n, W_B), BF16), sds((n, W_B), BF16),
                   sds((n, W_B), BF16), sds((n, W_B), BF16)],
        compiler_params=_cparams(("parallel",)),
        name="even_in_proj",
    )(h, g, w_in, gm, qg, kg, c_tab, s_tab)


def _flash_loop(qs_ref, k_ref, v_ref, m_scr, l_scr, acc_scr, qi, tq, tk, heads, dqk, dv):
    q0 = qi * tq
    n_full = q0 // tk
    n_tot = (q0 + tq + tk - 1) // tk
    m_scr[...] = jnp.full(m_scr.shape, NEG, F32)
    l_scr[...] = jnp.zeros(l_scr.shape, F32)
    acc_scr[...] = jnp.zeros(acc_scr.shape, F32)

    def step(j, masked):
        ks = pl.multiple_of(j * tk, tk)
        hs = range(heads)
        s = [lax.dot_general(qs_ref[g], k_ref[pl.ds(ks, tk), g * dqk:(g + 1) * dqk], (((1,), (1,)), ((), ())),
                             preferred_element_type=F32) for g in hs]
        if masked:
            row = q0 + lax.broadcasted_iota(jnp.int32, s[0].shape, 0) % tq
            col = ks + lax.broadcasted_iota(jnp.int32, s[0].shape, 1)
            s = [jnp.where(col <= row, x, NEG) for x in s]
        m_prev = [m_scr[g] for g in hs]
        m_new = [jnp.maximum(mp, jnp.max(x, axis=1, keepdims=True)) for mp, x in zip(m_prev, s)]
        alpha = [jnp.exp2(mp - mn) for mp, mn in zip(m_prev, m_new)]
        p = [jnp.exp2(x - jnp.tile(mn, (1, tk // LANES))) for x, mn in zip(s, m_new)]
        for g in hs:
            l_scr[g] = alpha[g] * l_scr[g] + jnp.sum(p[g], axis=1, keepdims=True)
            m_scr[g] = m_new[g]
        pv = [jnp.dot(p[g].astype(BF16), v_ref[pl.ds(ks, tk), g * dv:(g + 1) * dv], preferred_element_type=F32)
              for g in hs]
        for g in hs:
            acc_scr[g] = alpha[g] * acc_scr[g] + pv[g]

    def full_body(j, carry):
        step(j, False)
        return carry

    def diag_body(j, carry):
        step(j, True)
        return carry

    lax.fori_loop(0, n_full, full_body, 0)
    lax.fori_loop(n_full, n_tot, diag_body, 0)


def _diff_attn_kernel(q_ref, k_ref, v_ref, bg_ref, lam_ref, sub_ref, o_ref,
                      qs_scr, m_scr, l_scr, acc_scr, *, tq, tk, heads, lam_init):
    qi = pl.program_id(2)
    for g in range(heads):
        q = q_ref[:, g * DV_B:(g + 1) * DV_B]
        lane = lax.broadcasted_iota(jnp.int32, q.shape, 1)
        zero = jnp.zeros_like(q)
        qs_scr[g, 0:tq, :] = jnp.where(lane < DK_B, q, zero)
        qs_scr[g, tq:2 * tq, :] = jnp.where(lane >= DK_B, q, zero)
    _flash_loop(qs_scr, k_ref, v_ref, m_scr, l_scr, acc_scr, qi, tq, tk, heads, DV_B, DV_B)
    lv = lam_ref[...]
    lam = (jnp.exp(jnp.sum(lv[0:1] * lv[1:2], axis=1, keepdims=True))
           - jnp.exp(jnp.sum(lv[2:3] * lv[3:4], axis=1, keepdims=True)) + lam_init)
    for g in range(heads):
        o1 = acc_scr[g, 0:tq, :] / l_scr[g, 0:tq, :]
        o2 = acc_scr[g, tq:2 * tq, :] / l_scr[g, tq:2 * tq, :]
        o = o1 - lam * o2
        ms = jnp.mean(o * o, axis=-1, keepdims=True)
        o = o * lax.rsqrt(ms + EPS) * sub_ref[...] * (1.0 - lam_init)
        gate = bg_ref[:, g * DV_B:(g + 1) * DV_B].astype(F32)
        o_ref[:, g * DV_B:(g + 1) * DV_B] = (o * _silu(gate)).astype(BF16)


def _diff_attention(q, k, v, bgate, lam_vec, subln, batch, seq, lam_init):
    n = q.shape[0]
    tq = min(DIFF_TQ, seq)
    tk = min(ATT_TK, seq)
    nq = seq // tq
    heads = DIFF_HEADS_PER_STEP
    wid = heads * DV_B
    qspec = pl.BlockSpec((tq, wid), lambda b, h, i: (b * nq + i, h))
    kvspec = pl.BlockSpec((seq, wid), lambda b, h, i: (b, h))
    full = lambda a: pl.BlockSpec(a.shape, lambda b, h, i: (0,) * a.ndim)
    return pl.pallas_call(
        functools.partial(_diff_attn_kernel, tq=tq, tk=tk, heads=heads, lam_init=lam_init),
        grid=(batch, H_B // heads, nq),
        in_specs=[qspec, kvspec, kvspec, qspec, full(lam_vec), full(subln)],
        out_specs=qspec,
        out_shape=jax.ShapeDtypeStruct((n, W_B), BF16),
        scratch_shapes=[pltpu.VMEM((heads, 2 * tq, DV_B), BF16), pltpu.VMEM((heads, 2 * tq, LANES), F32),
                        pltpu.VMEM((heads, 2 * tq, LANES), F32), pltpu.VMEM((heads, 2 * tq, DV_B), F32)],
        compiler_params=_cparams(("parallel", "parallel", "arbitrary")),
        name="diff_attention",
    )(q, k, v, bgate, lam_vec, subln)


def _mla_attn_kernel(q_ref, k_ref, v_ref, cg_ref, o_ref, qs_scr, m_scr, l_scr, acc_scr, *, tq, tk, heads):
    qi = pl.program_id(2)
    for g in range(heads):
        qs_scr[g] = q_ref[:, g * HEAD_PAD:(g + 1) * HEAD_PAD]
    _flash_loop(qs_scr, k_ref, v_ref, m_scr, l_scr, acc_scr, qi, tq, tk, heads, HEAD_PAD, DV_C)
    for g in range(heads):
        o = acc_scr[g] / l_scr[g]
        gate = cg_ref[:, g * DV_C:(g + 1) * DV_C].astype(F32)
        o_ref[:, g * DV_C:(g + 1) * DV_C] = (o * _silu(gate)).astype(BF16)


def _mla_attention(q, k, v, cgate, batch, seq):
    n = q.shape[0]
    tq = min(MLA_TQ, seq)
    tk = min(ATT_TK, seq)
    nq = seq // tq
    heads = MLA_HEADS_PER_STEP
    qspec = pl.BlockSpec((tq, heads * HEAD_PAD), lambda b, h, i: (b * nq + i, h))
    kspec = pl.BlockSpec((seq, heads * HEAD_PAD), lambda b, h, i: (b, h))
    vspec = pl.BlockSpec((seq, heads * DV_C), lambda b, h, i: (b, h))
    ospec = pl.BlockSpec((tq, heads * DV_C), lambda b, h, i: (b * nq + i, h))
    return pl.pallas_call(
        functools.partial(_mla_attn_kernel, tq=tq, tk=tk, heads=heads),
        grid=(batch, H_C // heads, nq),
        in_specs=[qspec, kspec, vspec, ospec],
        out_specs=ospec,
        out_shape=jax.ShapeDtypeStruct((n, W_C), BF16),
        scratch_shapes=[pltpu.VMEM((heads, tq, HEAD_PAD), BF16), pltpu.VMEM((heads, tq, LANES), F32),
                        pltpu.VMEM((heads, tq, LANES), F32), pltpu.VMEM((heads, tq, DV_C), F32)],
        compiler_params=_cparams(("parallel", "parallel", "arbitrary")),
        name="mla_attention",
    )(q, k, v, cgate)


def _pool_kernel(x_ref, halo_ref, gate_ref, pw_ref, ps_ref, o_ref, buf_a, buf_b, *, tm, seq):
    i = pl.program_id(0)
    t0 = (i * tm) % seq
    x = x_ref[...]
    keep = (t0 > 0).astype(F32)
    buf_a[0:POOL_HALO, :] = halo_ref[...] * keep
    buf_a[POOL_HALO:POOL_HALO + tm, :] = x
    tot = tm + POOL_HALO
    src, dst = buf_a, buf_b
    levels = {}
    start = 0
    for w in (1, 2, 4, 8):
        start += 8
        cur = src[start:tot, :] + src[start - w:tot - w, :]
        dst[start:tot, :] = cur
        levels[2 * w] = cur[POOL_HALO - start:, :]
        src, dst = dst, src
    lane = lax.broadcasted_iota(jnp.int32, (tm, W_A), 1)
    grp = lane // POOL_GC
    win = jnp.where(grp == 0, levels[2], jnp.where(grp == 1, levels[4],
                    jnp.where(grp == 2, levels[8], levels[16])))
    wlane = jnp.where(grp == 0, 2, jnp.where(grp == 1, 4, jnp.where(grp == 2, 8, 16)))
    tpos = t0 + lax.broadcasted_iota(jnp.int32, (tm, W_A), 0)
    cnt = jnp.minimum(tpos + 1, wlane).astype(F32)
    pooled = win / cnt - x
    a = jnp.dot(pooled.astype(BF16), pw_ref[...], preferred_element_type=F32) * ps_ref[...]
    o_ref[...] = (a * _silu(gate_ref[...].astype(F32))).astype(BF16)


def _pool_mixer(a_in, a_gate, pool_w_bd, pool_scale, seq):
    n = a_in.shape[0]
    tm = min(ROW_TILE, seq)
    hb = tm // POOL_HALO
    row = lambda w: pl.BlockSpec((tm, w), lambda i: (i, 0))
    halo = pl.BlockSpec((POOL_HALO, W_A), lambda i: (jnp.maximum(i * hb - 1, 0), 0))
    full = lambda a: pl.BlockSpec(a.shape, lambda i: (0,) * a.ndim)
    return pl.pallas_call(
        functools.partial(_pool_kernel, tm=tm, seq=seq),
        grid=(n // tm,),
        in_specs=[row(W_A), halo, row(W_A), full(pool_w_bd), full(pool_scale)],
        out_specs=row(W_A),
        out_shape=jax.ShapeDtypeStruct((n, W_A), BF16),
        scratch_shapes=[pltpu.VMEM((tm + POOL_HALO, W_A), F32)] * 2,
        compiler_params=_cparams(("parallel",)),
        name="pool_mixer",
    )(a_in, a_in, a_gate, pool_w_bd, pool_scale)


def _out_kernel(h_ref, ma_ref, mb_ref, wo_ref, wg_ref, p_ref, wp_ref, o_ref, *, wa):
    m = (jnp.dot(ma_ref[...], wo_ref[0:wa, :], preferred_element_type=F32)
         + jnp.dot(mb_ref[...], wo_ref[wa:, :], preferred_element_type=F32))
    h1 = h_ref[...] + m
    gate = _sigmoid(jnp.dot(h1.astype(BF16), wg_ref[...], preferred_element_type=F32))
    pp = jnp.dot(p_ref[...].astype(BF16), wp_ref[...], preferred_element_type=F32)
    o_ref[...] = h1 + gate * pp


def _out_proj(h, mix_a, mix_b, w_out, w_gate, p, w_proj):
    n = h.shape[0]
    tm = min(ROW_TILE, n)
    wa = mix_a.shape[1]
    row = lambda w: pl.BlockSpec((tm, w), lambda i: (i, 0))
    full = lambda a: pl.BlockSpec(a.shape, lambda i: (0,) * a.ndim)
    return pl.pallas_call(
        functools.partial(_out_kernel, wa=wa),
        grid=(n // tm,),
        in_specs=[row(D_MODEL), row(wa), row(mix_b.shape[1]), full(w_out), full(w_gate),
                  row(PLE_DIM), full(w_proj)],
        out_specs=row(D_MODEL),
        out_shape=jax.ShapeDtypeStruct((n, D_MODEL), F32),
        compiler_params=_cparams(("parallel",)),
        name="out_proj_ple",
    )(h, mix_a, mix_b, w_out, w_gate, p, w_proj)


O_CQ = 0
O_CKV = O_CQ + Q_LORA
O_KR = O_CKV + KV_LORA
O_CG = O_KR + LANES
O_QKV = O_CG + W_C
O_DG = O_QKV + CONV_CH
O_BA = O_DG + W_D
O_TOT = O_BA + LANES
HEAD_PAD = 2 * LANES


def _odd_in_kernel(h_ref, g_ref, w_ref, qan_ref, wuq_ref, kvan_ref, wukv_ref, qn_ref, kn_ref,
                   c_ref, s_ref, alog_ref, dtb_ref,
                   q_ref, k_ref, v_ref, cg_ref, qkv_ref, dg_ref, gb_ref):
    x = h_ref[...]
    ms = jnp.mean(x * x, axis=-1, keepdims=True)
    yb = (x * lax.rsqrt(ms + EPS) * g_ref[...]).astype(BF16)

    def seg(lo, hi):
        return jnp.dot(yb, w_ref[:, lo:hi], preferred_element_type=F32)

    cg_ref[...] = seg(O_CG, O_QKV).astype(BF16)
    qkv_ref[...] = seg(O_QKV, O_DG).astype(BF16)
    dg_ref[...] = seg(O_DG, O_BA).astype(BF16)

    ba = seg(O_BA, O_TOT)
    beta = _sigmoid(ba)
    g = -jnp.exp(alog_ref[...]) * jax.nn.softplus(ba + dtb_ref[...])
    lane = lax.broadcasted_iota(jnp.int32, ba.shape, 1)
    gb_ref[...] = jnp.where(lane < H_D, beta, g)

    c = c_ref[...]
    s = s_ref[...]
    qn = qn_ref[...]
    kn = kn_ref[...]
    scale = DQK_C ** -0.5 * LOG2E

    cq = seg(O_CQ, O_CKV)
    cqn = cq * lax.rsqrt(jnp.mean(cq * cq, axis=-1, keepdims=True) + EPS) * qan_ref[...]
    qu = jnp.dot(cqn.astype(BF16), wuq_ref[...], preferred_element_type=F32)
    for hh in range(H_C):
        nope = qu[:, hh * D_NOPE:(hh + 1) * D_NOPE]
        rope = qu[:, H_C * D_NOPE + hh * LANES:H_C * D_NOPE + (hh + 1) * LANES]
        ss = jnp.sum(nope * nope, axis=-1, keepdims=True) + jnp.sum(rope * rope, axis=-1, keepdims=True)
        r = lax.rsqrt(ss * (1.0 / DQK_C) + EPS) * scale
        q_ref[:, hh * HEAD_PAD:hh * HEAD_PAD + LANES] = (nope * r * qn[:, 0:LANES]).astype(BF16)
        q_ref[:, hh * HEAD_PAD + LANES:(hh + 1) * HEAD_PAD] = _rope_block(
            rope * r * qn[:, LANES:2 * LANES], c, s, D_ROPE // 2).astype(BF16)

    ckv = seg(O_CKV, O_KR)
    ckvn = ckv * lax.rsqrt(jnp.mean(ckv * ckv, axis=-1, keepdims=True) + EPS) * kvan_ref[...]
    kvu = jnp.dot(ckvn.astype(BF16), wukv_ref[...], preferred_element_type=F32)
    kr = seg(O_KR, O_CG)
    kr_ss = jnp.sum(kr * kr, axis=-1, keepdims=True)
    v_ref[...] = kvu[:, H_C * D_NOPE:].astype(BF16)
    for hh in range(H_C):
        nope = kvu[:, hh * D_NOPE:(hh + 1) * D_NOPE]
        ss = jnp.sum(nope * nope, axis=-1, keepdims=True) + kr_ss
        r = lax.rsqrt(ss * (1.0 / DQK_C) + EPS)
        k_ref[:, hh * HEAD_PAD:hh * HEAD_PAD + LANES] = (nope * r * kn[:, 0:LANES]).astype(BF16)
        k_ref[:, hh * HEAD_PAD + LANES:(hh + 1) * HEAD_PAD] = _rope_block(
            kr * r * kn[:, LANES:2 * LANES], c, s, D_ROPE // 2).astype(BF16)


def _odd_in(h, g, w_in, qan, wuq, kvan, wukv, qn, kn, c_tab, s_tab, alog, dtb):
    n = h.shape[0]
    tm = min(ROW_TILE, n)
    row = lambda w: pl.BlockSpec((tm, w), lambda i: (i, 0))
    col = lambda w: pl.BlockSpec((w, tm), lambda i: (0, i))
    full = lambda a: pl.BlockSpec(a.shape, lambda i: (0,) * a.ndim)
    sds = jax.ShapeDtypeStruct
    return pl.pallas_call(
        _odd_in_kernel,
        grid=(n // tm,),
        in_specs=[row(D_MODEL), full(g), full(w_in), full(qan), full(wuq), full(kvan), full(wukv),
                  full(qn), full(kn), row(LANES), row(LANES), full(alog), full(dtb)],
        out_specs=[row(H_C * HEAD_PAD), row(H_C * HEAD_PAD), row(W_C), row(W_C), row(CONV_CH), row(W_D),
                   row(LANES)],
        out_shape=[sds((n, H_C * HEAD_PAD), BF16), sds((n, H_C * HEAD_PAD), BF16), sds((n, W_C), BF16),
                   sds((n, W_C), BF16), sds((n, CONV_CH), BF16), sds((n, W_D), BF16), sds((n, LANES), F32)],
        compiler_params=_cparams(("parallel",)),
        name="odd_in_proj",
    )(h, g, w_in, qan, wuq, kvan, wukv, qn, kn, c_tab, s_tab, alog, dtb)


def _gdn_prep_kernel(x_ref, halo_ref, cw_ref, gbc_ref, lhs1_ref, lhs2_ref, u_ref, gl_ref,
                     buf, ybuf, *, tm, seq):
    i = pl.program_id(0)
    t0 = (i * tm) % seq
    keep = (t0 > 0).astype(F32)
    buf[0:CONV_HALO, :] = halo_ref[...].astype(F32) * keep
    buf[CONV_HALO:CONV_HALO + tm, :] = x_ref[...].astype(F32)
    cw = cw_ref[...]
    y = None
    for j in range(CONV_K):
        off = CONV_HALO - (CONV_K - 1) + j
        term = buf[off:off + tm, :] * cw[j:j + 1, :]
        y = term if y is None else y + term
    ybuf[...] = _silu(y)

    ri = lax.broadcasted_iota(jnp.int32, (tm, tm), 0)
    ci = lax.broadcasted_iota(jnp.int32, (tm, tm), 1)
    low = jnp.where(((ri // CHUNK) == (ci // CHUNK)) & (ri >= ci), 1.0, 0.0).astype(F32)
    i2 = lax.broadcasted_iota(jnp.int32, (LANES, LANES), 0)
    j2 = lax.broadcasted_iota(jnp.int32, (LANES, LANES), 1)
    eye_l = jnp.where(i2 == j2, 1.0, 0.0).astype(BF16)
    gbc = gbc_ref[...]
    gcs = _split3_dot(low, gbc)
    gcs_t = _split3_dot_nt(eye_l, gcs)

    pdim = H_D * CHUNK
    r4 = lax.broadcasted_iota(jnp.int32, (pdim, pdim), 0)
    c4 = lax.broadcasted_iota(jnp.int32, (pdim, pdim), 1)
    same = (r4 // CHUNK) == (c4 // CHUNK)
    incl = same & (r4 >= c4)
    strict = same & (r4 > c4)
    eye = jnp.where(r4 == c4, 1.0, 0.0).astype(F32)
    zblk = jnp.zeros((CHUNK, DK_D), F32)

    def block_diag(blocks):
        return jnp.concatenate(
            [jnp.concatenate([blocks[h] if j == h else zblk for j in range(H_D)], axis=1) for h in range(H_D)],
            axis=0)

    chunks = list(range(tm // CHUNK))
    heads = list(range(H_D))

    def load_qkv(c):
        r0 = c * CHUNK
        qs, ks, vs = [], [], []
        for hh in heads:
            l0 = hh * DK_D
            qh = ybuf[r0:r0 + CHUNK, l0:l0 + DK_D]
            kh = ybuf[r0:r0 + CHUNK, H_D * DK_D + l0:H_D * DK_D + l0 + DK_D]
            vs.append(ybuf[r0:r0 + CHUNK, 2 * H_D * DK_D + l0:2 * H_D * DK_D + l0 + DV_D])
            qs.append(qh * lax.rsqrt(jnp.sum(qh * qh, axis=-1, keepdims=True) + EPS) * (DK_D ** -0.5))
            ks.append(kh * lax.rsqrt(jnp.sum(kh * kh, axis=-1, keepdims=True) + EPS))
        return qs, ks, vs

    def decay_terms(c):
        r0 = c * CHUNK
        beta = jnp.concatenate([gbc[r0:r0 + CHUNK, hh:hh + 1] for hh in heads], axis=0)
        gcol = jnp.concatenate([gcs[r0:r0 + CHUNK, H_D + hh:H_D + hh + 1] for hh in heads], axis=0)
        grow = jnp.concatenate([gcs_t[H_D + hh:H_D + hh + 1, r0:r0 + CHUNK] for hh in heads], axis=1)
        glasts = [gcs[r0 + CHUNK - 1:r0 + CHUNK, H_D + hh:H_D + hh + 1] for hh in heads]
        glast = jnp.concatenate([jnp.broadcast_to(gl, (CHUNK, 1)) for gl in glasts], axis=0)
        gamma = jnp.exp(jnp.where(incl, gcol - grow, -jnp.inf))
        return beta, gcol, glasts, glast, gamma

    qkv = [load_qkv(c) for c in chunks]
    q_st = [jnp.concatenate(x[0], axis=0) for x in qkv]
    k_st = [jnp.concatenate(x[1], axis=0) for x in qkv]
    v_st = [jnp.concatenate(x[2], axis=0) for x in qkv]
    k_bd = [block_diag(x[1]).astype(BF16) for x in qkv]
    q_bd = [block_diag(x[0]).astype(BF16) for x in qkv]
    qkkk = [_bdot_nt(jnp.concatenate([qb, kb], axis=0), kb) for qb, kb in zip(q_bd, k_bd)]
    dec = [decay_terms(c) for c in chunks]
    a = [jnp.where(strict, x[pdim:, :] * d[4] * d[0], 0.0) for x, d in zip(qkkk, dec)]
    t = [eye - x for x in a]
    pw = a
    for _ in range(5):
        pw = [_bdot(x, x) for x in pw]
        t = [x + _bdot(x, y) for x, y in zip(t, pw)]
    egc = [jnp.exp(d[1]) for d in dec]
    uw = [_bdot(tt, jnp.concatenate([v * d[0], k * (d[0] * e)], axis=1))
          for tt, v, k, d, e in zip(t, v_st, k_st, dec, egc)]
    kdt = [_bdot_nt(eye_l, k * jnp.exp(d[3] - d[1])).astype(BF16) for k, d in zip(k_st, dec)]
    for c in chunks:
        qkg = (qkkk[c][0:pdim, :] * dec[c][4]).astype(BF16)
        q_dec = (q_st[c] * egc[c]).astype(BF16)
        for hh in heads:
            idx = c * H_D + hh
            hrows = slice(hh * CHUNK, (hh + 1) * CHUNK)
            lhs1_ref[idx, 0:CHUNK, :] = uw[c][hrows, DV_D:].astype(BF16)
            lhs1_ref[idx, CHUNK:2 * CHUNK, :] = q_dec[hrows, :]
            u_ref[idx] = uw[c][hrows, 0:DV_D].astype(BF16)
            gl_ref[c, hh:hh + 1, :] = jnp.broadcast_to(jnp.exp(dec[c][2][hh]), (1, LANES))
        for pp in range(H_D // 2):
            idx2 = c * (H_D // 2) + pp
            lhs2_ref[idx2, 0:LANES, :] = qkg[pp * LANES:(pp + 1) * LANES, pp * LANES:(pp + 1) * LANES]
            lhs2_ref[idx2, LANES:2 * LANES, :] = kdt[c][:, pp * LANES:(pp + 1) * LANES]


def _gdn_prep(qkv, conv_w, gb, seq):
    n = qkv.shape[0]
    tm = min(GDN_PREP_TILE, seq)
    nch = tm // CHUNK
    hb = tm // CONV_HALO
    row = lambda w: pl.BlockSpec((tm, w), lambda i: (i, 0))
    halo = pl.BlockSpec((CONV_HALO, CONV_CH), lambda i: (jnp.maximum(i * hb - 1, 0), 0))
    full = lambda a: pl.BlockSpec(a.shape, lambda i: (0,) * a.ndim)
    ch3 = lambda m, r, w: pl.BlockSpec((nch * m, r, w), lambda i: (i, 0, 0))
    sds = jax.ShapeDtypeStruct
    nc = n // CHUNK
    return pl.pallas_call(
        functools.partial(_gdn_prep_kernel, tm=tm, seq=seq),
        grid=(n // tm,),
        in_specs=[row(CONV_CH), halo, full(conv_w), row(LANES)],
        out_specs=[ch3(H_D, 2 * CHUNK, DK_D), ch3(H_D // 2, 2 * LANES, LANES), ch3(H_D, CHUNK, DV_D),
                   pl.BlockSpec((nch, H_D, LANES), lambda i: (i, 0, 0))],
        out_shape=[sds((nc * H_D, 2 * CHUNK, DK_D), BF16), sds((nc * H_D // 2, 2 * LANES, LANES), BF16),
                   sds((nc * H_D, CHUNK, DV_D), BF16), sds((nc, H_D, LANES), F32)],
        scratch_shapes=[pltpu.VMEM((tm + CONV_HALO, CONV_CH), F32), pltpu.VMEM((tm, CONV_CH), F32)],
        compiler_params=_cparams(("parallel",)),
        name="gdn_chunk_prep",
    )(qkv, qkv, conv_w, gb)


def _gdn_scan_kernel(lhs1_ref, lhs2_ref, u_ref, gl_ref, dg_ref, on_ref, o_ref, s_scr, *, nb, tb):
    @pl.when(pl.program_id(0) == 0)
    def _():
        s_scr[...] = jnp.zeros(s_scr.shape, F32)

    on = on_ref[...]
    zb = jnp.zeros((2 * CHUNK, DK_D), BF16)
    left = lax.broadcasted_iota(jnp.int32, (DK_D, LANES), 1) < CHUNK

    def finish(o, b, rows, hh):
        ms = jnp.mean(o * o, axis=-1, keepdims=True)
        o = o * lax.rsqrt(ms + EPS) * on
        gate = dg_ref[b, rows, hh * DV_D:(hh + 1) * DV_D].astype(F32)
        o_ref[b, rows, hh * DV_D:(hh + 1) * DV_D] = (o * _silu(gate)).astype(BF16)

    def chunk_body(c, carry):
        rows = pl.ds(pl.multiple_of(c * CHUNK, CHUNK), CHUNK)
        chains = [(b, 2 * pp) for b in range(nb) for pp in range(H_D // 2)]
        st = [(s_scr[b * H_D + h0], s_scr[b * H_D + h0 + 1]) for b, h0 in chains]
        r1 = [jnp.dot(jnp.concatenate([jnp.concatenate([lhs1_ref[b, c * H_D + h0], zb], axis=1),
                                       jnp.concatenate([zb, lhs1_ref[b, c * H_D + h0 + 1]], axis=1)], axis=0),
                      jnp.concatenate([s0, s1], axis=0).astype(BF16), preferred_element_type=F32)
              for (b, h0), (s0, s1) in zip(chains, st)]
        v2 = [jnp.concatenate([u_ref[b, c * H_D + h0].astype(F32) - x[0:CHUNK, :],
                               u_ref[b, c * H_D + h0 + 1].astype(F32) - x[2 * CHUNK:3 * CHUNK, :]],
                              axis=0).astype(BF16) for (b, h0), x in zip(chains, r1)]
        r2 = []
        for (b, h0), v in zip(chains, v2):
            blk = lhs2_ref[b, c * (H_D // 2) + h0 // 2]
            kd = blk[LANES:2 * LANES, :]
            zk = jnp.zeros_like(kd)
            l2 = jnp.concatenate([blk[0:LANES, :], jnp.where(left, kd, zk), jnp.where(left, zk, kd)], axis=0)
            r2.append(jnp.dot(l2, v, preferred_element_type=F32))
        for (b, h0), (s0, s1), x, y in zip(chains, st, r1, r2):
            s_scr[b * H_D + h0] = s0 * gl_ref[b, c, h0:h0 + 1, :] + y[LANES:LANES + DK_D, :]
            s_scr[b * H_D + h0 + 1] = s1 * gl_ref[b, c, h0 + 1:h0 + 2, :] + y[LANES + DK_D:LANES + 2 * DK_D, :]
            finish(x[CHUNK:2 * CHUNK, :] + y[0:CHUNK, :], b, rows, h0)
            finish(x[3 * CHUNK:4 * CHUNK, :] + y[CHUNK:2 * CHUNK, :], b, rows, h0 + 1)
        return carry

    lax.fori_loop(0, tb // CHUNK, chunk_body, 0)


def _gdn_scan(lhs1, lhs2, u, gl, d_gate, o_norm, batch, seq):
    tb = min(GDN_SCAN_TILE, seq)
    nch = tb // CHUNK
    ncb = seq // CHUNK
    lhs1 = lhs1.reshape(batch, ncb * H_D, 2 * CHUNK, DK_D)
    lhs2 = lhs2.reshape(batch, ncb * H_D // 2, 2 * LANES, LANES)
    u = u.reshape(batch, ncb * H_D, CHUNK, DV_D)
    gl = gl.reshape(batch, ncb, H_D, LANES)
    d_gate = d_gate.reshape(batch, seq, W_D)
    ch4 = lambda r, w: pl.BlockSpec((batch, nch * H_D, r, w), lambda i: (0, i, 0, 0))
    tok = pl.BlockSpec((batch, tb, W_D), lambda i: (0, i, 0))
    out = pl.pallas_call(
        functools.partial(_gdn_scan_kernel, nb=batch, tb=tb),
        grid=(seq // tb,),
        in_specs=[ch4(2 * CHUNK, DK_D),
                  pl.BlockSpec((batch, nch * H_D // 2, 2 * LANES, LANES), lambda i: (0, i, 0, 0)),
                  ch4(CHUNK, DV_D),
                  pl.BlockSpec((batch, nch, H_D, LANES), lambda i: (0, i, 0, 0)), tok,
                  pl.BlockSpec(o_norm.shape, lambda i: (0, 0))],
        out_specs=tok,
        out_shape=jax.ShapeDtypeStruct((batch, seq, W_D), BF16),
        scratch_shapes=[pltpu.VMEM((batch * H_D, DK_D, DV_D), F32)],
        compiler_params=_cparams(("arbitrary",)),
        name="gdn_state_scan",
    )(lhs1, lhs2, u, gl, d_gate, o_norm)
    return out.reshape(batch * seq, W_D)


def _rope_patterns(rot_dim, theta, period):
    half = rot_dim // 2
    inv = jnp.power(jnp.float32(theta), -jnp.arange(half, dtype=F32) * (2.0 / rot_dim))
    lane = np.arange(LANES)
    in_rot = (lane % period) < rot_dim
    idx = jnp.asarray(lane % half)
    inv_l = jnp.where(jnp.asarray(in_rot), inv[idx], 0.0).astype(F32)[None, :]
    sgn = np.where(in_rot, np.where((lane % period) < half, -1.0, 1.0), 0.0).astype(np.float32)[None, :]
    return inv_l, jnp.asarray(sgn)


def _pad_cols(w, width):
    return jnp.pad(w, ((0, 0), (0, width - w.shape[1])))


def _odd_w_in_layout(w):
    cq, ckv, kr, cg, qkv, db, da, dg = jnp.split(
        w, np.cumsum((Q_LORA, KV_LORA, D_ROPE, W_C, CONV_CH, H_D, H_D, W_D))[:-1].tolist(), axis=1)
    return jnp.concatenate([cq, ckv, _pad_cols(kr, LANES), cg, qkv, dg,
                            _pad_cols(jnp.concatenate([db, da], axis=1), LANES)], axis=1)


def _wuq_layout(w):
    w = w.reshape(Q_LORA, H_C, DQK_C)
    nope = w[:, :, :D_NOPE].reshape(Q_LORA, H_C * D_NOPE)
    rope = jnp.pad(w[:, :, D_NOPE:], ((0, 0), (0, 0), (0, LANES - D_ROPE))).reshape(Q_LORA, H_C * LANES)
    return jnp.concatenate([nope, rope], axis=1)


def _wukv_layout(w):
    w = w.reshape(KV_LORA, H_C, D_NOPE + DV_C)
    return jnp.concatenate([w[:, :, :D_NOPE].reshape(KV_LORA, H_C * D_NOPE),
                            w[:, :, D_NOPE:].reshape(KV_LORA, H_C * DV_C)], axis=1)


def _head_gain_layout(g):
    return jnp.pad(g, (0, HEAD_PAD - DQK_C))[None, :].astype(F32)


def kernel(x, p, positions, norm_g, ple_w_gate, ple_w_proj, ev_w_in, ev_pool_w, ev_pool_scale, ev_q_norm, ev_k_norm, ev_lambda, ev_subln, ev_w_out, od_w_in, od_q_a_norm, od_w_uq, od_kv_a_norm, od_w_ukv, od_q_norm, od_k_norm, od_conv_w, od_a_log, od_dt_bias, od_o_norm, od_w_out):
    batch, seq, _ = x.shape
    depth = p.shape[0]
    n = batch * seq
    h = x.reshape(n, D_MODEL)
    pos_b = jnp.broadcast_to(positions.astype(F32).reshape(n, 1), (n, LANES))

    inv_e, sgn_e = _rope_patterns(B_ROT, ROPE_THETA, DK_B)
    ce, se = _rope_tables(pos_b, inv_e, sgn_e)
    inv_o, sgn_o = _rope_patterns(D_ROPE, MLA_THETA, LANES)
    co, so = _rope_tables(pos_b, inv_o, sgn_o)

    gidx = np.arange(W_B) // DK_B
    gmean = jnp.asarray((gidx[:, None] == gidx[None, :]).astype(np.float32) / DK_B).astype(BF16)

    for i in range(depth):
        j = i // 2
        g = norm_g[i][None, :]
        if i % 2 == 0:
            w_in = ev_w_in[j].astype(BF16)
            qg = (jnp.tile(ev_q_norm[j], 2 * H_B) * (DK_B ** -0.5 * LOG2E))[None, :]
            kg = jnp.tile(ev_k_norm[j], 2 * H_B)[None, :]
            a_in, a_gate, q, k, vt, b_gate = _even_in(h, g, w_in, gmean, qg, kg, ce, se)
            lam_init = 0.8 - 0.6 * math.exp(-0.3 * i)
            ob = _diff_attention(q, k, vt, b_gate, ev_lambda[j], ev_subln[j][None, :], batch, seq, lam_init)
            pw = jax.scipy.linalg.block_diag(*[ev_pool_w[j][gi] for gi in range(len(POOL_WINDOWS))]).astype(BF16)
            oa = _pool_mixer(a_in, a_gate, pw, ev_pool_scale[j][None, :], seq)
            w_out = ev_w_out[j].astype(BF16)
        else:
            w_in = _odd_w_in_layout(od_w_in[j]).astype(BF16)
            alog = jnp.zeros((LANES,), F32).at[H_D:2 * H_D].set(od_a_log[j])[None, :]
            dtb = jnp.zeros((LANES,), F32).at[H_D:2 * H_D].set(od_dt_bias[j])[None, :]
            q, k, vt, c_gate, qkv, d_gate, gb = _odd_in(
                h, g, w_in, od_q_a_norm[j][None, :], _wuq_layout(od_w_uq[j]).astype(BF16),
                od_kv_a_norm[j][None, :], _wukv_layout(od_w_ukv[j]).astype(BF16),
                _head_gain_layout(od_q_norm[j]), _head_gain_layout(od_k_norm[j]), co, so, alog, dtb)
            oa = _mla_attention(q, k, vt, c_gate, batch, seq)
            lhs1, lhs2, u, gl = _gdn_prep(qkv, od_conv_w[j], gb, seq)
            ob = _gdn_scan(lhs1, lhs2, u, gl, d_gate, od_o_norm[j][None, :], batch, seq)
            w_out = od_w_out[j].astype(BF16)
        h = _out_proj(h, oa, ob, w_out, ple_w_gate[i].astype(BF16), p[i].reshape(n, PLE_DIM),
                      ple_w_proj[i].astype(BF16))
    return h.reshape(batch, seq, D_MODEL)
```

```python
import functools
import math

import numpy as np
import jax
import jax.numpy as jnp
from jax import lax
from jax.experimental import pallas as pl
from jax.experimental.pallas import tpu as pltpu

F32 = jnp.float32
BF16 = jnp.bfloat16

D_MODEL = 1024
PLE_DIM = 256
EPS = 1e-6
W_A = 256
POOL_WINDOWS = (2, 4, 8, 16)
POOL_GC = 64
W_B = 768
DV_B = 128
H_B = 6
DK_B = 64
B_ROT = 16
ROPE_THETA = 500000.0
W_C = 512
DV_C = 128
H_C = 4
D_NOPE = 128
D_ROPE = 64
DQK_C = 192
Q_LORA = 256
KV_LORA = 128
MLA_THETA = 10000.0
W_D = 512
DK_D = 128
DV_D = 128
H_D = 4
CONV_K = 4
CHUNK = 64
CONV_CH = H_D * (2 * DK_D + DV_D)

LANES = 128
VMEM_LIMIT = 48 * 1024 * 1024
NEG = -1e30
LOG2E = math.log2(math.e)

ROW_TILE = 512
ATT_TK = 512
POOL_HALO = 32
CONV_HALO = 16
GDN_PREP_TILE = 256
GDN_SCAN_TILE = 256
DIFF_HEADS_PER_STEP = 3
MLA_HEADS_PER_STEP = 2
DIFF_TQ = 512
MLA_TQ = 1024


def _cparams(sem):
    return pltpu.CompilerParams(dimension_semantics=sem, vmem_limit_bytes=VMEM_LIMIT)


def _silu(x):
    return x * (1.0 / (1.0 + jnp.exp(-x)))


def _sigmoid(x):
    return 1.0 / (1.0 + jnp.exp(-x))


def _bdot(a, b):
    return jnp.dot(a.astype(BF16), b.astype(BF16), preferred_element_type=F32)


def _bdot_nt(a, b):
    return lax.dot_general(a.astype(BF16), b.astype(BF16), (((1,), (1,)), ((), ())),
                           preferred_element_type=F32)


def _split3_dot(a_exact, b):
    b0 = b.astype(BF16)
    r1 = b - b0.astype(F32)
    b1 = r1.astype(BF16)
    b2 = (r1 - b1.astype(F32)).astype(BF16)
    a = a_exact.astype(BF16)
    return (jnp.dot(a, b0, preferred_element_type=F32) + jnp.dot(a, b1, preferred_element_type=F32)
            + jnp.dot(a, b2, preferred_element_type=F32))


def _split3_dot_nt(a_exact, b):
    b0 = b.astype(BF16)
    r1 = b - b0.astype(F32)
    b1 = r1.astype(BF16)
    b2 = (r1 - b1.astype(F32)).astype(BF16)
    return _bdot_nt(a_exact, b0) + _bdot_nt(a_exact, b1) + _bdot_nt(a_exact, b2)


def _rope_table_kernel(pos_ref, inv_ref, sgn_ref, c_ref, s_ref):
    ang = pos_ref[...] * inv_ref[...]
    c_ref[...] = jnp.cos(ang)
    s_ref[...] = jnp.sin(ang) * sgn_ref[...]


def _rope_tables(pos_b, inv, sgn):
    n = pos_b.shape[0]
    tm = min(1024, n)
    row = pl.BlockSpec((tm, LANES), lambda i: (i, 0))
    par = pl.BlockSpec((1, LANES), lambda i: (0, 0))
    return pl.pallas_call(
        _rope_table_kernel,
        grid=(n // tm,),
        in_specs=[row, par, par],
        out_specs=[row, row],
        out_shape=[jax.ShapeDtypeStruct((n, LANES), F32)] * 2,
        compiler_params=_cparams(("parallel",)),
        name="rope_tables",
    )(pos_b, inv, sgn)


def _rope_block(x, c, s, half):
    fwd = pltpu.roll(x, LANES - half, 1)
    bwd = pltpu.roll(x, half, 1)
    lane = lax.broadcasted_iota(jnp.int32, x.shape, 1)
    rot = jnp.where((lane % (2 * half)) < half, fwd, bwd)
    return x * c + rot * s


def _even_in_kernel(h_ref, g_ref, w_ref, qg_ref, kg_ref, c_ref, s_ref,
                    ain_ref, agate_ref, q_ref, k_ref, v_ref, bgate_ref):
    x = h_ref[...]
    ms = jnp.mean(x * x, axis=-1, keepdims=True)
    yb = (x * lax.rsqrt(ms + EPS) * g_ref[...]).astype(BF16)

    c_all = c_ref[...]
    s_all = s_ref[...]
    o = 2 * W_A
    bounds = [(0, W_A), (W_A, 2 * W_A), (o, o + W_B), (o + W_B, o + 2 * W_B), (o + 2 * W_B, o + 3 * W_B),
              (o + 3 * W_B, o + 4 * W_B)]

    def project(rs):
        return [jnp.dot(yb[rs, :], w_ref[:, lo:hi], preferred_element_type=F32) for lo, hi in bounds]

    def qk_prep(rs, z, gain_ref, out_ref, c, s):
        gain = gain_ref[...]
        first = lax.broadcasted_iota(jnp.int32, (z.shape[0], LANES), 1) < DK_B
        for j in range(W_B // LANES):
            blk = z[:, j * LANES:(j + 1) * LANES]
            sq = blk * blk
            ss0 = jnp.sum(jnp.where(first, sq, 0.0), axis=-1, keepdims=True)
            ss1 = jnp.sum(jnp.where(first, 0.0, sq), axis=-1, keepdims=True)
            r = jnp.where(first, lax.rsqrt(ss0 * (1.0 / DK_B) + EPS), lax.rsqrt(ss1 * (1.0 / DK_B) + EPS))
            zn = blk * r * gain[:, j * LANES:(j + 1) * LANES]
            out_ref[rs, j * LANES:(j + 1) * LANES] = _rope_block(zn, c, s, B_ROT // 2).astype(BF16)

    def epilogue(rs, zs):
        z_ain, z_agate, z_q, z_k, z_v, z_bg = zs
        ain_ref[rs, :] = z_ain
        agate_ref[rs, :] = z_agate.astype(BF16)
        v_ref[rs, :] = z_v.astype(BF16)
        bgate_ref[rs, :] = z_bg.astype(BF16)
        qk_prep(rs, z_q, qg_ref, q_ref, c_all[rs, :], s_all[rs, :])
        qk_prep(rs, z_k, kg_ref, k_ref, c_all[rs, :], s_all[rs, :])

    half = x.shape[0] // 2
    halves = [slice(0, half), slice(half, 2 * half)]
    zs = [project(rs) for rs in halves]
    for rs, z in zip(halves, zs):
        epilogue(rs, z)


def _even_in(h, g, w_in, qg, kg, c_tab, s_tab):
    n = h.shape[0]
    tm = min(ROW_TILE, n)
    row = lambda w: pl.BlockSpec((tm, w), lambda i: (i, 0))
    col = lambda w: pl.BlockSpec((w, tm), lambda i: (0, i))
    full = lambda a: pl.BlockSpec(a.shape, lambda i: (0,) * a.ndim)
    sds = jax.ShapeDtypeStruct
    return pl.pallas_call(
        _even_in_kernel,
        grid=(n // tm,),
        in_specs=[row(D_MODEL), full(g), full(w_in), full(qg), full(kg), row(LANES), row(LANES)],
        out_specs=[row(W_A), row(W_A), row(W_B), row(W_B), row(W_B), row(W_B)],
        out_shape=[sds((n, W_A), F32), sds((n, W_A), BF16), sds((n, W_B), BF16), sds((n, W_B), BF16),
                   sds((n, W_B), BF16), sds((n, W_B), BF16)],
        compiler_params=_cparams(("parallel",)),
        name="even_in_proj",
    )(h, g, w_in, qg, kg, c_tab, s_tab)


def _flash_loop(qs_ref, k_ref, v_ref, m_scr, l_scr, acc_scr, qi, tq, tk, heads, dqk, dv):
    q0 = qi * tq
    n_full = q0 // tk
    n_tot = (q0 + tq + tk - 1) // tk
    m_scr[...] = jnp.full(m_scr.shape, NEG, F32)
    l_scr[...] = jnp.zeros(l_scr.shape, F32)
    acc_scr[...] = jnp.zeros(acc_scr.shape, F32)

    def step(j, masked):
        ks = pl.multiple_of(j * tk, tk)
        hs = range(heads)
        s = [lax.dot_general(qs_ref[g], k_ref[pl.ds(ks, tk), g * dqk:(g + 1) * dqk], (((1,), (1,)), ((), ())),
                             preferred_element_type=F32) for g in hs]
        if masked:
            row = q0 + lax.broadcasted_iota(jnp.int32, s[0].shape, 0) % tq
            col = ks + lax.broadcasted_iota(jnp.int32, s[0].shape, 1)
            s = [jnp.where(col <= row, x, NEG) for x in s]
        m_prev = [m_scr[g] for g in hs]
        m_new = [jnp.maximum(mp, jnp.max(x, axis=1, keepdims=True)) for mp, x in zip(m_prev, s)]
        alpha = [jnp.exp2(mp - mn) for mp, mn in zip(m_prev, m_new)]
        p = [jnp.exp2(x - jnp.tile(mn, (1, tk // LANES))) for x, mn in zip(s, m_new)]
        ones = jnp.ones((tk, LANES), BF16)
        pv = [jnp.dot(p[g].astype(BF16),
                      jnp.concatenate([v_ref[pl.ds(ks, tk), g * dv:(g + 1) * dv], ones], axis=1),
                      preferred_element_type=F32) for g in hs]
        for g in hs:
            m_scr[g] = m_new[g]
            l_scr[g] = alpha[g] * l_scr[g] + pv[g][:, dv:]
            acc_scr[g] = alpha[g] * acc_scr[g] + pv[g][:, 0:dv]

    def full_body(j, carry):
        step(j, False)
        return carry

    def diag_body(j, carry):
        step(j, True)
        return carry

    lax.fori_loop(0, n_full, full_body, 0)
    lax.fori_loop(n_full, n_tot, diag_body, 0)


def _diff_attn_kernel(q_ref, k_ref, v_ref, bg_ref, lam_ref, sub_ref, o_ref,
                      qs_scr, m_scr, l_scr, acc_scr, *, tq, tk, heads, lam_init):
    qi = pl.program_id(2)
    for g in range(heads):
        q = q_ref[:, g * DV_B:(g + 1) * DV_B]
        lane = lax.broadcasted_iota(jnp.int32, q.shape, 1)
        zero = jnp.zeros_like(q)
        qs_scr[g, 0:tq, :] = jnp.where(lane < DK_B, q, zero)
        qs_scr[g, tq:2 * tq, :] = jnp.where(lane >= DK_B, q, zero)
    _flash_loop(qs_scr, k_ref, v_ref, m_scr, l_scr, acc_scr, qi, tq, tk, heads, DV_B, DV_B)
    lv = lam_ref[...]
    lam = (jnp.exp(jnp.sum(lv[0:1] * lv[1:2], axis=1, keepdims=True))
           - jnp.exp(jnp.sum(lv[2:3] * lv[3:4], axis=1, keepdims=True)) + lam_init)
    for g in range(heads):
        o1 = acc_scr[g, 0:tq, :] / l_scr[g, 0:tq, :]
        o2 = acc_scr[g, tq:2 * tq, :] / l_scr[g, tq:2 * tq, :]
        o = o1 - lam * o2
        ms = jnp.mean(o * o, axis=-1, keepdims=True)
        o = o * lax.rsqrt(ms + EPS) * sub_ref[...] * (1.0 - lam_init)
        gate = bg_ref[:, g * DV_B:(g + 1) * DV_B].astype(F32)
        o_ref[:, g * DV_B:(g + 1) * DV_B] = (o * _silu(gate)).astype(BF16)


def _diff_attention(q, k, v, bgate, lam_vec, subln, batch, seq, lam_init):
    n = q.shape[0]
    tq = min(DIFF_TQ, seq)
    tk = min(ATT_TK, seq)
    nq = seq // tq
    heads = DIFF_HEADS_PER_STEP
    wid = heads * DV_B
    qspec = pl.BlockSpec((tq, wid), lambda b, h, i: (b * nq + i, h))
    kvspec = pl.BlockSpec((seq, wid), lambda b, h, i: (b, h))
    full = lambda a: pl.BlockSpec(a.shape, lambda b, h, i: (0,) * a.ndim)
    return pl.pallas_call(
        functools.partial(_diff_attn_kernel, tq=tq, tk=tk, heads=heads, lam_init=lam_init),
        grid=(batch, H_B // heads, nq),
        in_specs=[qspec, kvspec, kvspec, qspec, full(lam_vec), full(subln)],
        out_specs=qspec,
        out_shape=jax.ShapeDtypeStruct((n, W_B), BF16),
        scratch_shapes=[pltpu.VMEM((heads, 2 * tq, DV_B), BF16), pltpu.VMEM((heads, 2 * tq, LANES), F32),
                        pltpu.VMEM((heads, 2 * tq, LANES), F32), pltpu.VMEM((heads, 2 * tq, DV_B), F32)],
        compiler_params=_cparams(("parallel", "parallel", "arbitrary")),
        name="diff_attention",
    )(q, k, v, bgate, lam_vec, subln)


def _mla_attn_kernel(q_ref, k_ref, v_ref, cg_ref, o_ref, qs_scr, m_scr, l_scr, acc_scr, *, tq, tk, heads):
    qi = pl.program_id(2)
    for g in range(heads):
        qs_scr[g] = q_ref[:, g * HEAD_PAD:(g + 1) * HEAD_PAD]
    _flash_loop(qs_scr, k_ref, v_ref, m_scr, l_scr, acc_scr, qi, tq, tk, heads, HEAD_PAD, DV_C)
    for g in range(heads):
        o = acc_scr[g] / l_scr[g]
        gate = cg_ref[:, g * DV_C:(g + 1) * DV_C].astype(F32)
        o_ref[:, g * DV_C:(g + 1) * DV_C] = (o * _silu(gate)).astype(BF16)


def _mla_attention(q, k, v, cgate, batch, seq):
    n = q.shape[0]
    tq = min(MLA_TQ, seq)
    tk = min(ATT_TK, seq)
    nq = seq // tq
    heads = MLA_HEADS_PER_STEP
    qspec = pl.BlockSpec((tq, heads * HEAD_PAD), lambda b, h, i: (b * nq + i, h))
    kspec = pl.BlockSpec((seq, heads * HEAD_PAD), lambda b, h, i: (b, h))
    vspec = pl.BlockSpec((seq, heads * DV_C), lambda b, h, i: (b, h))
    ospec = pl.BlockSpec((tq, heads * DV_C), lambda b, h, i: (b * nq + i, h))
    return pl.pallas_call(
        functools.partial(_mla_attn_kernel, tq=tq, tk=tk, heads=heads),
        grid=(batch, H_C // heads, nq),
        in_specs=[qspec, kspec, vspec, ospec],
        out_specs=ospec,
        out_shape=jax.ShapeDtypeStruct((n, W_C), BF16),
        scratch_shapes=[pltpu.VMEM((heads, tq, HEAD_PAD), BF16), pltpu.VMEM((heads, tq, LANES), F32),
                        pltpu.VMEM((heads, tq, LANES), F32), pltpu.VMEM((heads, tq, DV_C), F32)],
        compiler_params=_cparams(("parallel", "parallel", "arbitrary")),
        name="mla_attention",
    )(q, k, v, cgate)


def _pool_kernel(x_ref, halo_ref, gate_ref, pw_ref, ps_ref, o_ref, buf_a, buf_b, *, tm, seq):
    i = pl.program_id(0)
    t0 = (i * tm) % seq
    x = x_ref[...]
    keep = (t0 > 0).astype(F32)
    buf_a[0:POOL_HALO, :] = halo_ref[...] * keep
    buf_a[POOL_HALO:POOL_HALO + tm, :] = x
    tot = tm + POOL_HALO
    src, dst = buf_a, buf_b
    levels = {}
    start = 0
    for w in (1, 2, 4, 8):
        start += 8
        cur = src[start:tot, :] + src[start - w:tot - w, :]
        dst[start:tot, :] = cur
        levels[2 * w] = cur[POOL_HALO - start:, :]
        src, dst = dst, src
    lane = lax.broadcasted_iota(jnp.int32, (tm, W_A), 1)
    grp = lane // POOL_GC
    win = jnp.where(grp == 0, levels[2], jnp.where(grp == 1, levels[4],
                    jnp.where(grp == 2, levels[8], levels[16])))
    wlane = jnp.where(grp == 0, 2, jnp.where(grp == 1, 4, jnp.where(grp == 2, 8, 16)))
    tpos = t0 + lax.broadcasted_iota(jnp.int32, (tm, W_A), 0)
    cnt = jnp.minimum(tpos + 1, wlane).astype(F32)
    pooled = win / cnt - x
    a = jnp.dot(pooled.astype(BF16), pw_ref[...], preferred_element_type=F32) * ps_ref[...]
    o_ref[...] = (a * _silu(gate_ref[...].astype(F32))).astype(BF16)


def _pool_mixer(a_in, a_gate, pool_w_bd, pool_scale, seq):
    n = a_in.shape[0]
    tm = min(ROW_TILE, seq)
    hb = tm // POOL_HALO
    row = lambda w: pl.BlockSpec((tm, w), lambda i: (i, 0))
    halo = pl.BlockSpec((POOL_HALO, W_A), lambda i: (jnp.maximum(i * hb - 1, 0), 0))
    full = lambda a: pl.BlockSpec(a.shape, lambda i: (0,) * a.ndim)
    return pl.pallas_call(
        functools.partial(_pool_kernel, tm=tm, seq=seq),
        grid=(n // tm,),
        in_specs=[row(W_A), halo, row(W_A), full(pool_w_bd), full(pool_scale)],
        out_specs=row(W_A),
        out_shape=jax.ShapeDtypeStruct((n, W_A), BF16),
        scratch_shapes=[pltpu.VMEM((tm + POOL_HALO, W_A), F32)] * 2,
        compiler_params=_cparams(("parallel",)),
        name="pool_mixer",
    )(a_in, a_in, a_gate, pool_w_bd, pool_scale)


def _out_kernel(h_ref, ma_ref, mb_ref, wo_ref, wg_ref, p_ref, wp_ref, o_ref, *, wa):
    m = (jnp.dot(ma_ref[...], wo_ref[0:wa, :], preferred_element_type=F32)
         + jnp.dot(mb_ref[...], wo_ref[wa:, :], preferred_element_type=F32))
    h1 = h_ref[...] + m
    gate = _sigmoid(jnp.dot(h1.astype(BF16), wg_ref[...], preferred_element_type=F32))
    pp = jnp.dot(p_ref[...].astype(BF16), wp_ref[...], preferred_element_type=F32)
    o_ref[...] = h1 + gate * pp


def _out_proj(h, mix_a, mix_b, w_out, w_gate, p, w_proj):
    n = h.shape[0]
    tm = min(ROW_TILE, n)
    wa = mix_a.shape[1]
    row = lambda w: pl.BlockSpec((tm, w), lambda i: (i, 0))
    full = lambda a: pl.BlockSpec(a.shape, lambda i: (0,) * a.ndim)
    return pl.pallas_call(
        functools.partial(_out_kernel, wa=wa),
        grid=(n // tm,),
        in_specs=[row(D_MODEL), row(wa), row(mix_b.shape[1]), full(w_out), full(w_gate),
                  row(PLE_DIM), full(w_proj)],
        out_specs=row(D_MODEL),
        out_shape=jax.ShapeDtypeStruct((n, D_MODEL), F32),
        compiler_params=_cparams(("parallel",)),
        name="out_proj_ple",
    )(h, mix_a, mix_b, w_out, w_gate, p, w_proj)


O_CQ = 0
O_CKV = O_CQ + Q_LORA
O_KR = O_CKV + KV_LORA
O_CG = O_KR + LANES
O_QKV = O_CG + W_C
O_DG = O_QKV + CONV_CH
O_BA = O_DG + W_D
O_TOT = O_BA + LANES
HEAD_PAD = 2 * LANES


def _odd_in_kernel(h_ref, g_ref, w_ref, qan_ref, wuq_ref, kvan_ref, wukv_ref, qn_ref, kn_ref,
                   c_ref, s_ref, alog_ref, dtb_ref,
                   q_ref, k_ref, v_ref, cg_ref, qkv_ref, dg_ref, gb_ref):
    x = h_ref[...]
    ms = jnp.mean(x * x, axis=-1, keepdims=True)
    yb = (x * lax.rsqrt(ms + EPS) * g_ref[...]).astype(BF16)
    c_all = c_ref[...]
    s_all = s_ref[...]
    qn = qn_ref[...]
    kn = kn_ref[...]
    scale = DQK_C ** -0.5 * LOG2E

    def project(rs):
        def seg(lo, hi):
            return jnp.dot(yb[rs, :], w_ref[:, lo:hi], preferred_element_type=F32)

        cq = seg(O_CQ, O_CKV)
        ckv = seg(O_CKV, O_KR)
        kr = seg(O_KR, O_CG)
        z_cg = seg(O_CG, O_QKV)
        z_qkv = seg(O_QKV, O_DG)
        z_dg = seg(O_DG, O_BA)
        ba = seg(O_BA, O_TOT)
        cqn = cq * lax.rsqrt(jnp.mean(cq * cq, axis=-1, keepdims=True) + EPS) * qan_ref[...]
        ckvn = ckv * lax.rsqrt(jnp.mean(ckv * ckv, axis=-1, keepdims=True) + EPS) * kvan_ref[...]
        qu = jnp.dot(cqn.astype(BF16), wuq_ref[...], preferred_element_type=F32)
        kvu = jnp.dot(ckvn.astype(BF16), wukv_ref[...], preferred_element_type=F32)
        return kr, z_cg, z_qkv, z_dg, ba, qu, kvu

    def epilogue(rs, zs):
        kr, z_cg, z_qkv, z_dg, ba, qu, kvu = zs
        c = c_all[rs, :]
        s = s_all[rs, :]
        cg_ref[rs, :] = z_cg.astype(BF16)
        qkv_ref[rs, :] = z_qkv.astype(BF16)
        dg_ref[rs, :] = z_dg.astype(BF16)
        v_ref[rs, :] = kvu[:, H_C * D_NOPE:].astype(BF16)
        beta = _sigmoid(ba)
        g = -jnp.exp(alog_ref[...]) * jax.nn.softplus(ba + dtb_ref[...])
        lane = lax.broadcasted_iota(jnp.int32, ba.shape, 1)
        gb_ref[rs, :] = jnp.where(lane < H_D, beta, g)
        for hh in range(H_C):
            nope = qu[:, hh * D_NOPE:(hh + 1) * D_NOPE]
            rope = qu[:, H_C * D_NOPE + hh * LANES:H_C * D_NOPE + (hh + 1) * LANES]
            ss = jnp.sum(nope * nope, axis=-1, keepdims=True) + jnp.sum(rope * rope, axis=-1, keepdims=True)
            r = lax.rsqrt(ss * (1.0 / DQK_C) + EPS) * scale
            q_ref[rs, hh * HEAD_PAD:hh * HEAD_PAD + LANES] = (nope * r * qn[:, 0:LANES]).astype(BF16)
            q_ref[rs, hh * HEAD_PAD + LANES:(hh + 1) * HEAD_PAD] = _rope_block(
                rope * r * qn[:, LANES:2 * LANES], c, s, D_ROPE // 2).astype(BF16)
        kr_ss = jnp.sum(kr * kr, axis=-1, keepdims=True)
        kr_rot = _rope_block(kr * kn[:, LANES:2 * LANES], c, s, D_ROPE // 2)
        for hh in range(H_C):
            nope = kvu[:, hh * D_NOPE:(hh + 1) * D_NOPE]
            ss = jnp.sum(nope * nope, axis=-1, keepdims=True) + kr_ss
            r = lax.rsqrt(ss * (1.0 / DQK_C) + EPS)
            k_ref[rs, hh * HEAD_PAD:hh * HEAD_PAD + LANES] = (nope * r * kn[:, 0:LANES]).astype(BF16)
            k_ref[rs, hh * HEAD_PAD + LANES:(hh + 1) * HEAD_PAD] = (kr_rot * r).astype(BF16)

    half = x.shape[0] // 2
    halves = [slice(0, half), slice(half, 2 * half)]
    zs = [project(rs) for rs in halves]
    for rs, z in zip(halves, zs):
        epilogue(rs, z)


def _odd_in(h, g, w_in, qan, wuq, kvan, wukv, qn, kn, c_tab, s_tab, alog, dtb):
    n = h.shape[0]
    tm = min(ROW_TILE, n)
    row = lambda w: pl.BlockSpec((tm, w), lambda i: (i, 0))
    col = lambda w: pl.BlockSpec((w, tm), lambda i: (0, i))
    full = lambda a: pl.BlockSpec(a.shape, lambda i: (0,) * a.ndim)
    sds = jax.ShapeDtypeStruct
    return pl.pallas_call(
        _odd_in_kernel,
        grid=(n // tm,),
        in_specs=[row(D_MODEL), full(g), full(w_in), full(qan), full(wuq), full(kvan), full(wukv),
                  full(qn), full(kn), row(LANES), row(LANES), full(alog), full(dtb)],
        out_specs=[row(H_C * HEAD_PAD), row(H_C * HEAD_PAD), row(W_C), row(W_C), row(CONV_CH), row(W_D),
                   row(LANES)],
        out_shape=[sds((n, H_C * HEAD_PAD), BF16), sds((n, H_C * HEAD_PAD), BF16), sds((n, W_C), BF16),
                   sds((n, W_C), BF16), sds((n, CONV_CH), BF16), sds((n, W_D), BF16), sds((n, LANES), F32)],
        compiler_params=_cparams(("parallel",)),
        name="odd_in_proj",
    )(h, g, w_in, qan, wuq, kvan, wukv, qn, kn, c_tab, s_tab, alog, dtb)


def _gdn_prep_kernel(x_ref, halo_ref, cw_ref, gbc_ref, lhs1_ref, lhs2_ref, u_ref, gl_ref,
                     buf, ybuf, *, tm, seq):
    i = pl.program_id(0)
    t0 = (i * tm) % seq
    keep = (t0 > 0).astype(F32)
    buf[0:CONV_HALO, :] = halo_ref[...].astype(F32) * keep
    buf[CONV_HALO:CONV_HALO + tm, :] = x_ref[...].astype(F32)
    cw = cw_ref[...]
    y = None
    for j in range(CONV_K):
        off = CONV_HALO - (CONV_K - 1) + j
        term = buf[off:off + tm, :] * cw[j:j + 1, :]
        y = term if y is None else y + term
    ybuf[...] = _silu(y)

    ri = lax.broadcasted_iota(jnp.int32, (tm, tm), 0)
    ci = lax.broadcasted_iota(jnp.int32, (tm, tm), 1)
    low = jnp.where(((ri // CHUNK) == (ci // CHUNK)) & (ri >= ci), 1.0, 0.0).astype(F32)
    i2 = lax.broadcasted_iota(jnp.int32, (LANES, LANES), 0)
    j2 = lax.broadcasted_iota(jnp.int32, (LANES, LANES), 1)
    eye_l = jnp.where(i2 == j2, 1.0, 0.0).astype(BF16)
    gbc = gbc_ref[...]
    gcs = _split3_dot(low, gbc)
    gcs_t = _split3_dot_nt(eye_l, gcs)

    pdim = H_D * CHUNK
    r4 = lax.broadcasted_iota(jnp.int32, (pdim, pdim), 0)
    c4 = lax.broadcasted_iota(jnp.int32, (pdim, pdim), 1)
    same = (r4 // CHUNK) == (c4 // CHUNK)
    incl = same & (r4 >= c4)
    strict = same & (r4 > c4)
    eye = jnp.where(r4 == c4, 1.0, 0.0).astype(F32)
    zblk = jnp.zeros((CHUNK, DK_D), F32)

    def block_diag(blocks):
        return jnp.concatenate(
            [jnp.concatenate([blocks[h] if j == h else zblk for j in range(H_D)], axis=1) for h in range(H_D)],
            axis=0)

    chunks = list(range(tm // CHUNK))
    heads = list(range(H_D))

    def load_qkv(c):
        r0 = c * CHUNK
        qs, ks, vs = [], [], []
        for hh in heads:
            l0 = hh * DK_D
            qh = ybuf[r0:r0 + CHUNK, l0:l0 + DK_D]
            kh = ybuf[r0:r0 + CHUNK, H_D * DK_D + l0:H_D * DK_D + l0 + DK_D]
            vs.append(ybuf[r0:r0 + CHUNK, 2 * H_D * DK_D + l0:2 * H_D * DK_D + l0 + DV_D])
            qs.append(qh * lax.rsqrt(jnp.sum(qh * qh, axis=-1, keepdims=True) + EPS) * (DK_D ** -0.5))
            ks.append(kh * lax.rsqrt(jnp.sum(kh * kh, axis=-1, keepdims=True) + EPS))
        return qs, ks, vs

    def decay_terms(c):
        r0 = c * CHUNK
        beta = jnp.concatenate([gbc[r0:r0 + CHUNK, hh:hh + 1] for hh in heads], axis=0)
        gcol = jnp.concatenate([gcs[r0:r0 + CHUNK, H_D + hh:H_D + hh + 1] for hh in heads], axis=0)
        grow = jnp.concatenate([gcs_t[H_D + hh:H_D + hh + 1, r0:r0 + CHUNK] for hh in heads], axis=1)
        glasts = [gcs[r0 + CHUNK - 1:r0 + CHUNK, H_D + hh:H_D + hh + 1] for hh in heads]
        glast = jnp.concatenate([jnp.broadcast_to(gl, (CHUNK, 1)) for gl in glasts], axis=0)
        gamma = jnp.exp(jnp.where(incl, gcol - grow, -jnp.inf))
        return beta, gcol, glasts, glast, gamma

    qkv = [load_qkv(c) for c in chunks]
    q_st = [jnp.concatenate(x[0], axis=0) for x in qkv]
    k_st = [jnp.concatenate(x[1], axis=0) for x in qkv]
    v_st = [jnp.concatenate(x[2], axis=0) for x in qkv]
    k_bd = [block_diag(x[1]).astype(BF16) for x in qkv]
    q_bd = [block_diag(x[0]).astype(BF16) for x in qkv]
    qkkk = [_bdot_nt(jnp.concatenate([qb, kb], axis=0), kb) for qb, kb in zip(q_bd, k_bd)]
    dec = [decay_terms(c) for c in chunks]
    a = [jnp.where(strict, x[pdim:, :] * d[4] * d[0], 0.0) for x, d in zip(qkkk, dec)]
    t = [eye - x for x in a]
    pw = a
    for _ in range(5):
        pw = [_bdot(x, x) for x in pw]
        t = [x + _bdot(x, y) for x, y in zip(t, pw)]
    egc = [jnp.exp(d[1]) for d in dec]
    uw = [_bdot(tt, jnp.concatenate([v * d[0], k * (d[0] * e)], axis=1))
          for tt, v, k, d, e in zip(t, v_st, k_st, dec, egc)]
    kdt = [_bdot_nt(eye_l, k * jnp.exp(d[3] - d[1])).astype(BF16) for k, d in zip(k_st, dec)]
    for c in chunks:
        qkg = (qkkk[c][0:pdim, :] * dec[c][4]).astype(BF16)
        q_dec = (q_st[c] * egc[c]).astype(BF16)
        for hh in heads:
            idx = c * H_D + hh
            hrows = slice(hh * CHUNK, (hh + 1) * CHUNK)
            lhs1_ref[idx, 0:CHUNK, :] = uw[c][hrows, DV_D:].astype(BF16)
            lhs1_ref[idx, CHUNK:2 * CHUNK, :] = q_dec[hrows, :]
            u_ref[idx] = uw[c][hrows, 0:DV_D].astype(BF16)
            gl_ref[c, hh:hh + 1, :] = jnp.broadcast_to(jnp.exp(dec[c][2][hh]), (1, LANES))
        for pp in range(H_D // 2):
            idx2 = c * (H_D // 2) + pp
            lhs2_ref[idx2, 0:LANES, :] = qkg[pp * LANES:(pp + 1) * LANES, pp * LANES:(pp + 1) * LANES]
            lhs2_ref[idx2, LANES:2 * LANES, :] = kdt[c][:, pp * LANES:(pp + 1) * LANES]


def _gdn_prep(qkv, conv_w, gb, seq):
    n = qkv.shape[0]
    tm = min(GDN_PREP_TILE, seq)
    nch = tm // CHUNK
    hb = tm // CONV_HALO
    row = lambda w: pl.BlockSpec((tm, w), lambda i: (i, 0))
    halo = pl.BlockSpec((CONV_HALO, CONV_CH), lambda i: (jnp.maximum(i * hb - 1, 0), 0))
    full = lambda a: pl.BlockSpec(a.shape, lambda i: (0,) * a.ndim)
    ch3 = lambda m, r, w: pl.BlockSpec((nch * m, r, w), lambda i: (i, 0, 0))
    sds = jax.ShapeDtypeStruct
    nc = n // CHUNK
    return pl.pallas_call(
        functools.partial(_gdn_prep_kernel, tm=tm, seq=seq),
        grid=(n // tm,),
        in_specs=[row(CONV_CH), halo, full(conv_w), row(LANES)],
        out_specs=[ch3(H_D, 2 * CHUNK, DK_D), ch3(H_D // 2, 2 * LANES, LANES), ch3(H_D, CHUNK, DV_D),
                   pl.BlockSpec((nch, H_D, LANES), lambda i: (i, 0, 0))],
        out_shape=[sds((nc * H_D, 2 * CHUNK, DK_D), BF16), sds((nc * H_D // 2, 2 * LANES, LANES), BF16),
                   sds((nc * H_D, CHUNK, DV_D), BF16), sds((nc, H_D, LANES), F32)],
        scratch_shapes=[pltpu.VMEM((tm + CONV_HALO, CONV_CH), F32), pltpu.VMEM((tm, CONV_CH), F32)],
        compiler_params=_cparams(("parallel",)),
        name="gdn_chunk_prep",
    )(qkv, qkv, conv_w, gb)


def _gdn_scan_kernel(lhs1_ref, lhs2_ref, u_ref, gl_ref, dg_ref, on_ref, o_ref, s_scr, *, nb, tb):
    @pl.when(pl.program_id(0) == 0)
    def _():
        s_scr[...] = jnp.zeros(s_scr.shape, F32)

    on = on_ref[...]
    zb = jnp.zeros((2 * CHUNK, DK_D), BF16)
    left = lax.broadcasted_iota(jnp.int32, (DK_D, LANES), 1) < CHUNK

    def finish(o, b, rows, hh):
        ms = jnp.mean(o * o, axis=-1, keepdims=True)
        o = o * lax.rsqrt(ms + EPS) * on
        gate = dg_ref[b, rows, hh * DV_D:(hh + 1) * DV_D].astype(F32)
        o_ref[b, rows, hh * DV_D:(hh + 1) * DV_D] = (o * _silu(gate)).astype(BF16)

    def chunk_body(c, carry):
        rows = pl.ds(pl.multiple_of(c * CHUNK, CHUNK), CHUNK)
        chains = [(b, 2 * pp) for b in range(nb) for pp in range(H_D // 2)]
        st = [(s_scr[b * H_D + h0], s_scr[b * H_D + h0 + 1]) for b, h0 in chains]
        r1 = [jnp.dot(jnp.concatenate([jnp.concatenate([lhs1_ref[b, c * H_D + h0], zb], axis=1),
                                       jnp.concatenate([zb, lhs1_ref[b, c * H_D + h0 + 1]], axis=1)], axis=0),
                      jnp.concatenate([s0, s1], axis=0).astype(BF16), preferred_element_type=F32)
              for (b, h0), (s0, s1) in zip(chains, st)]
        v2 = [jnp.concatenate([u_ref[b, c * H_D + h0].astype(F32) - x[0:CHUNK, :],
                               u_ref[b, c * H_D + h0 + 1].astype(F32) - x[2 * CHUNK:3 * CHUNK, :]],
                              axis=0).astype(BF16) for (b, h0), x in zip(chains, r1)]
        r2 = []
        for (b, h0), v in zip(chains, v2):
            blk = lhs2_ref[b, c * (H_D // 2) + h0 // 2]
            kd = blk[LANES:2 * LANES, :]
            zk = jnp.zeros_like(kd)
            l2 = jnp.concatenate([blk[0:LANES, :], jnp.where(left, kd, zk), jnp.where(left, zk, kd)], axis=0)
            r2.append(jnp.dot(l2, v, preferred_element_type=F32))
        for (b, h0), (s0, s1), x, y in zip(chains, st, r1, r2):
            s_scr[b * H_D + h0] = s0 * gl_ref[b, c, h0:h0 + 1, :] + y[LANES:LANES + DK_D, :]
            s_scr[b * H_D + h0 + 1] = s1 * gl_ref[b, c, h0 + 1:h0 + 2, :] + y[LANES + DK_D:LANES + 2 * DK_D, :]
            finish(x[CHUNK:2 * CHUNK, :] + y[0:CHUNK, :], b, rows, h0)
            finish(x[3 * CHUNK:4 * CHUNK, :] + y[CHUNK:2 * CHUNK, :], b, rows, h0 + 1)
        return carry

    lax.fori_loop(0, tb // CHUNK, chunk_body, 0)


def _gdn_scan(lhs1, lhs2, u, gl, d_gate, o_norm, batch, seq):
    tb = min(GDN_SCAN_TILE, seq)
    nch = tb // CHUNK
    ncb = seq // CHUNK
    lhs1 = lhs1.reshape(batch, ncb * H_D, 2 * CHUNK, DK_D)
    lhs2 = lhs2.reshape(batch, ncb * H_D // 2, 2 * LANES, LANES)
    u = u.reshape(batch, ncb * H_D, CHUNK, DV_D)
    gl = gl.reshape(batch, ncb, H_D, LANES)
    d_gate = d_gate.reshape(batch, seq, W_D)
    ch4 = lambda r, w: pl.BlockSpec((batch, nch * H_D, r, w), lambda i: (0, i, 0, 0))
    tok = pl.BlockSpec((batch, tb, W_D), lambda i: (0, i, 0))
    out = pl.pallas_call(
        functools.partial(_gdn_scan_kernel, nb=batch, tb=tb),
        grid=(seq // tb,),
        in_specs=[ch4(2 * CHUNK, DK_D),
                  pl.BlockSpec((batch, nch * H_D // 2, 2 * LANES, LANES), lambda i: (0, i, 0, 0)),
                  ch4(CHUNK, DV_D),
                  pl.BlockSpec((batch, nch, H_D, LANES), lambda i: (0, i, 0, 0)), tok,
                  pl.BlockSpec(o_norm.shape, lambda i: (0, 0))],
        out_specs=tok,
        out_shape=jax.ShapeDtypeStruct((batch, seq, W_D), BF16),
        scratch_shapes=[pltpu.VMEM((batch * H_D, DK_D, DV_D), F32)],
        compiler_params=_cparams(("arbitrary",)),
        name="gdn_state_scan",
    )(lhs1, lhs2, u, gl, d_gate, o_norm)
    return out.reshape(batch * seq, W_D)


def _rope_patterns(rot_dim, theta, period):
    half = rot_dim // 2
    inv = jnp.power(jnp.float32(theta), -jnp.arange(half, dtype=F32) * (2.0 / rot_dim))
    lane = np.arange(LANES)
    in_rot = (lane % period) < rot_dim
    idx = jnp.asarray(lane % half)
    inv_l = jnp.where(jnp.asarray(in_rot), inv[idx], 0.0).astype(F32)[None, :]
    sgn = np.where(in_rot, np.where((lane % period) < half, -1.0, 1.0), 0.0).astype(np.float32)[None, :]
    return inv_l, jnp.asarray(sgn)


def _pad_cols(w, width):
    return jnp.pad(w, ((0, 0), (0, width - w.shape[1])))


def _odd_w_in_layout(w):
    cq, ckv, kr, cg, qkv, db, da, dg = jnp.split(
        w, np.cumsum((Q_LORA, KV_LORA, D_ROPE, W_C, CONV_CH, H_D, H_D, W_D))[:-1].tolist(), axis=1)
    return jnp.concatenate([cq, ckv, _pad_cols(kr, LANES), cg, qkv, dg,
                            _pad_cols(jnp.concatenate([db, da], axis=1), LANES)], axis=1)


def _wuq_layout(w):
    w = w.reshape(Q_LORA, H_C, DQK_C)
    nope = w[:, :, :D_NOPE].reshape(Q_LORA, H_C * D_NOPE)
    rope = jnp.pad(w[:, :, D_NOPE:], ((0, 0), (0, 0), (0, LANES - D_ROPE))).reshape(Q_LORA, H_C * LANES)
    return jnp.concatenate([nope, rope], axis=1)


def _wukv_layout(w):
    w = w.reshape(KV_LORA, H_C, D_NOPE + DV_C)
    return jnp.concatenate([w[:, :, :D_NOPE].reshape(KV_LORA, H_C * D_NOPE),
                            w[:, :, D_NOPE:].reshape(KV_LORA, H_C * DV_C)], axis=1)


def _head_gain_layout(g):
    return jnp.pad(g, (0, HEAD_PAD - DQK_C))[None, :].astype(F32)


def kernel(x, p, positions, norm_g, ple_w_gate, ple_w_proj, ev_w_in, ev_pool_w, ev_pool_scale, ev_q_norm, ev_k_norm, ev_lambda, ev_subln, ev_w_out, od_w_in, od_q_a_norm, od_w_uq, od_kv_a_norm, od_w_ukv, od_q_norm, od_k_norm, od_conv_w, od_a_log, od_dt_bias, od_o_norm, od_w_out):
    batch, seq, _ = x.shape
    depth = p.shape[0]
    n = batch * seq
    h = x.reshape(n, D_MODEL)
    pos_b = jnp.broadcast_to(positions.astype(F32).reshape(n, 1), (n, LANES))

    inv_e, sgn_e = _rope_patterns(B_ROT, ROPE_THETA, DK_B)
    ce, se = _rope_tables(pos_b, inv_e, sgn_e)
    inv_o, sgn_o = _rope_patterns(D_ROPE, MLA_THETA, LANES)
    co, so = _rope_tables(pos_b, inv_o, sgn_o)


    for i in range(depth):
        j = i // 2
        g = norm_g[i][None, :]
        if i % 2 == 0:
            w_in = ev_w_in[j].astype(BF16)
            qg = (jnp.tile(ev_q_norm[j], 2 * H_B) * (DK_B ** -0.5 * LOG2E))[None, :]
            kg = jnp.tile(ev_k_norm[j], 2 * H_B)[None, :]
            a_in, a_gate, q, k, vt, b_gate = _even_in(h, g, w_in, qg, kg, ce, se)
            lam_init = 0.8 - 0.6 * math.exp(-0.3 * i)
            ob = _diff_attention(q, k, vt, b_gate, ev_lambda[j], ev_subln[j][None, :], batch, seq, lam_init)
            pw = jax.scipy.linalg.block_diag(*[ev_pool_w[j][gi] for gi in range(len(POOL_WINDOWS))]).astype(BF16)
            oa = _pool_mixer(a_in, a_gate, pw, ev_pool_scale[j][None, :], seq)
            w_out = ev_w_out[j].astype(BF16)
        else:
            w_in = _odd_w_in_layout(od_w_in[j]).astype(BF16)
            alog = jnp.zeros((LANES,), F32).at[H_D:2 * H_D].set(od_a_log[j])[None, :]
            dtb = jnp.zeros((LANES,), F32).at[H_D:2 * H_D].set(od_dt_bias[j])[None, :]
            q, k, vt, c_gate, qkv, d_gate, gb = _odd_in(
                h, g, w_in, od_q_a_norm[j][None, :], _wuq_layout(od_w_uq[j]).astype(BF16),
                od_kv_a_norm[j][None, :], _wukv_layout(od_w_ukv[j]).astype(BF16),
                _head_gain_layout(od_q_norm[j]), _head_gain_layout(od_k_norm[j]), co, so, alog, dtb)
            oa = _mla_attention(q, k, vt, c_gate, batch, seq)
            lhs1, lhs2, u, gl = _gdn_prep(qkv, od_conv_w[j], gb, seq)
            ob = _gdn_scan(lhs1, lhs2, u, gl, d_gate, od_o_norm[j][None, :], batch, seq)
            w_out = od_w_out[j].astype(BF16)
        h = _out_proj(h, oa, ob, w_out, ple_w_gate[i].astype(BF16), p[i].reshape(n, PLE_DIM),
                      ple_w_proj[i].astype(BF16))
    return h.reshape(batch, seq, D_MODEL)
```

```python
import functools
import math

import numpy as np
import jax
import jax.numpy as jnp
from jax import lax
from jax.experimental import pallas as pl
from jax.experimental.pallas import tpu as pltpu

F32 = jnp.float32
BF16 = jnp.bfloat16

D_MODEL = 1024
PLE_DIM = 256
EPS = 1e-6
W_A = 256
POOL_WINDOWS = (2, 4, 8, 16)
POOL_GC = 64
W_B = 768
DV_B = 128
H_B = 6
DK_B = 64
B_ROT = 16
ROPE_THETA = 500000.0
W_C = 512
DV_C = 128
H_C = 4
D_NOPE = 128
D_ROPE = 64
DQK_C = 192
Q_LORA = 256
KV_LORA = 128
MLA_THETA = 10000.0
W_D = 512
DK_D = 128
DV_D = 128
H_D = 4
CONV_K = 4
CHUNK = 64
CONV_CH = H_D * (2 * DK_D + DV_D)

LANES = 128
VMEM_LIMIT = 48 * 1024 * 1024
NEG = -1e30
LOG2E = math.log2(math.e)

ROW_TILE = 512
ATT_TK = 1024
POOL_HALO = 32
CONV_HALO = 16
GDN_PREP_TILE = 256
GDN_SCAN_TILE = 256
DIFF_HEADS_PER_STEP = 2
MLA_HEADS_PER_STEP = 2
DIFF_TQ = 512
MLA_TQ = 1024


def _cparams(sem):
    return pltpu.CompilerParams(dimension_semantics=sem, vmem_limit_bytes=VMEM_LIMIT)


def _silu(x):
    return x * (1.0 / (1.0 + jnp.exp(-x)))


def _sigmoid(x):
    return 1.0 / (1.0 + jnp.exp(-x))


def _bdot(a, b):
    return jnp.dot(a.astype(BF16), b.astype(BF16), preferred_element_type=F32)


def _bdot_nt(a, b):
    return lax.dot_general(a.astype(BF16), b.astype(BF16), (((1,), (1,)), ((), ())),
                           preferred_element_type=F32)


def _split3_dot(a_exact, b):
    b0 = b.astype(BF16)
    r1 = b - b0.astype(F32)
    b1 = r1.astype(BF16)
    b2 = (r1 - b1.astype(F32)).astype(BF16)
    a = a_exact.astype(BF16)
    return (jnp.dot(a, b0, preferred_element_type=F32) + jnp.dot(a, b1, preferred_element_type=F32)
            + jnp.dot(a, b2, preferred_element_type=F32))


def _split3_dot_nt(a_exact, b):
    b0 = b.astype(BF16)
    r1 = b - b0.astype(F32)
    b1 = r1.astype(BF16)
    b2 = (r1 - b1.astype(F32)).astype(BF16)
    return _bdot_nt(a_exact, b0) + _bdot_nt(a_exact, b1) + _bdot_nt(a_exact, b2)


def _rope_table_kernel(pos_ref, inv_ref, sgn_ref, c_ref, s_ref):
    ang = pos_ref[...] * inv_ref[...]
    c_ref[...] = jnp.cos(ang)
    s_ref[...] = jnp.sin(ang) * sgn_ref[...]


def _rope_tables(pos_b, inv, sgn):
    n = pos_b.shape[0]
    tm = min(1024, n)
    row = pl.BlockSpec((tm, LANES), lambda i: (i, 0))
    par = pl.BlockSpec((1, LANES), lambda i: (0, 0))
    return pl.pallas_call(
        _rope_table_kernel,
        grid=(n // tm,),
        in_specs=[row, par, par],
        out_specs=[row, row],
        out_shape=[jax.ShapeDtypeStruct((n, LANES), F32)] * 2,
        compiler_params=_cparams(("parallel",)),
        name="rope_tables",
    )(pos_b, inv, sgn)


def _rope_block(x, c, s, half):
    fwd = pltpu.roll(x, LANES - half, 1)
    bwd = pltpu.roll(x, half, 1)
    lane = lax.broadcasted_iota(jnp.int32, x.shape, 1)
    rot = jnp.where((lane % (2 * half)) < half, fwd, bwd)
    return x * c + rot * s


def _even_in_kernel(h_ref, g_ref, w_ref, qg_ref, kg_ref, c_ref, s_ref,
                    ain_ref, agate_ref, q_ref, k_ref, v_ref, bgate_ref):
    x = h_ref[...]
    ms = jnp.mean(x * x, axis=-1, keepdims=True)
    yb = (x * lax.rsqrt(ms + EPS) * g_ref[...]).astype(BF16)

    c_all = c_ref[...]
    s_all = s_ref[...]
    o = 2 * W_A
    bounds = [(0, W_A), (W_A, 2 * W_A), (o, o + W_B), (o + W_B, o + 2 * W_B), (o + 2 * W_B, o + 3 * W_B),
              (o + 3 * W_B, o + 4 * W_B)]

    def project(rs):
        return [jnp.dot(yb[rs, :], w_ref[:, lo:hi], preferred_element_type=F32) for lo, hi in bounds]

    def qk_prep(rs, z, gain_ref, out_ref, c, s):
        gain = gain_ref[...]
        first = lax.broadcasted_iota(jnp.int32, (z.shape[0], LANES), 1) < DK_B
        for j in range(W_B // LANES):
            blk = z[:, j * LANES:(j + 1) * LANES]
            sq = blk * blk
            ss0 = jnp.sum(jnp.where(first, sq, 0.0), axis=-1, keepdims=True)
            ss1 = jnp.sum(jnp.where(first, 0.0, sq), axis=-1, keepdims=True)
            r = jnp.where(first, lax.rsqrt(ss0 * (1.0 / DK_B) + EPS), lax.rsqrt(ss1 * (1.0 / DK_B) + EPS))
            zn = blk * r * gain[:, j * LANES:(j + 1) * LANES]
            out_ref[rs, j * LANES:(j + 1) * LANES] = _rope_block(zn, c, s, B_ROT // 2).astype(BF16)

    def epilogue(rs, zs):
        z_ain, z_agate, z_q, z_k, z_v, z_bg = zs
        ain_ref[rs, :] = z_ain
        agate_ref[rs, :] = z_agate.astype(BF16)
        v_ref[rs, :] = z_v.astype(BF16)
        bgate_ref[rs, :] = z_bg.astype(BF16)
        qk_prep(rs, z_q, qg_ref, q_ref, c_all[rs, :], s_all[rs, :])
        qk_prep(rs, z_k, kg_ref, k_ref, c_all[rs, :], s_all[rs, :])

    half = x.shape[0] // 2
    halves = [slice(0, half), slice(half, 2 * half)]
    zs = [project(rs) for rs in halves]
    for rs, z in zip(halves, zs):
        epilogue(rs, z)


def _even_in(h, g, w_in, qg, kg, c_tab, s_tab):
    n = h.shape[0]
    tm = min(ROW_TILE, n)
    row = lambda w: pl.BlockSpec((tm, w), lambda i: (i, 0))
    col = lambda w: pl.BlockSpec((w, tm), lambda i: (0, i))
    full = lambda a: pl.BlockSpec(a.shape, lambda i: (0,) * a.ndim)
    sds = jax.ShapeDtypeStruct
    return pl.pallas_call(
        _even_in_kernel,
        grid=(n // tm,),
        in_specs=[row(D_MODEL), full(g), full(w_in), full(qg), full(kg), row(LANES), row(LANES)],
        out_specs=[row(W_A), row(W_A), row(W_B), row(W_B), row(W_B), row(W_B)],
        out_shape=[sds((n, W_A), F32), sds((n, W_A), BF16), sds((n, W_B), BF16), sds((n, W_B), BF16),
                   sds((n, W_B), BF16), sds((n, W_B), BF16)],
        compiler_params=_cparams(("parallel",)),
        name="even_in_proj",
    )(h, g, w_in, qg, kg, c_tab, s_tab)


def _flash_loop(qs_ref, k_ref, v_ref, m_scr, l_scr, acc_scr, qi, tq, tk, heads, dqk, dv):
    q0 = qi * tq
    tkt = min(tq, tk)
    n_main = q0 // tk
    base = n_main * tk
    n_tail_full = (q0 - base) // tkt
    n_tail_diag = tq // tkt
    m_scr[...] = jnp.full(m_scr.shape, NEG, F32)
    l_scr[...] = jnp.zeros(l_scr.shape, F32)
    acc_scr[...] = jnp.zeros(acc_scr.shape, F32)

    def step(ks, tk, masked):
        hs = range(heads)
        s = [lax.dot_general(qs_ref[g], k_ref[pl.ds(ks, tk), g * dqk:(g + 1) * dqk], (((1,), (1,)), ((), ())),
                             preferred_element_type=F32) for g in hs]
        if masked:
            row = q0 + lax.broadcasted_iota(jnp.int32, s[0].shape, 0) % tq
            col = ks + lax.broadcasted_iota(jnp.int32, s[0].shape, 1)
            s = [jnp.where(col <= row, x, NEG) for x in s]
        m_prev = [m_scr[g] for g in hs]
        m_new = [jnp.maximum(mp, jnp.max(x, axis=1, keepdims=True)) for mp, x in zip(m_prev, s)]
        alpha = [jnp.exp2(mp - mn) for mp, mn in zip(m_prev, m_new)]
        p = [jnp.exp2(x - jnp.tile(mn, (1, tk // LANES))) for x, mn in zip(s, m_new)]
        ones = jnp.ones((tk, LANES), BF16)
        pv = [jnp.dot(p[g].astype(BF16),
                      jnp.concatenate([v_ref[pl.ds(ks, tk), g * dv:(g + 1) * dv], ones], axis=1),
                      preferred_element_type=F32) for g in hs]
        for g in hs:
            m_scr[g] = m_new[g]
            l_scr[g] = alpha[g] * l_scr[g] + pv[g][:, dv:]
            acc_scr[g] = alpha[g] * acc_scr[g] + pv[g][:, 0:dv]

    def main_body(j, carry):
        step(pl.multiple_of(j * tk, tk), tk, False)
        return carry

    def tail_body(j, carry):
        step(pl.multiple_of(base + j * tkt, tkt), tkt, False)
        return carry

    def diag_body(j, carry):
        step(pl.multiple_of(base + (n_tail_full + j) * tkt, tkt), tkt, True)
        return carry

    lax.fori_loop(0, n_main, main_body, 0)
    if tkt < tk:
        lax.fori_loop(0, n_tail_full, tail_body, 0)
    lax.fori_loop(0, n_tail_diag, diag_body, 0)


def _diff_attn_kernel(q_ref, k_ref, v_ref, bg_ref, lam_ref, sub_ref, o_ref,
                      qs_scr, m_scr, l_scr, acc_scr, *, tq, tk, heads, lam_init):
    qi = pl.program_id(2)
    for g in range(heads):
        q = q_ref[:, g * DV_B:(g + 1) * DV_B]
        lane = lax.broadcasted_iota(jnp.int32, q.shape, 1)
        zero = jnp.zeros_like(q)
        qs_scr[g, 0:tq, :] = jnp.where(lane < DK_B, q, zero)
        qs_scr[g, tq:2 * tq, :] = jnp.where(lane >= DK_B, q, zero)
    _flash_loop(qs_scr, k_ref, v_ref, m_scr, l_scr, acc_scr, qi, tq, tk, heads, DV_B, DV_B)
    lv = lam_ref[...]
    lam = (jnp.exp(jnp.sum(lv[0:1] * lv[1:2], axis=1, keepdims=True))
           - jnp.exp(jnp.sum(lv[2:3] * lv[3:4], axis=1, keepdims=True)) + lam_init)
    for g in range(heads):
        o1 = acc_scr[g, 0:tq, :] / l_scr[g, 0:tq, :]
        o2 = acc_scr[g, tq:2 * tq, :] / l_scr[g, tq:2 * tq, :]
        o = o1 - lam * o2
        ms = jnp.mean(o * o, axis=-1, keepdims=True)
        o = o * lax.rsqrt(ms + EPS) * sub_ref[...] * (1.0 - lam_init)
        gate = bg_ref[:, g * DV_B:(g + 1) * DV_B].astype(F32)
        o_ref[:, g * DV_B:(g + 1) * DV_B] = (o * _silu(gate)).astype(BF16)


def _diff_attention(q, k, v, bgate, lam_vec, subln, batch, seq, lam_init):
    n = q.shape[0]
    tq = min(DIFF_TQ, seq)
    tk = min(ATT_TK, seq)
    nq = seq // tq
    heads = DIFF_HEADS_PER_STEP
    wid = heads * DV_B
    qspec = pl.BlockSpec((tq, wid), lambda b, h, i: (b * nq + i, h))
    kvspec = pl.BlockSpec((seq, wid), lambda b, h, i: (b, h))
    full = lambda a: pl.BlockSpec(a.shape, lambda b, h, i: (0,) * a.ndim)
    return pl.pallas_call(
        functools.partial(_diff_attn_kernel, tq=tq, tk=tk, heads=heads, lam_init=lam_init),
        grid=(batch, H_B // heads, nq),
        in_specs=[qspec, kvspec, kvspec, qspec, full(lam_vec), full(subln)],
        out_specs=qspec,
        out_shape=jax.ShapeDtypeStruct((n, W_B), BF16),
        scratch_shapes=[pltpu.VMEM((heads, 2 * tq, DV_B), BF16), pltpu.VMEM((heads, 2 * tq, LANES), F32),
                        pltpu.VMEM((heads, 2 * tq, LANES), F32), pltpu.VMEM((heads, 2 * tq, DV_B), F32)],
        compiler_params=_cparams(("parallel", "parallel", "arbitrary")),
        name="diff_attention",
    )(q, k, v, bgate, lam_vec, subln)


def _mla_attn_kernel(q_ref, k_ref, v_ref, cg_ref, o_ref, qs_scr, m_scr, l_scr, acc_scr, *, tq, tk, heads):
    qi = pl.program_id(2)
    for g in range(heads):
        qs_scr[g] = q_ref[:, g * HEAD_PAD:(g + 1) * HEAD_PAD]
    _flash_loop(qs_scr, k_ref, v_ref, m_scr, l_scr, acc_scr, qi, tq, tk, heads, HEAD_PAD, DV_C)
    for g in range(heads):
        o = acc_scr[g] / l_scr[g]
        gate = cg_ref[:, g * DV_C:(g + 1) * DV_C].astype(F32)
        o_ref[:, g * DV_C:(g + 1) * DV_C] = (o * _silu(gate)).astype(BF16)


def _mla_attention(q, k, v, cgate, batch, seq):
    n = q.shape[0]
    tq = min(MLA_TQ, seq)
    tk = min(ATT_TK, seq)
    nq = seq // tq
    heads = MLA_HEADS_PER_STEP
    qspec = pl.BlockSpec((tq, heads * HEAD_PAD), lambda b, h, i: (b * nq + i, h))
    kspec = pl.BlockSpec((seq, heads * HEAD_PAD), lambda b, h, i: (b, h))
    vspec = pl.BlockSpec((seq, heads * DV_C), lambda b, h, i: (b, h))
    ospec = pl.BlockSpec((tq, heads * DV_C), lambda b, h, i: (b * nq + i, h))
    return pl.pallas_call(
        functools.partial(_mla_attn_kernel, tq=tq, tk=tk, heads=heads),
        grid=(batch, H_C // heads, nq),
        in_specs=[qspec, kspec, vspec, ospec],
        out_specs=ospec,
        out_shape=jax.ShapeDtypeStruct((n, W_C), BF16),
        scratch_shapes=[pltpu.VMEM((heads, tq, HEAD_PAD), BF16), pltpu.VMEM((heads, tq, LANES), F32),
                        pltpu.VMEM((heads, tq, LANES), F32), pltpu.VMEM((heads, tq, DV_C), F32)],
        compiler_params=_cparams(("parallel", "parallel", "arbitrary")),
        name="mla_attention",
    )(q, k, v, cgate)


def _pool_kernel(x_ref, halo_ref, gate_ref, pw_ref, ps_ref, o_ref, buf_a, buf_b, *, tm, seq):
    i = pl.program_id(0)
    t0 = (i * tm) % seq
    x = x_ref[...]
    keep = (t0 > 0).astype(F32)
    buf_a[0:POOL_HALO, :] = halo_ref[...] * keep
    buf_a[POOL_HALO:POOL_HALO + tm, :] = x
    tot = tm + POOL_HALO
    src, dst = buf_a, buf_b
    levels = {}
    start = 0
    for w in (1, 2, 4, 8):
        start += 8
        cur = src[start:tot, :] + src[start - w:tot - w, :]
        dst[start:tot, :] = cur
        levels[2 * w] = cur[POOL_HALO - start:, :]
        src, dst = dst, src
    lane = lax.broadcasted_iota(jnp.int32, (tm, W_A), 1)
    grp = lane // POOL_GC
    win = jnp.where(grp == 0, levels[2], jnp.where(grp == 1, levels[4],
                    jnp.where(grp == 2, levels[8], levels[16])))
    wlane = jnp.where(grp == 0, 2, jnp.where(grp == 1, 4, jnp.where(grp == 2, 8, 16)))
    tpos = t0 + lax.broadcasted_iota(jnp.int32, (tm, W_A), 0)
    cnt = jnp.minimum(tpos + 1, wlane).astype(F32)
    pooled = win / cnt - x
    a = jnp.dot(pooled.astype(BF16), pw_ref[...], preferred_element_type=F32) * ps_ref[...]
    o_ref[...] = (a * _silu(gate_ref[...].astype(F32))).astype(BF16)


def _pool_mixer(a_in, a_gate, pool_w_bd, pool_scale, seq):
    n = a_in.shape[0]
    tm = min(ROW_TILE, seq)
    hb = tm // POOL_HALO
    row = lambda w: pl.BlockSpec((tm, w), lambda i: (i, 0))
    halo = pl.BlockSpec((POOL_HALO, W_A), lambda i: (jnp.maximum(i * hb - 1, 0), 0))
    full = lambda a: pl.BlockSpec(a.shape, lambda i: (0,) * a.ndim)
    return pl.pallas_call(
        functools.partial(_pool_kernel, tm=tm, seq=seq),
        grid=(n // tm,),
        in_specs=[row(W_A), halo, row(W_A), full(pool_w_bd), full(pool_scale)],
        out_specs=row(W_A),
        out_shape=jax.ShapeDtypeStruct((n, W_A), BF16),
        scratch_shapes=[pltpu.VMEM((tm + POOL_HALO, W_A), F32)] * 2,
        compiler_params=_cparams(("parallel",)),
        name="pool_mixer",
    )(a_in, a_in, a_gate, pool_w_bd, pool_scale)


def _out_kernel(h_ref, ma_ref, mb_ref, wo_ref, wg_ref, p_ref, wp_ref, o_ref, *, wa):
    m = (jnp.dot(ma_ref[...], wo_ref[0:wa, :], preferred_element_type=F32)
         + jnp.dot(mb_ref[...], wo_ref[wa:, :], preferred_element_type=F32))
    h1 = h_ref[...] + m
    gate = _sigmoid(jnp.dot(h1.astype(BF16), wg_ref[...], preferred_element_type=F32))
    pp = jnp.dot(p_ref[...].astype(BF16), wp_ref[...], preferred_element_type=F32)
    o_ref[...] = h1 + gate * pp


def _out_proj(h, mix_a, mix_b, w_out, w_gate, p, layer, w_proj):
    n = h.shape[0]
    tm = min(ROW_TILE, n)
    wa = mix_a.shape[1]
    row = lambda w: pl.BlockSpec((tm, w), lambda i: (i, 0))
    full = lambda a: pl.BlockSpec(a.shape, lambda i: (0,) * a.ndim)
    return pl.pallas_call(
        functools.partial(_out_kernel, wa=wa),
        grid=(n // tm,),
        in_specs=[row(D_MODEL), row(wa), row(mix_b.shape[1]), full(w_out), full(w_gate),
                  pl.BlockSpec((None, tm, PLE_DIM), lambda i: (layer, i, 0)), full(w_proj)],
        out_specs=row(D_MODEL),
        out_shape=jax.ShapeDtypeStruct((n, D_MODEL), F32),
        compiler_params=_cparams(("parallel",)),
        name="out_proj_ple",
    )(h, mix_a, mix_b, w_out, w_gate, p, w_proj)


O_CQ = 0
O_CKV = O_CQ + Q_LORA
O_KR = O_CKV + KV_LORA
O_CG = O_KR + LANES
O_QKV = O_CG + W_C
O_DG = O_QKV + CONV_CH
O_BA = O_DG + W_D
O_TOT = O_BA + LANES
HEAD_PAD = 2 * LANES


def _odd_in_kernel(h_ref, g_ref, w_ref, qan_ref, wuq_ref, kvan_ref, wukv_ref, qn_ref, kn_ref,
                   c_ref, s_ref, alog_ref, dtb_ref,
                   q_ref, k_ref, v_ref, cg_ref, qkv_ref, dg_ref, gb_ref):
    x = h_ref[...]
    ms = jnp.mean(x * x, axis=-1, keepdims=True)
    yb = (x * lax.rsqrt(ms + EPS) * g_ref[...]).astype(BF16)
    c_all = c_ref[...]
    s_all = s_ref[...]
    qn = qn_ref[...]
    kn = kn_ref[...]
    scale = DQK_C ** -0.5 * LOG2E

    def project(rs):
        def seg(lo, hi):
            return jnp.dot(yb[rs, :], w_ref[:, lo:hi], preferred_element_type=F32)

        cq = seg(O_CQ, O_CKV)
        ckv = seg(O_CKV, O_KR)
        kr = seg(O_KR, O_CG)
        z_cg = seg(O_CG, O_QKV)
        z_qkv = seg(O_QKV, O_DG)
        z_dg = seg(O_DG, O_BA)
        ba = seg(O_BA, O_TOT)
        cqn = cq * lax.rsqrt(jnp.mean(cq * cq, axis=-1, keepdims=True) + EPS) * qan_ref[...]
        ckvn = ckv * lax.rsqrt(jnp.mean(ckv * ckv, axis=-1, keepdims=True) + EPS) * kvan_ref[...]
        qu = jnp.dot(cqn.astype(BF16), wuq_ref[...], preferred_element_type=F32)
        kvu = jnp.dot(ckvn.astype(BF16), wukv_ref[...], preferred_element_type=F32)
        return kr, z_cg, z_qkv, z_dg, ba, qu, kvu

    def epilogue(rs, zs):
        kr, z_cg, z_qkv, z_dg, ba, qu, kvu = zs
        c = c_all[rs, :]
        s = s_all[rs, :]
        cg_ref[rs, :] = z_cg.astype(BF16)
        qkv_ref[rs, :] = z_qkv.astype(BF16)
        dg_ref[rs, :] = z_dg.astype(BF16)
        v_ref[rs, :] = kvu[:, H_C * D_NOPE:].astype(BF16)
        beta = _sigmoid(ba)
        g = -jnp.exp(alog_ref[...]) * jax.nn.softplus(ba + dtb_ref[...])
        lane = lax.broadcasted_iota(jnp.int32, ba.shape, 1)
        gb_ref[rs, :] = jnp.where(lane < H_D, beta, g)
        for hh in range(H_C):
            nope = qu[:, hh * D_NOPE:(hh + 1) * D_NOPE]
            rope = qu[:, H_C * D_NOPE + hh * LANES:H_C * D_NOPE + (hh + 1) * LANES]
            ss = jnp.sum(nope * nope, axis=-1, keepdims=True) + jnp.sum(rope * rope, axis=-1, keepdims=True)
            r = lax.rsqrt(ss * (1.0 / DQK_C) + EPS) * scale
            q_ref[rs, hh * HEAD_PAD:hh * HEAD_PAD + LANES] = (nope * r * qn[:, 0:LANES]).astype(BF16)
            q_ref[rs, hh * HEAD_PAD + LANES:(hh + 1) * HEAD_PAD] = _rope_block(
                rope * r * qn[:, LANES:2 * LANES], c, s, D_ROPE // 2).astype(BF16)
        kr_ss = jnp.sum(kr * kr, axis=-1, keepdims=True)
        kr_rot = _rope_block(kr * kn[:, LANES:2 * LANES], c, s, D_ROPE // 2)
        for hh in range(H_C):
            nope = kvu[:, hh * D_NOPE:(hh + 1) * D_NOPE]
            ss = jnp.sum(nope * nope, axis=-1, keepdims=True) + kr_ss
            r = lax.rsqrt(ss * (1.0 / DQK_C) + EPS)
            k_ref[rs, hh * HEAD_PAD:hh * HEAD_PAD + LANES] = (nope * r * kn[:, 0:LANES]).astype(BF16)
            k_ref[rs, hh * HEAD_PAD + LANES:(hh + 1) * HEAD_PAD] = (kr_rot * r).astype(BF16)

    half = x.shape[0] // 2
    halves = [slice(0, half), slice(half, 2 * half)]
    zs = [project(rs) for rs in halves]
    for rs, z in zip(halves, zs):
        epilogue(rs, z)


def _odd_in(h, g, w_in, qan, wuq, kvan, wukv, qn, kn, c_tab, s_tab, alog, dtb):
    n = h.shape[0]
    tm = min(ROW_TILE, n)
    row = lambda w: pl.BlockSpec((tm, w), lambda i: (i, 0))
    col = lambda w: pl.BlockSpec((w, tm), lambda i: (0, i))
    full = lambda a: pl.BlockSpec(a.shape, lambda i: (0,) * a.ndim)
    sds = jax.ShapeDtypeStruct
    return pl.pallas_call(
        _odd_in_kernel,
        grid=(n // tm,),
        in_specs=[row(D_MODEL), full(g), full(w_in), full(qan), full(wuq), full(kvan), full(wukv),
                  full(qn), full(kn), row(LANES), row(LANES), full(alog), full(dtb)],
        out_specs=[row(H_C * HEAD_PAD), row(H_C * HEAD_PAD), row(W_C), row(W_C), row(CONV_CH), row(W_D),
                   row(LANES)],
        out_shape=[sds((n, H_C * HEAD_PAD), BF16), sds((n, H_C * HEAD_PAD), BF16), sds((n, W_C), BF16),
                   sds((n, W_C), BF16), sds((n, CONV_CH), BF16), sds((n, W_D), BF16), sds((n, LANES), F32)],
        compiler_params=_cparams(("parallel",)),
        name="odd_in_proj",
    )(h, g, w_in, qan, wuq, kvan, wukv, qn, kn, c_tab, s_tab, alog, dtb)


def _gdn_prep_kernel(x_ref, halo_ref, cw_ref, gbc_ref, lhs1_ref, lhs2_ref, u_ref, gl_ref,
                     buf, ybuf, *, tm, seq):
    i = pl.program_id(0)
    t0 = (i * tm) % seq
    keep = (t0 > 0).astype(F32)
    buf[0:CONV_HALO, :] = halo_ref[...].astype(F32) * keep
    buf[CONV_HALO:CONV_HALO + tm, :] = x_ref[...].astype(F32)
    cw = cw_ref[...]
    y = None
    for j in range(CONV_K):
        off = CONV_HALO - (CONV_K - 1) + j
        term = buf[off:off + tm, :] * cw[j:j + 1, :]
        y = term if y is None else y + term
    ybuf[...] = _silu(y)

    ri = lax.broadcasted_iota(jnp.int32, (tm, tm), 0)
    ci = lax.broadcasted_iota(jnp.int32, (tm, tm), 1)
    low = jnp.where(((ri // CHUNK) == (ci // CHUNK)) & (ri >= ci), 1.0, 0.0).astype(F32)
    i2 = lax.broadcasted_iota(jnp.int32, (LANES, LANES), 0)
    j2 = lax.broadcasted_iota(jnp.int32, (LANES, LANES), 1)
    eye_l = jnp.where(i2 == j2, 1.0, 0.0).astype(BF16)
    gbc = gbc_ref[...]
    gcs = _split3_dot(low, gbc)
    gcs_t = _split3_dot_nt(eye_l, gcs)

    pdim = H_D * CHUNK
    r4 = lax.broadcasted_iota(jnp.int32, (pdim, pdim), 0)
    c4 = lax.broadcasted_iota(jnp.int32, (pdim, pdim), 1)
    same = (r4 // CHUNK) == (c4 // CHUNK)
    incl = same & (r4 >= c4)
    strict = same & (r4 > c4)
    eye = jnp.where(r4 == c4, 1.0, 0.0).astype(F32)
    zblk = jnp.zeros((CHUNK, DK_D), F32)

    def block_diag(blocks):
        return jnp.concatenate(
            [jnp.concatenate([blocks[h] if j == h else zblk for j in range(H_D)], axis=1) for h in range(H_D)],
            axis=0)

    chunks = list(range(tm // CHUNK))
    heads = list(range(H_D))

    def load_qkv(c):
        r0 = c * CHUNK
        qs, ks, vs = [], [], []
        for hh in heads:
            l0 = hh * DK_D
            qh = ybuf[r0:r0 + CHUNK, l0:l0 + DK_D]
            kh = ybuf[r0:r0 + CHUNK, H_D * DK_D + l0:H_D * DK_D + l0 + DK_D]
            vs.append(ybuf[r0:r0 + CHUNK, 2 * H_D * DK_D + l0:2 * H_D * DK_D + l0 + DV_D])
            qs.append(qh * lax.rsqrt(jnp.sum(qh * qh, axis=-1, keepdims=True) + EPS) * (DK_D ** -0.5))
            ks.append(kh * lax.rsqrt(jnp.sum(kh * kh, axis=-1, keepdims=True) + EPS))
        return qs, ks, vs

    def decay_terms(c):
        r0 = c * CHUNK
        beta = jnp.concatenate([gbc[r0:r0 + CHUNK, hh:hh + 1] for hh in heads], axis=0)
        gcol = jnp.concatenate([gcs[r0:r0 + CHUNK, H_D + hh:H_D + hh + 1] for hh in heads], axis=0)
        grow = jnp.concatenate([gcs_t[H_D + hh:H_D + hh + 1, r0:r0 + CHUNK] for hh in heads], axis=1)
        glasts = [gcs[r0 + CHUNK - 1:r0 + CHUNK, H_D + hh:H_D + hh + 1] for hh in heads]
        glast = jnp.concatenate([jnp.broadcast_to(gl, (CHUNK, 1)) for gl in glasts], axis=0)
        gamma = jnp.exp(jnp.where(incl, gcol - grow, -jnp.inf))
        return beta, gcol, glasts, glast, gamma

    qkv = [load_qkv(c) for c in chunks]
    q_st = [jnp.concatenate(x[0], axis=0) for x in qkv]
    k_st = [jnp.concatenate(x[1], axis=0) for x in qkv]
    v_st = [jnp.concatenate(x[2], axis=0) for x in qkv]
    k_bd = [block_diag(x[1]).astype(BF16) for x in qkv]
    q_bd = [block_diag(x[0]).astype(BF16) for x in qkv]
    qkkk = [_bdot_nt(jnp.concatenate([qb, kb], axis=0), kb) for qb, kb in zip(q_bd, k_bd)]
    dec = [decay_terms(c) for c in chunks]
    a = [jnp.where(strict, x[pdim:, :] * d[4] * d[0], 0.0) for x, d in zip(qkkk, dec)]
    t = [eye - x for x in a]
    pw = a
    for _ in range(5):
        pw = [_bdot(x, x) for x in pw]
        t = [x + _bdot(x, y) for x, y in zip(t, pw)]
    egc = [jnp.exp(d[1]) for d in dec]
    uw = [_bdot(tt, jnp.concatenate([v * d[0], k * (d[0] * e)], axis=1))
          for tt, v, k, d, e in zip(t, v_st, k_st, dec, egc)]
    kdt = [_bdot_nt(eye_l, k * jnp.exp(d[3] - d[1])).astype(BF16) for k, d in zip(k_st, dec)]
    for c in chunks:
        qkg = (qkkk[c][0:pdim, :] * dec[c][4]).astype(BF16)
        q_dec = (q_st[c] * egc[c]).astype(BF16)
        for hh in heads:
            idx = c * H_D + hh
            hrows = slice(hh * CHUNK, (hh + 1) * CHUNK)
            lhs1_ref[idx, 0:CHUNK, :] = uw[c][hrows, DV_D:].astype(BF16)
            lhs1_ref[idx, CHUNK:2 * CHUNK, :] = q_dec[hrows, :]
            u_ref[idx] = uw[c][hrows, 0:DV_D].astype(BF16)
            gl_ref[c, hh:hh + 1, :] = jnp.broadcast_to(jnp.exp(dec[c][2][hh]), (1, LANES))
        for pp in range(H_D // 2):
            idx2 = c * (H_D // 2) + pp
            lhs2_ref[idx2, 0:LANES, :] = qkg[pp * LANES:(pp + 1) * LANES, pp * LANES:(pp + 1) * LANES]
            lhs2_ref[idx2, LANES:2 * LANES, :] = kdt[c][:, pp * LANES:(pp + 1) * LANES]


def _gdn_prep(qkv, conv_w, gb, seq):
    n = qkv.shape[0]
    tm = min(GDN_PREP_TILE, seq)
    nch = tm // CHUNK
    hb = tm // CONV_HALO
    row = lambda w: pl.BlockSpec((tm, w), lambda i: (i, 0))
    halo = pl.BlockSpec((CONV_HALO, CONV_CH), lambda i: (jnp.maximum(i * hb - 1, 0), 0))
    full = lambda a: pl.BlockSpec(a.shape, lambda i: (0,) * a.ndim)
    ch3 = lambda m, r, w: pl.BlockSpec((nch * m, r, w), lambda i: (i, 0, 0))
    sds = jax.ShapeDtypeStruct
    nc = n // CHUNK
    return pl.pallas_call(
        functools.partial(_gdn_prep_kernel, tm=tm, seq=seq),
        grid=(n // tm,),
        in_specs=[row(CONV_CH), halo, full(conv_w), row(LANES)],
        out_specs=[ch3(H_D, 2 * CHUNK, DK_D), ch3(H_D // 2, 2 * LANES, LANES), ch3(H_D, CHUNK, DV_D),
                   pl.BlockSpec((nch, H_D, LANES), lambda i: (i, 0, 0))],
        out_shape=[sds((nc * H_D, 2 * CHUNK, DK_D), BF16), sds((nc * H_D // 2, 2 * LANES, LANES), BF16),
                   sds((nc * H_D, CHUNK, DV_D), BF16), sds((nc, H_D, LANES), F32)],
        scratch_shapes=[pltpu.VMEM((tm + CONV_HALO, CONV_CH), F32), pltpu.VMEM((tm, CONV_CH), F32)],
        compiler_params=_cparams(("parallel",)),
        name="gdn_chunk_prep",
    )(qkv, qkv, conv_w, gb)


def _gdn_scan_kernel(lhs1_ref, lhs2_ref, u_ref, gl_ref, dg_ref, on_ref, o_ref, s_scr, *, nb, tb):
    @pl.when(pl.program_id(0) == 0)
    def _():
        s_scr[...] = jnp.zeros(s_scr.shape, F32)

    on = on_ref[...]
    zb = jnp.zeros((2 * CHUNK, DK_D), BF16)
    left = lax.broadcasted_iota(jnp.int32, (DK_D, LANES), 1) < CHUNK

    def finish(o, b, rows, hh):
        ms = jnp.mean(o * o, axis=-1, keepdims=True)
        o = o * lax.rsqrt(ms + EPS) * on
        gate = dg_ref[b, rows, hh * DV_D:(hh + 1) * DV_D].astype(F32)
        o_ref[b, rows, hh * DV_D:(hh + 1) * DV_D] = (o * _silu(gate)).astype(BF16)

    def chunk_body(c, carry):
        rows = pl.ds(pl.multiple_of(c * CHUNK, CHUNK), CHUNK)
        chains = [(b, 2 * pp) for b in range(nb) for pp in range(H_D // 2)]
        st = [(s_scr[b * H_D + h0], s_scr[b * H_D + h0 + 1]) for b, h0 in chains]
        r1 = [jnp.dot(jnp.concatenate([jnp.concatenate([lhs1_ref[b, c * H_D + h0], zb], axis=1),
                                       jnp.concatenate([zb, lhs1_ref[b, c * H_D + h0 + 1]], axis=1)], axis=0),
                      jnp.concatenate([s0, s1], axis=0).astype(BF16), preferred_element_type=F32)
              for (b, h0), (s0, s1) in zip(chains, st)]
        v2 = [jnp.concatenate([u_ref[b, c * H_D + h0].astype(F32) - x[0:CHUNK, :],
                               u_ref[b, c * H_D + h0 + 1].astype(F32) - x[2 * CHUNK:3 * CHUNK, :]],
                              axis=0).astype(BF16) for (b, h0), x in zip(chains, r1)]
        r2 = []
        for (b, h0), v in zip(chains, v2):
            blk = lhs2_ref[b, c * (H_D // 2) + h0 // 2]
            kd = blk[LANES:2 * LANES, :]
            zk = jnp.zeros_like(kd)
            l2 = jnp.concatenate([blk[0:LANES, :], jnp.where(left, kd, zk), jnp.where(left, zk, kd)], axis=0)
            r2.append(jnp.dot(l2, v, preferred_element_type=F32))
        for (b, h0), (s0, s1), x, y in zip(chains, st, r1, r2):
            s_scr[b * H_D + h0] = s0 * gl_ref[b, c, h0:h0 + 1, :] + y[LANES:LANES + DK_D, :]
            s_scr[b * H_D + h0 + 1] = s1 * gl_ref[b, c, h0 + 1:h0 + 2, :] + y[LANES + DK_D:LANES + 2 * DK_D, :]
            finish(x[CHUNK:2 * CHUNK, :] + y[0:CHUNK, :], b, rows, h0)
            finish(x[3 * CHUNK:4 * CHUNK, :] + y[CHUNK:2 * CHUNK, :], b, rows, h0 + 1)
        return carry

    lax.fori_loop(0, tb // CHUNK, chunk_body, 0)


def _gdn_scan(lhs1, lhs2, u, gl, d_gate, o_norm, batch, seq):
    tb = min(GDN_SCAN_TILE, seq)
    nch = tb // CHUNK
    ncb = seq // CHUNK
    lhs1 = lhs1.reshape(batch, ncb * H_D, 2 * CHUNK, DK_D)
    lhs2 = lhs2.reshape(batch, ncb * H_D // 2, 2 * LANES, LANES)
    u = u.reshape(batch, ncb * H_D, CHUNK, DV_D)
    gl = gl.reshape(batch, ncb, H_D, LANES)
    d_gate = d_gate.reshape(batch, seq, W_D)
    ch4 = lambda r, w: pl.BlockSpec((batch, nch * H_D, r, w), lambda i: (0, i, 0, 0))
    tok = pl.BlockSpec((batch, tb, W_D), lambda i: (0, i, 0))
    out = pl.pallas_call(
        functools.partial(_gdn_scan_kernel, nb=batch, tb=tb),
        grid=(seq // tb,),
        in_specs=[ch4(2 * CHUNK, DK_D),
                  pl.BlockSpec((batch, nch * H_D // 2, 2 * LANES, LANES), lambda i: (0, i, 0, 0)),
                  ch4(CHUNK, DV_D),
                  pl.BlockSpec((batch, nch, H_D, LANES), lambda i: (0, i, 0, 0)), tok,
                  pl.BlockSpec(o_norm.shape, lambda i: (0, 0))],
        out_specs=tok,
        out_shape=jax.ShapeDtypeStruct((batch, seq, W_D), BF16),
        scratch_shapes=[pltpu.VMEM((batch * H_D, DK_D, DV_D), F32)],
        compiler_params=_cparams(("arbitrary",)),
        name="gdn_state_scan",
    )(lhs1, lhs2, u, gl, d_gate, o_norm)
    return out.reshape(batch * seq, W_D)


def _rope_patterns(rot_dim, theta, period):
    half = rot_dim // 2
    inv = jnp.power(jnp.float32(theta), -jnp.arange(half, dtype=F32) * (2.0 / rot_dim))
    lane = np.arange(LANES)
    in_rot = (lane % period) < rot_dim
    idx = jnp.asarray(lane % half)
    inv_l = jnp.where(jnp.asarray(in_rot), inv[idx], 0.0).astype(F32)[None, :]
    sgn = np.where(in_rot, np.where((lane % period) < half, -1.0, 1.0), 0.0).astype(np.float32)[None, :]
    return inv_l, jnp.asarray(sgn)


def _pad_cols(w, width):
    return jnp.pad(w, ((0, 0), (0, width - w.shape[1])))


def _odd_w_in_layout(w):
    cq, ckv, kr, cg, qkv, db, da, dg = jnp.split(
        w, np.cumsum((Q_LORA, KV_LORA, D_ROPE, W_C, CONV_CH, H_D, H_D, W_D))[:-1].tolist(), axis=1)
    return jnp.concatenate([cq, ckv, _pad_cols(kr, LANES), cg, qkv, dg,
                            _pad_cols(jnp.concatenate([db, da], axis=1), LANES)], axis=1)


def _wuq_layout(w):
    w = w.reshape(Q_LORA, H_C, DQK_C)
    nope = w[:, :, :D_NOPE].reshape(Q_LORA, H_C * D_NOPE)
    rope = jnp.pad(w[:, :, D_NOPE:], ((0, 0), (0, 0), (0, LANES - D_ROPE))).reshape(Q_LORA, H_C * LANES)
    return jnp.concatenate([nope, rope], axis=1)


def _wukv_layout(w):
    w = w.reshape(KV_LORA, H_C, D_NOPE + DV_C)
    return jnp.concatenate([w[:, :, :D_NOPE].reshape(KV_LORA, H_C * D_NOPE),
                            w[:, :, D_NOPE:].reshape(KV_LORA, H_C * DV_C)], axis=1)


def _head_gain_layout(g):
    return jnp.pad(g, (0, HEAD_PAD - DQK_C))[None, :].astype(F32)


def kernel(x, p, positions, norm_g, ple_w_gate, ple_w_proj, ev_w_in, ev_pool_w, ev_pool_scale, ev_q_norm, ev_k_norm, ev_lambda, ev_subln, ev_w_out, od_w_in, od_q_a_norm, od_w_uq, od_kv_a_norm, od_w_ukv, od_q_norm, od_k_norm, od_conv_w, od_a_log, od_dt_bias, od_o_norm, od_w_out):
    batch, seq, _ = x.shape
    depth = p.shape[0]
    n = batch * seq
    h = x.reshape(n, D_MODEL)
    pos_b = jnp.broadcast_to(positions.astype(F32).reshape(n, 1), (n, LANES))

    inv_e, sgn_e = _rope_patterns(B_ROT, ROPE_THETA, DK_B)
    ce, se = _rope_tables(pos_b, inv_e, sgn_e)
    inv_o, sgn_o = _rope_patterns(D_ROPE, MLA_THETA, LANES)
    co, so = _rope_tables(pos_b, inv_o, sgn_o)


    for i in range(depth):
        j = i // 2
        g = norm_g[i][None, :]
        if i % 2 == 0:
            w_in = ev_w_in[j].astype(BF16)
            qg = (jnp.tile(ev_q_norm[j], 2 * H_B) * (DK_B ** -0.5 * LOG2E))[None, :]
            kg = jnp.tile(ev_k_norm[j], 2 * H_B)[None, :]
            a_in, a_gate, q, k, vt, b_gate = _even_in(h, g, w_in, qg, kg, ce, se)
            lam_init = 0.8 - 0.6 * math.exp(-0.3 * i)
            ob = _diff_attention(q, k, vt, b_gate, ev_lambda[j], ev_subln[j][None, :], batch, seq, lam_init)
            pw = jax.scipy.linalg.block_diag(*[ev_pool_w[j][gi] for gi in range(len(POOL_WINDOWS))]).astype(BF16)
            oa = _pool_mixer(a_in, a_gate, pw, ev_pool_scale[j][None, :], seq)
            w_out = ev_w_out[j].astype(BF16)
        else:
            w_in = _odd_w_in_layout(od_w_in[j]).astype(BF16)
            alog = jnp.zeros((LANES,), F32).at[H_D:2 * H_D].set(od_a_log[j])[None, :]
            dtb = jnp.zeros((LANES,), F32).at[H_D:2 * H_D].set(od_dt_bias[j])[None, :]
            q, k, vt, c_gate, qkv, d_gate, gb = _odd_in(
                h, g, w_in, od_q_a_norm[j][None, :], _wuq_layout(od_w_uq[j]).astype(BF16),
                od_kv_a_norm[j][None, :], _wukv_layout(od_w_ukv[j]).astype(BF16),
                _head_gain_layout(od_q_norm[j]), _head_gain_layout(od_k_norm[j]), co, so, alog, dtb)
            oa = _mla_attention(q, k, vt, c_gate, batch, seq)
            lhs1, lhs2, u, gl = _gdn_prep(qkv, od_conv_w[j], gb, seq)
            ob = _gdn_scan(lhs1, lhs2, u, gl, d_gate, od_o_norm[j][None, :], batch, seq)
            w_out = od_w_out[j].astype(BF16)
        h = _out_proj(h, oa, ob, w_out, ple_w_gate[i].astype(BF16), p.reshape(depth, n, PLE_DIM), i,
                      ple_w_proj[i].astype(BF16))
    return h.reshape(batch, seq, D_MODEL)
```

```python
import functools
import math

import numpy as np
import jax
import jax.numpy as jnp
from jax import lax
from jax.experimental import pallas as pl
from jax.experimental.pallas import tpu as pltpu

F32 = jnp.float32
BF16 = jnp.bfloat16

D_MODEL = 1024
PLE_DIM = 256
EPS = 1e-6
W_A = 256
POOL_WINDOWS = (2, 4, 8, 16)
POOL_GC = 64
W_B = 768
DV_B = 128
H_B = 6
DK_B = 64
B_ROT = 16
ROPE_THETA = 500000.0
W_C = 512
DV_C = 128
H_C = 4
D_NOPE = 128
D_ROPE = 64
DQK_C = 192
Q_LORA = 256
KV_LORA = 128
MLA_THETA = 10000.0
W_D = 512
DK_D = 128
DV_D = 128
H_D = 4
CONV_K = 4
CHUNK = 64
CONV_CH = H_D * (2 * DK_D + DV_D)

LANES = 128
VMEM_LIMIT = 48 * 1024 * 1024
NEG = -1e30
LOG2E = math.log2(math.e)

ROW_TILE = 512
IN_PROJ_PARTS = 2
DIFF_TK = 1024
MLA_TK = 1024
POOL_HALO = 32
CONV_PAD = 8
GDN_PREP_TILE = 256
GDN_SCAN_TILE = 256
DIFF_HEADS_PER_STEP = 2
MLA_HEADS_PER_STEP = 2
DIFF_TQ = 512
MLA_TQ = 1024


def _cparams(sem):
    return pltpu.CompilerParams(dimension_semantics=sem, vmem_limit_bytes=VMEM_LIMIT)


def _silu(x):
    return x * (1.0 / (1.0 + jnp.exp(-x)))


def _sigmoid(x):
    return 1.0 / (1.0 + jnp.exp(-x))


def _bdot(a, b):
    return jnp.dot(a.astype(BF16), b.astype(BF16), preferred_element_type=F32)


def _bdot_nt(a, b):
    return lax.dot_general(a.astype(BF16), b.astype(BF16), (((1,), (1,)), ((), ())),
                           preferred_element_type=F32)


def _split3_dot(a_exact, b):
    b0 = b.astype(BF16)
    r1 = b - b0.astype(F32)
    b1 = r1.astype(BF16)
    b2 = (r1 - b1.astype(F32)).astype(BF16)
    a = a_exact.astype(BF16)
    return (jnp.dot(a, b0, preferred_element_type=F32) + jnp.dot(a, b1, preferred_element_type=F32)
            + jnp.dot(a, b2, preferred_element_type=F32))


def _split3_dot_nt(a_exact, b):
    b0 = b.astype(BF16)
    r1 = b - b0.astype(F32)
    b1 = r1.astype(BF16)
    b2 = (r1 - b1.astype(F32)).astype(BF16)
    return _bdot_nt(a_exact, b0) + _bdot_nt(a_exact, b1) + _bdot_nt(a_exact, b2)


def _rope_table_kernel(pos_ref, inv_ref, sgn_ref, c_ref, s_ref):
    ang = pos_ref[...] * inv_ref[...]
    c_ref[...] = jnp.cos(ang)
    s_ref[...] = jnp.sin(ang) * sgn_ref[...]


def _rope_tables(pos_b, inv, sgn):
    n = pos_b.shape[0]
    tm = min(1024, n)
    row = pl.BlockSpec((tm, LANES), lambda i: (i, 0))
    par = pl.BlockSpec((1, LANES), lambda i: (0, 0))
    return pl.pallas_call(
        _rope_table_kernel,
        grid=(n // tm,),
        in_specs=[row, par, par],
        out_specs=[row, row],
        out_shape=[jax.ShapeDtypeStruct((n, LANES), F32)] * 2,
        compiler_params=_cparams(("parallel",)),
        name="rope_tables",
    )(pos_b, inv, sgn)


def _rope_block(x, c, s, half):
    fwd = pltpu.roll(x, LANES - half, 1)
    bwd = pltpu.roll(x, half, 1)
    lane = lax.broadcasted_iota(jnp.int32, x.shape, 1)
    rot = jnp.where((lane % (2 * half)) < half, fwd, bwd)
    return x * c + rot * s


def _even_in_kernel(h_ref, g_ref, w_ref, qg_ref, kg_ref, c_ref, s_ref,
                    ain_ref, agate_ref, q_ref, k_ref, v_ref, bgate_ref):
    x = h_ref[...]
    ms = jnp.mean(x * x, axis=-1, keepdims=True)
    yb = (x * lax.rsqrt(ms + EPS) * g_ref[...]).astype(BF16)

    c_all = c_ref[...]
    s_all = s_ref[...]
    o = 2 * W_A
    bounds = [(0, W_A), (W_A, 2 * W_A), (o, o + W_B), (o + W_B, o + 2 * W_B), (o + 2 * W_B, o + 3 * W_B),
              (o + 3 * W_B, o + 4 * W_B)]

    def project(rs):
        return [jnp.dot(yb[rs, :], w_ref[:, lo:hi], preferred_element_type=F32) for lo, hi in bounds]

    def qk_prep(rs, z, gain_ref, out_ref, c, s):
        gain = gain_ref[...]
        first = lax.broadcasted_iota(jnp.int32, (z.shape[0], LANES), 1) < DK_B
        for j in range(W_B // LANES):
            blk = z[:, j * LANES:(j + 1) * LANES]
            sq = blk * blk
            ss0 = jnp.sum(jnp.where(first, sq, 0.0), axis=-1, keepdims=True)
            ss1 = jnp.sum(jnp.where(first, 0.0, sq), axis=-1, keepdims=True)
            r = jnp.where(first, lax.rsqrt(ss0 * (1.0 / DK_B) + EPS), lax.rsqrt(ss1 * (1.0 / DK_B) + EPS))
            zn = blk * r * gain[:, j * LANES:(j + 1) * LANES]
            out_ref[rs, j * LANES:(j + 1) * LANES] = _rope_block(zn, c, s, B_ROT // 2).astype(BF16)

    def epilogue(rs, zs):
        z_ain, z_agate, z_q, z_k, z_v, z_bg = zs
        ain_ref[rs, :] = z_ain
        agate_ref[rs, :] = z_agate.astype(BF16)
        v_ref[rs, :] = z_v.astype(BF16)
        bgate_ref[rs, :] = z_bg.astype(BF16)
        qk_prep(rs, z_q, qg_ref, q_ref, c_all[rs, :], s_all[rs, :])
        qk_prep(rs, z_k, kg_ref, k_ref, c_all[rs, :], s_all[rs, :])

    part = x.shape[0] // IN_PROJ_PARTS
    parts = [slice(k * part, (k + 1) * part) for k in range(IN_PROJ_PARTS)]
    z_prev = project(parts[0])
    for k in range(1, IN_PROJ_PARTS):
        z_next = project(parts[k])
        epilogue(parts[k - 1], z_prev)
        z_prev = z_next
    epilogue(parts[-1], z_prev)


def _even_in(h, g, w_in, qg, kg, c_tab, s_tab):
    n = h.shape[0]
    tm = min(ROW_TILE, n)
    row = lambda w: pl.BlockSpec((tm, w), lambda i: (i, 0))
    col = lambda w: pl.BlockSpec((w, tm), lambda i: (0, i))
    full = lambda a: pl.BlockSpec(a.shape, lambda i: (0,) * a.ndim)
    sds = jax.ShapeDtypeStruct
    return pl.pallas_call(
        _even_in_kernel,
        grid=(n // tm,),
        in_specs=[row(D_MODEL), full(g), full(w_in), full(qg), full(kg), row(LANES), row(LANES)],
        out_specs=[row(W_A), row(W_A), row(W_B), row(W_B), row(W_B), row(W_B)],
        out_shape=[sds((n, W_A), F32), sds((n, W_A), BF16), sds((n, W_B), BF16), sds((n, W_B), BF16),
                   sds((n, W_B), BF16), sds((n, W_B), BF16)],
        compiler_params=_cparams(("parallel",)),
        name="even_in_proj",
    )(h, g, w_in, qg, kg, c_tab, s_tab)


def _flash_loop(qs_ref, k_ref, v_ref, m_scr, l_scr, acc_scr, qi, tq, tk, heads, dqk, dv):
    q0 = qi * tq
    tkt = min(tq, tk)
    n_main = q0 // tk
    base = n_main * tk
    n_tail_full = (q0 - base) // tkt
    n_tail_diag = tq // tkt
    m_scr[...] = jnp.full(m_scr.shape, NEG, F32)
    l_scr[...] = jnp.zeros(l_scr.shape, F32)
    acc_scr[...] = jnp.zeros(acc_scr.shape, F32)

    def step(ks, tk, masked):
        hs = range(heads)
        s = [lax.dot_general(qs_ref[g], k_ref[pl.ds(ks, tk), g * dqk:(g + 1) * dqk], (((1,), (1,)), ((), ())),
                             preferred_element_type=F32) for g in hs]
        if masked:
            row = q0 + lax.broadcasted_iota(jnp.int32, s[0].shape, 0) % tq
            col = ks + lax.broadcasted_iota(jnp.int32, s[0].shape, 1)
            s = [jnp.where(col <= row, x, NEG) for x in s]
        m_prev = [m_scr[g] for g in hs]
        m_new = [jnp.maximum(mp, jnp.max(x, axis=1, keepdims=True)) for mp, x in zip(m_prev, s)]
        alpha = [jnp.exp2(mp - mn) for mp, mn in zip(m_prev, m_new)]
        p = [jnp.exp2(x - jnp.tile(mn, (1, tk // LANES))) for x, mn in zip(s, m_new)]
        ones = jnp.ones((tk, LANES), BF16)
        pv = [jnp.dot(p[g].astype(BF16),
                      jnp.concatenate([v_ref[pl.ds(ks, tk), g * dv:(g + 1) * dv], ones], axis=1),
                      preferred_element_type=F32) for g in hs]
        for g in hs:
            m_scr[g] = m_new[g]
            l_scr[g] = alpha[g] * l_scr[g] + pv[g][:, dv:]
            acc_scr[g] = alpha[g] * acc_scr[g] + pv[g][:, 0:dv]

    def main_body(j, carry):
        step(pl.multiple_of(j * tk, tk), tk, False)
        return carry

    def tail_body(j, carry):
        step(pl.multiple_of(base + j * tkt, tkt), tkt, False)
        return carry

    def diag_body(j, carry):
        step(pl.multiple_of(base + (n_tail_full + j) * tkt, tkt), tkt, True)
        return carry

    lax.fori_loop(0, n_main, main_body, 0)
    if tkt < tk:
        lax.fori_loop(0, n_tail_full, tail_body, 0)
    lax.fori_loop(0, n_tail_diag, diag_body, 0)


def _diff_attn_kernel(q_ref, k_ref, v_ref, bg_ref, lam_ref, sub_ref, o_ref,
                      qs_scr, m_scr, l_scr, acc_scr, *, tq, tk, heads, lam_init):
    qi = pl.program_id(2)
    for g in range(heads):
        q = q_ref[:, g * DV_B:(g + 1) * DV_B]
        lane = lax.broadcasted_iota(jnp.int32, q.shape, 1)
        zero = jnp.zeros_like(q)
        qs_scr[g, 0:tq, :] = jnp.where(lane < DK_B, q, zero)
        qs_scr[g, tq:2 * tq, :] = jnp.where(lane >= DK_B, q, zero)
    _flash_loop(qs_scr, k_ref, v_ref, m_scr, l_scr, acc_scr, qi, tq, tk, heads, DV_B, DV_B)
    lv = lam_ref[...]
    lam = (jnp.exp(jnp.sum(lv[0:1] * lv[1:2], axis=1, keepdims=True))
           - jnp.exp(jnp.sum(lv[2:3] * lv[3:4], axis=1, keepdims=True)) + lam_init)
    for g in range(heads):
        o1 = acc_scr[g, 0:tq, :] / l_scr[g, 0:tq, :]
        o2 = acc_scr[g, tq:2 * tq, :] / l_scr[g, tq:2 * tq, :]
        o = o1 - lam * o2
        ms = jnp.mean(o * o, axis=-1, keepdims=True)
        o = o * lax.rsqrt(ms + EPS) * sub_ref[...] * (1.0 - lam_init)
        gate = bg_ref[:, g * DV_B:(g + 1) * DV_B].astype(F32)
        o_ref[:, g * DV_B:(g + 1) * DV_B] = (o * _silu(gate)).astype(BF16)


def _diff_attention(q, k, v, bgate, lam_vec, subln, batch, seq, lam_init):
    n = q.shape[0]
    tq = min(DIFF_TQ, seq)
    tk = min(DIFF_TK, seq)
    nq = seq // tq
    heads = DIFF_HEADS_PER_STEP
    wid = heads * DV_B
    qspec = pl.BlockSpec((tq, wid), lambda b, h, i: (b * nq + i, h))
    kvspec = pl.BlockSpec((seq, wid), lambda b, h, i: (b, h))
    full = lambda a: pl.BlockSpec(a.shape, lambda b, h, i: (0,) * a.ndim)
    return pl.pallas_call(
        functools.partial(_diff_attn_kernel, tq=tq, tk=tk, heads=heads, lam_init=lam_init),
        grid=(batch, H_B // heads, nq),
        in_specs=[qspec, kvspec, kvspec, qspec, full(lam_vec), full(subln)],
        out_specs=qspec,
        out_shape=jax.ShapeDtypeStruct((n, W_B), BF16),
        scratch_shapes=[pltpu.VMEM((heads, 2 * tq, DV_B), BF16), pltpu.VMEM((heads, 2 * tq, LANES), F32),
                        pltpu.VMEM((heads, 2 * tq, LANES), F32), pltpu.VMEM((heads, 2 * tq, DV_B), F32)],
        compiler_params=_cparams(("parallel", "parallel", "arbitrary")),
        name="diff_attention",
    )(q, k, v, bgate, lam_vec, subln)


def _mla_attn_kernel(q_ref, k_ref, v_ref, cg_ref, o_ref, qs_scr, m_scr, l_scr, acc_scr, *, tq, tk, heads):
    qi = pl.program_id(2)
    for g in range(heads):
        qs_scr[g] = q_ref[:, g * HEAD_PAD:(g + 1) * HEAD_PAD]
    _flash_loop(qs_scr, k_ref, v_ref, m_scr, l_scr, acc_scr, qi, tq, tk, heads, HEAD_PAD, DV_C)
    for g in range(heads):
        o = acc_scr[g] / l_scr[g]
        gate = cg_ref[:, g * DV_C:(g + 1) * DV_C].astype(F32)
        o_ref[:, g * DV_C:(g + 1) * DV_C] = (o * _silu(gate)).astype(BF16)


def _mla_attention(q, k, v, cgate, batch, seq):
    n = q.shape[0]
    tq = min(MLA_TQ, seq)
    tk = min(MLA_TK, seq)
    nq = seq // tq
    heads = MLA_HEADS_PER_STEP
    qspec = pl.BlockSpec((tq, heads * HEAD_PAD), lambda b, h, i: (b * nq + i, h))
    kspec = pl.BlockSpec((seq, heads * HEAD_PAD), lambda b, h, i: (b, h))
    vspec = pl.BlockSpec((seq, heads * DV_C), lambda b, h, i: (b, h))
    ospec = pl.BlockSpec((tq, heads * DV_C), lambda b, h, i: (b * nq + i, h))
    return pl.pallas_call(
        functools.partial(_mla_attn_kernel, tq=tq, tk=tk, heads=heads),
        grid=(batch, H_C // heads, nq),
        in_specs=[qspec, kspec, vspec, ospec],
        out_specs=ospec,
        out_shape=jax.ShapeDtypeStruct((n, W_C), BF16),
        scratch_shapes=[pltpu.VMEM((heads, tq, HEAD_PAD), BF16), pltpu.VMEM((heads, tq, LANES), F32),
                        pltpu.VMEM((heads, tq, LANES), F32), pltpu.VMEM((heads, tq, DV_C), F32)],
        compiler_params=_cparams(("parallel", "parallel", "arbitrary")),
        name="mla_attention",
    )(q, k, v, cgate)


def _pool_kernel(x_ref, halo_ref, gate_ref, pw_ref, ps_ref, o_ref, buf_a, buf_b, *, tm, seq):
    i = pl.program_id(0)
    t0 = (i * tm) % seq
    x = x_ref[...]
    keep = (t0 > 0).astype(F32)
    buf_a[0:POOL_HALO, :] = halo_ref[...] * keep
    buf_a[POOL_HALO:POOL_HALO + tm, :] = x
    tot = tm + POOL_HALO
    src, dst = buf_a, buf_b
    levels = {}
    start = 0
    for w in (1, 2, 4, 8):
        start += 8
        cur = src[start:tot, :] + src[start - w:tot - w, :]
        dst[start:tot, :] = cur
        levels[2 * w] = cur[POOL_HALO - start:, :]
        src, dst = dst, src
    lane = lax.broadcasted_iota(jnp.int32, (tm, W_A), 1)
    grp = lane // POOL_GC
    win = jnp.where(grp == 0, levels[2], jnp.where(grp == 1, levels[4],
                    jnp.where(grp == 2, levels[8], levels[16])))
    wlane = jnp.where(grp == 0, 2, jnp.where(grp == 1, 4, jnp.where(grp == 2, 8, 16)))
    tpos = t0 + lax.broadcasted_iota(jnp.int32, (tm, W_A), 0)
    cnt = jnp.minimum(tpos + 1, wlane).astype(F32)
    pooled = win / cnt - x
    a = jnp.dot(pooled.astype(BF16), pw_ref[...], preferred_element_type=F32) * ps_ref[...]
    o_ref[...] = (a * _silu(gate_ref[...].astype(F32))).astype(BF16)


def _pool_mixer(a_in, a_gate, pool_w_bd, pool_scale, seq):
    n = a_in.shape[0]
    tm = min(ROW_TILE, seq)
    hb = tm // POOL_HALO
    row = lambda w: pl.BlockSpec((tm, w), lambda i: (i, 0))
    halo = pl.BlockSpec((POOL_HALO, W_A), lambda i: (jnp.maximum(i * hb - 1, 0), 0))
    full = lambda a: pl.BlockSpec(a.shape, lambda i: (0,) * a.ndim)
    return pl.pallas_call(
        functools.partial(_pool_kernel, tm=tm, seq=seq),
        grid=(n // tm,),
        in_specs=[row(W_A), halo, row(W_A), full(pool_w_bd), full(pool_scale)],
        out_specs=row(W_A),
        out_shape=jax.ShapeDtypeStruct((n, W_A), BF16),
        scratch_shapes=[pltpu.VMEM((tm + POOL_HALO, W_A), F32)] * 2,
        compiler_params=_cparams(("parallel",)),
        name="pool_mixer",
    )(a_in, a_in, a_gate, pool_w_bd, pool_scale)


def _out_kernel(h_ref, ma_ref, mb_ref, wo_ref, wg_ref, p_ref, wp_ref, o_ref, *, wa):
    m = (jnp.dot(ma_ref[...], wo_ref[0:wa, :], preferred_element_type=F32)
         + jnp.dot(mb_ref[...], wo_ref[wa:, :], preferred_element_type=F32))
    h1 = h_ref[...] + m
    gate = _sigmoid(jnp.dot(h1.astype(BF16), wg_ref[...], preferred_element_type=F32))
    pp = jnp.dot(p_ref[...].astype(BF16), wp_ref[...], preferred_element_type=F32)
    o_ref[...] = h1 + gate * pp


def _out_proj(h, mix_a, mix_b, w_out, w_gate, p, layer, w_proj):
    n = h.shape[0]
    tm = min(ROW_TILE, n)
    wa = mix_a.shape[1]
    row = lambda w: pl.BlockSpec((tm, w), lambda i: (i, 0))
    full = lambda a: pl.BlockSpec(a.shape, lambda i: (0,) * a.ndim)
    return pl.pallas_call(
        functools.partial(_out_kernel, wa=wa),
        grid=(n // tm,),
        in_specs=[row(D_MODEL), row(wa), row(mix_b.shape[1]), full(w_out), full(w_gate),
                  pl.BlockSpec((None, tm, PLE_DIM), lambda i: (layer, i, 0)), full(w_proj)],
        out_specs=row(D_MODEL),
        out_shape=jax.ShapeDtypeStruct((n, D_MODEL), F32),
        compiler_params=_cparams(("parallel",)),
        name="out_proj_ple",
    )(h, mix_a, mix_b, w_out, w_gate, p, w_proj)


O_CQ = 0
O_CKV = O_CQ + Q_LORA
O_KR = O_CKV + KV_LORA
O_CG = O_KR + LANES
O_QKV = O_CG + W_C
O_DG = O_QKV + CONV_CH
O_BA = O_DG + W_D
O_TOT = O_BA + LANES
HEAD_PAD = 2 * LANES


def _odd_in_kernel(h_ref, g_ref, w_ref, qan_ref, wuq_ref, kvan_ref, wukv_ref, qn_ref, kn_ref,
                   c_ref, s_ref, alog_ref, dtb_ref, cw_ref,
                   q_ref, k_ref, v_ref, cg_ref, qkv_ref, dg_ref, gb_ref, cbuf, *, tm, seq):
    @pl.when((pl.program_id(0) * tm) % seq == 0)
    def _():
        cbuf[0:CONV_PAD, :] = jnp.zeros((CONV_PAD, CONV_CH), F32)

    cw = cw_ref[...]
    x = h_ref[...]
    ms = jnp.mean(x * x, axis=-1, keepdims=True)
    yb = (x * lax.rsqrt(ms + EPS) * g_ref[...]).astype(BF16)
    c_all = c_ref[...]
    s_all = s_ref[...]
    qn = qn_ref[...]
    kn = kn_ref[...]
    scale = DQK_C ** -0.5 * LOG2E

    def project(rs):
        def seg(lo, hi):
            return jnp.dot(yb[rs, :], w_ref[:, lo:hi], preferred_element_type=F32)

        cq = seg(O_CQ, O_CKV)
        ckv = seg(O_CKV, O_KR)
        kr = seg(O_KR, O_CG)
        z_cg = seg(O_CG, O_QKV)
        z_qkv = seg(O_QKV, O_DG)
        z_dg = seg(O_DG, O_BA)
        ba = seg(O_BA, O_TOT)
        cqn = cq * lax.rsqrt(jnp.mean(cq * cq, axis=-1, keepdims=True) + EPS) * qan_ref[...]
        ckvn = ckv * lax.rsqrt(jnp.mean(ckv * ckv, axis=-1, keepdims=True) + EPS) * kvan_ref[...]
        qu = jnp.dot(cqn.astype(BF16), wuq_ref[...], preferred_element_type=F32)
        kvu = jnp.dot(ckvn.astype(BF16), wukv_ref[...], preferred_element_type=F32)
        return kr, z_cg, z_qkv, z_dg, ba, qu, kvu

    def epilogue(rs, zs):
        kr, z_cg, z_qkv, z_dg, ba, qu, kvu = zs
        c = c_all[rs, :]
        s = s_all[rs, :]
        cg_ref[rs, :] = z_cg.astype(BF16)
        dg_ref[rs, :] = z_dg.astype(BF16)
        lo = CONV_PAD + rs.start
        nrow = rs.stop - rs.start
        cbuf[lo:lo + nrow, :] = z_qkv
        y = None
        for j in range(CONV_K):
            off = lo - (CONV_K - 1) + j
            term = cbuf[off:off + nrow, :] * cw[j:j + 1, :]
            y = term if y is None else y + term
        qkv_ref[rs, :] = _silu(y).astype(BF16)
        v_ref[rs, :] = kvu[:, H_C * D_NOPE:].astype(BF16)
        beta = _sigmoid(ba)
        g = -jnp.exp(alog_ref[...]) * jax.nn.softplus(ba + dtb_ref[...])
        lane = lax.broadcasted_iota(jnp.int32, ba.shape, 1)
        gb_ref[rs, :] = jnp.where(lane < H_D, beta, g)
        for hh in range(H_C):
            nope = qu[:, hh * D_NOPE:(hh + 1) * D_NOPE]
            rope = qu[:, H_C * D_NOPE + hh * LANES:H_C * D_NOPE + (hh + 1) * LANES]
            ss = jnp.sum(nope * nope + rope * rope, axis=-1, keepdims=True)
            r = lax.rsqrt(ss * (1.0 / DQK_C) + EPS) * scale
            q_ref[rs, hh * HEAD_PAD:hh * HEAD_PAD + LANES] = (nope * r * qn[:, 0:LANES]).astype(BF16)
            q_ref[rs, hh * HEAD_PAD + LANES:(hh + 1) * HEAD_PAD] = _rope_block(
                rope * r * qn[:, LANES:2 * LANES], c, s, D_ROPE // 2).astype(BF16)
        kr_ss = jnp.sum(kr * kr, axis=-1, keepdims=True)
        kr_rot = _rope_block(kr * kn[:, LANES:2 * LANES], c, s, D_ROPE // 2)
        for hh in range(H_C):
            nope = kvu[:, hh * D_NOPE:(hh + 1) * D_NOPE]
            ss = jnp.sum(nope * nope, axis=-1, keepdims=True) + kr_ss
            r = lax.rsqrt(ss * (1.0 / DQK_C) + EPS)
            k_ref[rs, hh * HEAD_PAD:hh * HEAD_PAD + LANES] = (nope * r * kn[:, 0:LANES]).astype(BF16)
            k_ref[rs, hh * HEAD_PAD + LANES:(hh + 1) * HEAD_PAD] = (kr_rot * r).astype(BF16)

    part = x.shape[0] // IN_PROJ_PARTS
    parts = [slice(k * part, (k + 1) * part) for k in range(IN_PROJ_PARTS)]
    z_prev = project(parts[0])
    for k in range(1, IN_PROJ_PARTS):
        z_next = project(parts[k])
        epilogue(parts[k - 1], z_prev)
        z_prev = z_next
    epilogue(parts[-1], z_prev)
    cbuf[0:CONV_PAD, :] = cbuf[tm:tm + CONV_PAD, :]


def _odd_in(h, g, w_in, qan, wuq, kvan, wukv, qn, kn, c_tab, s_tab, alog, dtb, conv_w, seq):
    n = h.shape[0]
    tm = min(ROW_TILE, seq)
    row = lambda w: pl.BlockSpec((tm, w), lambda i: (i, 0))
    col = lambda w: pl.BlockSpec((w, tm), lambda i: (0, i))
    full = lambda a: pl.BlockSpec(a.shape, lambda i: (0,) * a.ndim)
    sds = jax.ShapeDtypeStruct
    return pl.pallas_call(
        functools.partial(_odd_in_kernel, tm=tm, seq=seq),
        grid=(n // tm,),
        in_specs=[row(D_MODEL), full(g), full(w_in), full(qan), full(wuq), full(kvan), full(wukv),
                  full(qn), full(kn), row(LANES), row(LANES), full(alog), full(dtb), full(conv_w)],
        out_specs=[row(H_C * HEAD_PAD), row(H_C * HEAD_PAD), row(W_C), row(W_C), row(CONV_CH), row(W_D),
                   row(LANES)],
        out_shape=[sds((n, H_C * HEAD_PAD), BF16), sds((n, H_C * HEAD_PAD), BF16), sds((n, W_C), BF16),
                   sds((n, W_C), BF16), sds((n, CONV_CH), BF16), sds((n, W_D), BF16), sds((n, LANES), F32)],
        scratch_shapes=[pltpu.VMEM((tm + CONV_PAD, CONV_CH), F32)],
        compiler_params=_cparams(("arbitrary",)),
        name="odd_in_proj",
    )(h, g, w_in, qan, wuq, kvan, wukv, qn, kn, c_tab, s_tab, alog, dtb, conv_w)


def _gdn_prep_kernel(y_ref, gbc_ref, lhs1_ref, lhs2_ref, u_ref, gl_ref, *, tm):
    ri = lax.broadcasted_iota(jnp.int32, (tm, tm), 0)
    ci = lax.broadcasted_iota(jnp.int32, (tm, tm), 1)
    low = jnp.where(((ri // CHUNK) == (ci // CHUNK)) & (ri >= ci), 1.0, 0.0).astype(F32)
    i2 = lax.broadcasted_iota(jnp.int32, (LANES, LANES), 0)
    j2 = lax.broadcasted_iota(jnp.int32, (LANES, LANES), 1)
    eye_l = jnp.where(i2 == j2, 1.0, 0.0).astype(BF16)
    gbc = gbc_ref[...]
    gcs = _split3_dot(low, gbc)
    gcs_t = _split3_dot_nt(eye_l, gcs)

    pdim = H_D * CHUNK
    r4 = lax.broadcasted_iota(jnp.int32, (pdim, pdim), 0)
    c4 = lax.broadcasted_iota(jnp.int32, (pdim, pdim), 1)
    same = (r4 // CHUNK) == (c4 // CHUNK)
    incl = same & (r4 >= c4)
    strict = same & (r4 > c4)
    eye = jnp.where(r4 == c4, 1.0, 0.0).astype(F32)
    zblk = jnp.zeros((CHUNK, DK_D), F32)

    def block_diag(blocks):
        return jnp.concatenate(
            [jnp.concatenate([blocks[h] if j == h else zblk for j in range(H_D)], axis=1) for h in range(H_D)],
            axis=0)

    chunks = list(range(tm // CHUNK))
    heads = list(range(H_D))

    def load_qkv(c):
        r0 = c * CHUNK
        qs, ks, vs = [], [], []
        for hh in heads:
            l0 = hh * DK_D
            qh = y_ref[r0:r0 + CHUNK, l0:l0 + DK_D].astype(F32)
            kh = y_ref[r0:r0 + CHUNK, H_D * DK_D + l0:H_D * DK_D + l0 + DK_D].astype(F32)
            vs.append(y_ref[r0:r0 + CHUNK, 2 * H_D * DK_D + l0:2 * H_D * DK_D + l0 + DV_D].astype(F32))
            qs.append(qh * lax.rsqrt(jnp.sum(qh * qh, axis=-1, keepdims=True) + EPS) * (DK_D ** -0.5))
            ks.append(kh * lax.rsqrt(jnp.sum(kh * kh, axis=-1, keepdims=True) + EPS))
        return qs, ks, vs

    def decay_terms(c):
        r0 = c * CHUNK
        beta = jnp.concatenate([gbc[r0:r0 + CHUNK, hh:hh + 1] for hh in heads], axis=0)
        gcol = jnp.concatenate([gcs[r0:r0 + CHUNK, H_D + hh:H_D + hh + 1] for hh in heads], axis=0)
        grow = jnp.concatenate([gcs_t[H_D + hh:H_D + hh + 1, r0:r0 + CHUNK] for hh in heads], axis=1)
        glasts = [gcs[r0 + CHUNK - 1:r0 + CHUNK, H_D + hh:H_D + hh + 1] for hh in heads]
        glast = jnp.concatenate([jnp.broadcast_to(gl, (CHUNK, 1)) for gl in glasts], axis=0)
        gamma = jnp.exp(jnp.where(incl, gcol - grow, -jnp.inf))
        return beta, gcol, glasts, glast, gamma

    qkv = [load_qkv(c) for c in chunks]
    q_st = [jnp.concatenate(x[0], axis=0) for x in qkv]
    k_st = [jnp.concatenate(x[1], axis=0) for x in qkv]
    v_st = [jnp.concatenate(x[2], axis=0) for x in qkv]
    k_bd = [block_diag(x[1]).astype(BF16) for x in qkv]
    q_bd = [block_diag(x[0]).astype(BF16) for x in qkv]
    qkkk = [_bdot_nt(jnp.concatenate([qb, kb], axis=0), kb) for qb, kb in zip(q_bd, k_bd)]
    dec = [decay_terms(c) for c in chunks]
    a = [jnp.where(strict, x[pdim:, :] * d[4] * d[0], 0.0) for x, d in zip(qkkk, dec)]
    t = [eye - x for x in a]
    pw = a
    for _ in range(5):
        pw = [_bdot(x, x) for x in pw]
        t = [x + _bdot(x, y) for x, y in zip(t, pw)]
    egc = [jnp.exp(d[1]) for d in dec]
    uw = [_bdot(tt, jnp.concatenate([v * d[0], k * (d[0] * e)], axis=1))
          for tt, v, k, d, e in zip(t, v_st, k_st, dec, egc)]
    kdt = [_bdot_nt(eye_l, k * jnp.exp(d[3] - d[1])).astype(BF16) for k, d in zip(k_st, dec)]
    for c in chunks:
        qkg = (qkkk[c][0:pdim, :] * dec[c][4]).astype(BF16)
        q_dec = (q_st[c] * egc[c]).astype(BF16)
        for hh in heads:
            idx = c * H_D + hh
            hrows = slice(hh * CHUNK, (hh + 1) * CHUNK)
            lhs1_ref[idx, 0:CHUNK, :] = uw[c][hrows, DV_D:].astype(BF16)
            lhs1_ref[idx, CHUNK:2 * CHUNK, :] = q_dec[hrows, :]
            u_ref[idx] = uw[c][hrows, 0:DV_D].astype(BF16)
            gl_ref[c, hh:hh + 1, :] = jnp.broadcast_to(jnp.exp(dec[c][2][hh]), (1, LANES))
        for pp in range(H_D // 2):
            idx2 = c * (H_D // 2) + pp
            lhs2_ref[idx2, 0:LANES, :] = qkg[pp * LANES:(pp + 1) * LANES, pp * LANES:(pp + 1) * LANES]
            lhs2_ref[idx2, LANES:2 * LANES, :] = kdt[c][:, pp * LANES:(pp + 1) * LANES]


def _gdn_prep(y, gb, seq):
    n = y.shape[0]
    tm = min(GDN_PREP_TILE, seq)
    nch = tm // CHUNK
    row = lambda w: pl.BlockSpec((tm, w), lambda i: (i, 0))
    ch3 = lambda m, r, w: pl.BlockSpec((nch * m, r, w), lambda i: (i, 0, 0))
    sds = jax.ShapeDtypeStruct
    nc = n // CHUNK
    return pl.pallas_call(
        functools.partial(_gdn_prep_kernel, tm=tm),
        grid=(n // tm,),
        in_specs=[row(CONV_CH), row(LANES)],
        out_specs=[ch3(H_D, 2 * CHUNK, DK_D), ch3(H_D // 2, 2 * LANES, LANES), ch3(H_D, CHUNK, DV_D),
                   pl.BlockSpec((nch, H_D, LANES), lambda i: (i, 0, 0))],
        out_shape=[sds((nc * H_D, 2 * CHUNK, DK_D), BF16), sds((nc * H_D // 2, 2 * LANES, LANES), BF16),
                   sds((nc * H_D, CHUNK, DV_D), BF16), sds((nc, H_D, LANES), F32)],
        compiler_params=_cparams(("parallel",)),
        name="gdn_chunk_prep",
    )(y, gb)


def _gdn_scan_kernel(lhs1_ref, lhs2_ref, u_ref, gl_ref, dg_ref, on_ref, o_ref, s_scr, *, nb, tb):
    @pl.when(pl.program_id(0) == 0)
    def _():
        s_scr[...] = jnp.zeros(s_scr.shape, F32)

    on = on_ref[...]
    zb = jnp.zeros((2 * CHUNK, DK_D), BF16)
    left = lax.broadcasted_iota(jnp.int32, (DK_D, LANES), 1) < CHUNK

    def finish(o, b, rows, hh):
        ms = jnp.mean(o * o, axis=-1, keepdims=True)
        o = o * lax.rsqrt(ms + EPS) * on
        gate = dg_ref[b, rows, hh * DV_D:(hh + 1) * DV_D].astype(F32)
        o_ref[b, rows, hh * DV_D:(hh + 1) * DV_D] = (o * _silu(gate)).astype(BF16)

    def chunk_body(c, carry):
        rows = pl.ds(pl.multiple_of(c * CHUNK, CHUNK), CHUNK)
        chains = [(b, 2 * pp) for b in range(nb) for pp in range(H_D // 2)]
        st = [(s_scr[b * H_D + h0], s_scr[b * H_D + h0 + 1]) for b, h0 in chains]
        r1 = [jnp.dot(jnp.concatenate([jnp.concatenate([lhs1_ref[b, c * H_D + h0], zb], axis=1),
                                       jnp.concatenate([zb, lhs1_ref[b, c * H_D + h0 + 1]], axis=1)], axis=0),
                      jnp.concatenate([s0, s1], axis=0).astype(BF16), preferred_element_type=F32)
              for (b, h0), (s0, s1) in zip(chains, st)]
        v2 = [jnp.concatenate([u_ref[b, c * H_D + h0].astype(F32) - x[0:CHUNK, :],
                               u_ref[b, c * H_D + h0 + 1].astype(F32) - x[2 * CHUNK:3 * CHUNK, :]],
                              axis=0).astype(BF16) for (b, h0), x in zip(chains, r1)]
        r2 = []
        for (b, h0), v in zip(chains, v2):
            blk = lhs2_ref[b, c * (H_D // 2) + h0 // 2]
            kd = blk[LANES:2 * LANES, :]
            zk = jnp.zeros_like(kd)
            l2 = jnp.concatenate([blk[0:LANES, :], jnp.where(left, kd, zk), jnp.where(left, zk, kd)], axis=0)
            r2.append(jnp.dot(l2, v, preferred_element_type=F32))
        for (b, h0), (s0, s1), x, y in zip(chains, st, r1, r2):
            s_scr[b * H_D + h0] = s0 * gl_ref[b, c, h0:h0 + 1, :] + y[LANES:LANES + DK_D, :]
            s_scr[b * H_D + h0 + 1] = s1 * gl_ref[b, c, h0 + 1:h0 + 2, :] + y[LANES + DK_D:LANES + 2 * DK_D, :]
            finish(x[CHUNK:2 * CHUNK, :] + y[0:CHUNK, :], b, rows, h0)
            finish(x[3 * CHUNK:4 * CHUNK, :] + y[CHUNK:2 * CHUNK, :], b, rows, h0 + 1)
        return carry

    lax.fori_loop(0, tb // CHUNK, chunk_body, 0)


def _gdn_scan(lhs1, lhs2, u, gl, d_gate, o_norm, batch, seq):
    tb = min(GDN_SCAN_TILE, seq)
    nch = tb // CHUNK
    ncb = seq // CHUNK
    lhs1 = lhs1.reshape(batch, ncb * H_D, 2 * CHUNK, DK_D)
    lhs2 = lhs2.reshape(batch, ncb * H_D // 2, 2 * LANES, LANES)
    u = u.reshape(batch, ncb * H_D, CHUNK, DV_D)
    gl = gl.reshape(batch, ncb, H_D, LANES)
    d_gate = d_gate.reshape(batch, seq, W_D)
    ch4 = lambda r, w: pl.BlockSpec((batch, nch * H_D, r, w), lambda i: (0, i, 0, 0))
    tok = pl.BlockSpec((batch, tb, W_D), lambda i: (0, i, 0))
    out = pl.pallas_call(
        functools.partial(_gdn_scan_kernel, nb=batch, tb=tb),
        grid=(seq // tb,),
        in_specs=[ch4(2 * CHUNK, DK_D),
                  pl.BlockSpec((batch, nch * H_D // 2, 2 * LANES, LANES), lambda i: (0, i, 0, 0)),
                  ch4(CHUNK, DV_D),
                  pl.BlockSpec((batch, nch, H_D, LANES), lambda i: (0, i, 0, 0)), tok,
                  pl.BlockSpec(o_norm.shape, lambda i: (0, 0))],
        out_specs=tok,
        out_shape=jax.ShapeDtypeStruct((batch, seq, W_D), BF16),
        scratch_shapes=[pltpu.VMEM((batch * H_D, DK_D, DV_D), F32)],
        compiler_params=_cparams(("arbitrary",)),
        name="gdn_state_scan",
    )(lhs1, lhs2, u, gl, d_gate, o_norm)
    return out.reshape(batch * seq, W_D)


def _rope_patterns(rot_dim, theta, period):
    half = rot_dim // 2
    inv = jnp.power(jnp.float32(theta), -jnp.arange(half, dtype=F32) * (2.0 / rot_dim))
    lane = np.arange(LANES)
    in_rot = (lane % period) < rot_dim
    idx = jnp.asarray(lane % half)
    inv_l = jnp.where(jnp.asarray(in_rot), inv[idx], 0.0).astype(F32)[None, :]
    sgn = np.where(in_rot, np.where((lane % period) < half, -1.0, 1.0), 0.0).astype(np.float32)[None, :]
    return inv_l, jnp.asarray(sgn)


def _pad_cols(w, width):
    return jnp.pad(w, ((0, 0), (0, width - w.shape[1])))


def _odd_w_in_layout(w):
    cq, ckv, kr, cg, qkv, db, da, dg = jnp.split(
        w, np.cumsum((Q_LORA, KV_LORA, D_ROPE, W_C, CONV_CH, H_D, H_D, W_D))[:-1].tolist(), axis=1)
    return jnp.concatenate([cq, ckv, _pad_cols(kr, LANES), cg, qkv, dg,
                            _pad_cols(jnp.concatenate([db, da], axis=1), LANES)], axis=1)


def _wuq_layout(w):
    w = w.reshape(Q_LORA, H_C, DQK_C)
    nope = w[:, :, :D_NOPE].reshape(Q_LORA, H_C * D_NOPE)
    rope = jnp.pad(w[:, :, D_NOPE:], ((0, 0), (0, 0), (0, LANES - D_ROPE))).reshape(Q_LORA, H_C * LANES)
    return jnp.concatenate([nope, rope], axis=1)


def _wukv_layout(w):
    w = w.reshape(KV_LORA, H_C, D_NOPE + DV_C)
    return jnp.concatenate([w[:, :, :D_NOPE].reshape(KV_LORA, H_C * D_NOPE),
                            w[:, :, D_NOPE:].reshape(KV_LORA, H_C * DV_C)], axis=1)


def _head_gain_layout(g):
    return jnp.pad(g, (0, HEAD_PAD - DQK_C))[None, :].astype(F32)


def kernel(x, p, positions, norm_g, ple_w_gate, ple_w_proj, ev_w_in, ev_pool_w, ev_pool_scale, ev_q_norm, ev_k_norm, ev_lambda, ev_subln, ev_w_out, od_w_in, od_q_a_norm, od_w_uq, od_kv_a_norm, od_w_ukv, od_q_norm, od_k_norm, od_conv_w, od_a_log, od_dt_bias, od_o_norm, od_w_out):
    batch, seq, _ = x.shape
    depth = p.shape[0]
    n = batch * seq
    h = x.reshape(n, D_MODEL)
    pos_b = jnp.broadcast_to(positions.astype(F32).reshape(n, 1), (n, LANES))

    inv_e, sgn_e = _rope_patterns(B_ROT, ROPE_THETA, DK_B)
    ce, se = _rope_tables(pos_b, inv_e, sgn_e)
    inv_o, sgn_o = _rope_patterns(D_ROPE, MLA_THETA, LANES)
    co, so = _rope_tables(pos_b, inv_o, sgn_o)


    for i in range(depth):
        j = i // 2
        g = norm_g[i][None, :]
        if i % 2 == 0:
            w_in = ev_w_in[j].astype(BF16)
            qg = (jnp.tile(ev_q_norm[j], 2 * H_B) * (DK_B ** -0.5 * LOG2E))[None, :]
            kg = jnp.tile(ev_k_norm[j], 2 * H_B)[None, :]
            a_in, a_gate, q, k, vt, b_gate = _even_in(h, g, w_in, qg, kg, ce, se)
            lam_init = 0.8 - 0.6 * math.exp(-0.3 * i)
            ob = _diff_attention(q, k, vt, b_gate, ev_lambda[j], ev_subln[j][None, :], batch, seq, lam_init)
            pw = jax.scipy.linalg.block_diag(*[ev_pool_w[j][gi] for gi in range(len(POOL_WINDOWS))]).astype(BF16)
            oa = _pool_mixer(a_in, a_gate, pw, ev_pool_scale[j][None, :], seq)
            w_out = ev_w_out[j].astype(BF16)
        else:
            w_in = _odd_w_in_layout(od_w_in[j]).astype(BF16)
            alog = jnp.zeros((LANES,), F32).at[H_D:2 * H_D].set(od_a_log[j])[None, :]
            dtb = jnp.zeros((LANES,), F32).at[H_D:2 * H_D].set(od_dt_bias[j])[None, :]
            q, k, vt, c_gate, qkv, d_gate, gb = _odd_in(
                h, g, w_in, od_q_a_norm[j][None, :], _wuq_layout(od_w_uq[j]).astype(BF16),
                od_kv_a_norm[j][None, :], _wukv_layout(od_w_ukv[j]).astype(BF16),
                _head_gain_layout(od_q_norm[j]), _head_gain_layout(od_k_norm[j]), co, so, alog, dtb,
                od_conv_w[j], seq)
            oa = _mla_attention(q, k, vt, c_gate, batch, seq)
            lhs1, lhs2, u, gl = _gdn_prep(qkv, gb, seq)
            ob = _gdn_scan(lhs1, lhs2, u, gl, d_gate, od_o_norm[j][None, :], batch, seq)
            w_out = od_w_out[j].astype(BF16)
        h = _out_proj(h, oa, ob, w_out, ple_w_gate[i].astype(BF16), p.reshape(depth, n, PLE_DIM), i,
                      ple_w_proj[i].astype(BF16))
    return h.reshape(batch, seq, D_MODEL)
```

```python
import functools
import math

import numpy as np
import jax
import jax.numpy as jnp
from jax import lax
from jax.experimental import pallas as pl
from jax.experimental.pallas import tpu as pltpu

F32 = jnp.float32
BF16 = jnp.bfloat16

D_MODEL = 1024
PLE_DIM = 256
EPS = 1e-6
W_A = 256
POOL_WINDOWS = (2, 4, 8, 16)
POOL_GC = 64
W_B = 768
DV_B = 128
H_B = 6
DK_B = 64
B_ROT = 16
ROPE_THETA = 500000.0
W_C = 512
DV_C = 128
H_C = 4
D_NOPE = 128
D_ROPE = 64
DQK_C = 192
Q_LORA = 256
KV_LORA = 128
MLA_THETA = 10000.0
W_D = 512
DK_D = 128
DV_D = 128
H_D = 4
CONV_K = 4
CHUNK = 64
CONV_CH = H_D * (2 * DK_D + DV_D)

LANES = 128
VMEM_LIMIT = 48 * 1024 * 1024
NEG = -1e30
LOG2E = math.log2(math.e)

ROW_TILE = 512
IN_PROJ_PARTS = 2
DIFF_TK = 1024
MLA_TK = 1024
DIAG_TK = 512
POOL_HALO = 32
CONV_PAD = 8
GDN_PREP_TILE = 256
GDN_SCAN_TILE = 256
DIFF_HEADS_PER_STEP = 2
MLA_HEADS_PER_STEP = 2
DIFF_TQ = 512
MLA_TQ = 1024


def _cparams(sem):
    return pltpu.CompilerParams(dimension_semantics=sem, vmem_limit_bytes=VMEM_LIMIT)


def _silu(x):
    return x * (1.0 / (1.0 + jnp.exp(-x)))


def _sigmoid(x):
    return 1.0 / (1.0 + jnp.exp(-x))


def _bdot(a, b):
    return jnp.dot(a.astype(BF16), b.astype(BF16), preferred_element_type=F32)


def _bdot_nt(a, b):
    return lax.dot_general(a.astype(BF16), b.astype(BF16), (((1,), (1,)), ((), ())),
                           preferred_element_type=F32)


def _split3_dot(a_exact, b):
    b0 = b.astype(BF16)
    r1 = b - b0.astype(F32)
    b1 = r1.astype(BF16)
    b2 = (r1 - b1.astype(F32)).astype(BF16)
    a = a_exact.astype(BF16)
    return (jnp.dot(a, b0, preferred_element_type=F32) + jnp.dot(a, b1, preferred_element_type=F32)
            + jnp.dot(a, b2, preferred_element_type=F32))


def _split3_dot_nt(a_exact, b):
    b0 = b.astype(BF16)
    r1 = b - b0.astype(F32)
    b1 = r1.astype(BF16)
    b2 = (r1 - b1.astype(F32)).astype(BF16)
    return _bdot_nt(a_exact, b0) + _bdot_nt(a_exact, b1) + _bdot_nt(a_exact, b2)


def _rope_table_kernel(pos_ref, inv_ref, sgn_ref, c_ref, s_ref):
    ang = pos_ref[...] * inv_ref[...]
    c_ref[...] = jnp.cos(ang)
    s_ref[...] = jnp.sin(ang) * sgn_ref[...]


def _rope_tables(pos_b, inv, sgn):
    n = pos_b.shape[0]
    tm = min(1024, n)
    row = pl.BlockSpec((tm, LANES), lambda i: (i, 0))
    par = pl.BlockSpec((1, LANES), lambda i: (0, 0))
    return pl.pallas_call(
        _rope_table_kernel,
        grid=(n // tm,),
        in_specs=[row, par, par],
        out_specs=[row, row],
        out_shape=[jax.ShapeDtypeStruct((n, LANES), F32)] * 2,
        compiler_params=_cparams(("parallel",)),
        name="rope_tables",
    )(pos_b, inv, sgn)


def _rope_block(x, c, s, half):
    fwd = pltpu.roll(x, LANES - half, 1)
    bwd = pltpu.roll(x, half, 1)
    lane = lax.broadcasted_iota(jnp.int32, x.shape, 1)
    rot = jnp.where((lane % (2 * half)) < half, fwd, bwd)
    return x * c + rot * s


def _even_in_kernel(h_ref, g_ref, w_ref, qg_ref, kg_ref, c_ref, s_ref,
                    ain_ref, agate_ref, q_ref, k_ref, v_ref, bgate_ref):
    x = h_ref[...]
    ms = jnp.mean(x * x, axis=-1, keepdims=True)
    yb = (x * lax.rsqrt(ms + EPS) * g_ref[...]).astype(BF16)

    c_all = c_ref[...]
    s_all = s_ref[...]
    o = 2 * W_A
    bounds = [(0, W_A), (W_A, 2 * W_A), (o, o + W_B), (o + W_B, o + 2 * W_B), (o + 2 * W_B, o + 3 * W_B),
              (o + 3 * W_B, o + 4 * W_B)]

    def project(rs):
        return [jnp.dot(yb[rs, :], w_ref[:, lo:hi], preferred_element_type=F32) for lo, hi in bounds]

    def qk_prep(rs, z, gain_ref, out_ref, c, s):
        gain = gain_ref[...]
        first = lax.broadcasted_iota(jnp.int32, (z.shape[0], LANES), 1) < DK_B
        for j in range(W_B // LANES):
            blk = z[:, j * LANES:(j + 1) * LANES]
            sq = blk * blk
            ss0 = jnp.sum(jnp.where(first, sq, 0.0), axis=-1, keepdims=True)
            ss1 = jnp.sum(jnp.where(first, 0.0, sq), axis=-1, keepdims=True)
            r = jnp.where(first, lax.rsqrt(ss0 * (1.0 / DK_B) + EPS), lax.rsqrt(ss1 * (1.0 / DK_B) + EPS))
            zn = blk * r * gain[:, j * LANES:(j + 1) * LANES]
            out_ref[rs, j * LANES:(j + 1) * LANES] = _rope_block(zn, c, s, B_ROT // 2).astype(BF16)

    def epilogue(rs, zs):
        z_ain, z_agate, z_q, z_k, z_v, z_bg = zs
        ain_ref[rs, :] = z_ain
        agate_ref[rs, :] = z_agate.astype(BF16)
        v_ref[rs, :] = z_v.astype(BF16)
        bgate_ref[rs, :] = z_bg.astype(BF16)
        qk_prep(rs, z_q, qg_ref, q_ref, c_all[rs, :], s_all[rs, :])
        qk_prep(rs, z_k, kg_ref, k_ref, c_all[rs, :], s_all[rs, :])

    part = x.shape[0] // IN_PROJ_PARTS
    parts = [slice(k * part, (k + 1) * part) for k in range(IN_PROJ_PARTS)]
    z_prev = project(parts[0])
    for k in range(1, IN_PROJ_PARTS):
        z_next = project(parts[k])
        epilogue(parts[k - 1], z_prev)
        z_prev = z_next
    epilogue(parts[-1], z_prev)


def _even_in(h, g, w_in, qg, kg, c_tab, s_tab):
    n = h.shape[0]
    tm = min(ROW_TILE, n)
    row = lambda w: pl.BlockSpec((tm, w), lambda i: (i, 0))
    full = lambda a: pl.BlockSpec(a.shape, lambda i: (0,) * a.ndim)
    sds = jax.ShapeDtypeStruct
    return pl.pallas_call(
        _even_in_kernel,
        grid=(n // tm,),
        in_specs=[row(D_MODEL), full(g), full(w_in), full(qg), full(kg), row(LANES), row(LANES)],
        out_specs=[row(W_A), row(W_A), row(W_B), row(W_B), row(W_B), row(W_B)],
        out_shape=[sds((n, W_A), F32), sds((n, W_A), BF16), sds((n, W_B), BF16), sds((n, W_B), BF16),
                   sds((n, W_B), BF16), sds((n, W_B), BF16)],
        compiler_params=_cparams(("parallel",)),
        name="even_in_proj",
    )(h, g, w_in, qg, kg, c_tab, s_tab)


def _flash_loop(qs_ref, k_ref, v_ref, m_scr, l_scr, acc_scr, qi, tq, tk, heads, dqk, dv):
    q0 = qi * tq
    rows = qs_ref.shape[1]
    tkt = min(tq, tk, DIAG_TK)
    n_main = q0 // tk
    base = n_main * tk
    n_tail_full = (q0 - base) // tkt
    n_tail_diag = tq // tkt
    assert n_tail_diag == 1 or rows == tq
    m_scr[...] = jnp.full(m_scr.shape, NEG, F32)
    l_scr[...] = jnp.zeros(l_scr.shape, F32)
    acc_scr[...] = jnp.zeros(acc_scr.shape, F32)

    def step(ks, tk, masked, row_lo=0):
        hs = range(heads)
        rs = slice(row_lo, rows)
        s = [lax.dot_general(qs_ref[g, rs, :], k_ref[pl.ds(ks, tk), g * dqk:(g + 1) * dqk],
                             (((1,), (1,)), ((), ())), preferred_element_type=F32) for g in hs]
        if masked:
            row = q0 + (row_lo + lax.broadcasted_iota(jnp.int32, s[0].shape, 0)) % tq
            col = ks + lax.broadcasted_iota(jnp.int32, s[0].shape, 1)
            s = [jnp.where(col <= row, x, NEG) for x in s]
        m_prev = [m_scr[g, rs, :] for g in hs]
        m_new = [jnp.maximum(mp, jnp.max(x, axis=1, keepdims=True)) for mp, x in zip(m_prev, s)]
        alpha = [jnp.exp2(mp - mn) for mp, mn in zip(m_prev, m_new)]
        p = [jnp.exp2(x - jnp.tile(mn, (1, tk // LANES))) for x, mn in zip(s, m_new)]
        ones = jnp.ones((tk, LANES), BF16)
        pv = [jnp.dot(p[g].astype(BF16),
                      jnp.concatenate([v_ref[pl.ds(ks, tk), g * dv:(g + 1) * dv], ones], axis=1),
                      preferred_element_type=F32) for g in hs]
        for g in hs:
            m_scr[g, rs, :] = m_new[g]
            l_scr[g, rs, :] = alpha[g] * l_scr[g, rs, :] + pv[g][:, dv:]
            acc_scr[g, rs, :] = alpha[g] * acc_scr[g, rs, :] + pv[g][:, 0:dv]

    def main_body(j, carry):
        step(pl.multiple_of(j * tk, tk), tk, False)
        return carry

    def tail_body(j, carry):
        step(pl.multiple_of(base + j * tkt, tkt), tkt, False)
        return carry

    lax.fori_loop(0, n_main, main_body, 0)
    if tq % tk:
        lax.fori_loop(0, n_tail_full, tail_body, 0)
    for j in range(n_tail_diag):
        step(pl.multiple_of(base + (n_tail_full + j) * tkt, tkt), tkt, True, row_lo=j * tkt)


def _diff_attn_kernel(q_ref, k_ref, v_ref, bg_ref, lam_ref, sub_ref, o_ref,
                      qs_scr, m_scr, l_scr, acc_scr, *, tq, tk, heads, lam_init):
    qi = pl.program_id(2)
    for g in range(heads):
        q = q_ref[:, g * DV_B:(g + 1) * DV_B]
        lane = lax.broadcasted_iota(jnp.int32, q.shape, 1)
        zero = jnp.zeros_like(q)
        qs_scr[g, 0:tq, :] = jnp.where(lane < DK_B, q, zero)
        qs_scr[g, tq:2 * tq, :] = jnp.where(lane >= DK_B, q, zero)
    _flash_loop(qs_scr, k_ref, v_ref, m_scr, l_scr, acc_scr, qi, tq, tk, heads, DV_B, DV_B)
    lv = lam_ref[...]
    lam = (jnp.exp(jnp.sum(lv[0:1] * lv[1:2], axis=1, keepdims=True))
           - jnp.exp(jnp.sum(lv[2:3] * lv[3:4], axis=1, keepdims=True)) + lam_init)
    for g in range(heads):
        o1 = acc_scr[g, 0:tq, :] / l_scr[g, 0:tq, :]
        o2 = acc_scr[g, tq:2 * tq, :] / l_scr[g, tq:2 * tq, :]
        o = o1 - lam * o2
        ms = jnp.mean(o * o, axis=-1, keepdims=True)
        o = o * lax.rsqrt(ms + EPS) * sub_ref[...] * (1.0 - lam_init)
        gate = bg_ref[:, g * DV_B:(g + 1) * DV_B].astype(F32)
        o_ref[:, g * DV_B:(g + 1) * DV_B] = (o * _silu(gate)).astype(BF16)


def _diff_attention(q, k, v, bgate, lam_vec, subln, batch, seq, lam_init):
    n = q.shape[0]
    tq = min(DIFF_TQ, seq)
    tk = min(DIFF_TK, seq)
    nq = seq // tq
    heads = DIFF_HEADS_PER_STEP
    wid = heads * DV_B
    qspec = pl.BlockSpec((tq, wid), lambda b, h, i: (b * nq + i, h))
    kvspec = pl.BlockSpec((seq, wid), lambda b, h, i: (b, h))
    full = lambda a: pl.BlockSpec(a.shape, lambda b, h, i: (0,) * a.ndim)
    return pl.pallas_call(
        functools.partial(_diff_attn_kernel, tq=tq, tk=tk, heads=heads, lam_init=lam_init),
        grid=(batch, H_B // heads, nq),
        in_specs=[qspec, kvspec, kvspec, qspec, full(lam_vec), full(subln)],
        out_specs=qspec,
        out_shape=jax.ShapeDtypeStruct((n, W_B), BF16),
        scratch_shapes=[pltpu.VMEM((heads, 2 * tq, DV_B), BF16), pltpu.VMEM((heads, 2 * tq, LANES), F32),
                        pltpu.VMEM((heads, 2 * tq, LANES), F32), pltpu.VMEM((heads, 2 * tq, DV_B), F32)],
        compiler_params=_cparams(("parallel", "parallel", "arbitrary")),
        name="diff_attention",
    )(q, k, v, bgate, lam_vec, subln)


def _mla_attn_kernel(q_ref, k_ref, v_ref, cg_ref, o_ref, qs_scr, m_scr, l_scr, acc_scr, *, tq, tk, heads):
    qi = pl.program_id(2)
    for g in range(heads):
        qs_scr[g] = q_ref[:, g * HEAD_PAD:(g + 1) * HEAD_PAD]
    _flash_loop(qs_scr, k_ref, v_ref, m_scr, l_scr, acc_scr, qi, tq, tk, heads, HEAD_PAD, DV_C)
    for g in range(heads):
        o = acc_scr[g] / l_scr[g]
        gate = cg_ref[:, g * DV_C:(g + 1) * DV_C].astype(F32)
        o_ref[:, g * DV_C:(g + 1) * DV_C] = (o * _silu(gate)).astype(BF16)


def _mla_attention(q, k, v, cgate, batch, seq):
    n = q.shape[0]
    tq = min(MLA_TQ, seq)
    tk = min(MLA_TK, seq)
    nq = seq // tq
    heads = MLA_HEADS_PER_STEP
    qspec = pl.BlockSpec((tq, heads * HEAD_PAD), lambda b, h, i: (b * nq + i, h))
    kspec = pl.BlockSpec((seq, heads * HEAD_PAD), lambda b, h, i: (b, h))
    vspec = pl.BlockSpec((seq, heads * DV_C), lambda b, h, i: (b, h))
    ospec = pl.BlockSpec((tq, heads * DV_C), lambda b, h, i: (b * nq + i, h))
    return pl.pallas_call(
        functools.partial(_mla_attn_kernel, tq=tq, tk=tk, heads=heads),
        grid=(batch, H_C // heads, nq),
        in_specs=[qspec, kspec, vspec, ospec],
        out_specs=ospec,
        out_shape=jax.ShapeDtypeStruct((n, W_C), BF16),
        scratch_shapes=[pltpu.VMEM((heads, tq, HEAD_PAD), BF16), pltpu.VMEM((heads, tq, LANES), F32),
                        pltpu.VMEM((heads, tq, LANES), F32), pltpu.VMEM((heads, tq, DV_C), F32)],
        compiler_params=_cparams(("parallel", "parallel", "arbitrary")),
        name="mla_attention",
    )(q, k, v, cgate)


def _pool_kernel(x_ref, halo_ref, gate_ref, pw_ref, ps_ref, o_ref, buf_a, buf_b, *, tm, seq):
    i = pl.program_id(0)
    t0 = (i * tm) % seq
    x = x_ref[...]
    keep = (t0 > 0).astype(F32)
    buf_a[0:POOL_HALO, :] = halo_ref[...] * keep
    buf_a[POOL_HALO:POOL_HALO + tm, :] = x
    tot = tm + POOL_HALO
    src, dst = buf_a, buf_b
    levels = {}
    start = 0
    for w in (1, 2, 4, 8):
        start += 8
        cur = src[start:tot, :] + src[start - w:tot - w, :]
        dst[start:tot, :] = cur
        levels[2 * w] = cur[POOL_HALO - start:, :]
        src, dst = dst, src
    lane = lax.broadcasted_iota(jnp.int32, (tm, W_A), 1)
    grp = lane // POOL_GC
    win = jnp.where(grp == 0, levels[2], jnp.where(grp == 1, levels[4],
                    jnp.where(grp == 2, levels[8], levels[16])))
    wlane = jnp.where(grp == 0, 2, jnp.where(grp == 1, 4, jnp.where(grp == 2, 8, 16)))
    tpos = t0 + lax.broadcasted_iota(jnp.int32, (tm, W_A), 0)
    cnt = jnp.minimum(tpos + 1, wlane).astype(F32)
    pooled = win / cnt - x
    a = jnp.dot(pooled.astype(BF16), pw_ref[...], preferred_element_type=F32) * ps_ref[...]
    o_ref[...] = (a * _silu(gate_ref[...].astype(F32))).astype(BF16)


def _pool_mixer(a_in, a_gate, pool_w_bd, pool_scale, seq):
    n = a_in.shape[0]
    tm = min(ROW_TILE, seq)
    hb = tm // POOL_HALO
    row = lambda w: pl.BlockSpec((tm, w), lambda i: (i, 0))
    halo = pl.BlockSpec((POOL_HALO, W_A), lambda i: (jnp.maximum(i * hb - 1, 0), 0))
    full = lambda a: pl.BlockSpec(a.shape, lambda i: (0,) * a.ndim)
    return pl.pallas_call(
        functools.partial(_pool_kernel, tm=tm, seq=seq),
        grid=(n // tm,),
        in_specs=[row(W_A), halo, row(W_A), full(pool_w_bd), full(pool_scale)],
        out_specs=row(W_A),
        out_shape=jax.ShapeDtypeStruct((n, W_A), BF16),
        scratch_shapes=[pltpu.VMEM((tm + POOL_HALO, W_A), F32)] * 2,
        compiler_params=_cparams(("parallel",)),
        name="pool_mixer",
    )(a_in, a_in, a_gate, pool_w_bd, pool_scale)


def _out_kernel(h_ref, ma_ref, mb_ref, wo_ref, wg_ref, p_ref, wp_ref, o_ref, *, wa):
    m = (jnp.dot(ma_ref[...], wo_ref[0:wa, :], preferred_element_type=F32)
         + jnp.dot(mb_ref[...], wo_ref[wa:, :], preferred_element_type=F32))
    h1 = h_ref[...] + m
    gate = _sigmoid(jnp.dot(h1.astype(BF16), wg_ref[...], preferred_element_type=F32))
    pp = jnp.dot(p_ref[...].astype(BF16), wp_ref[...], preferred_element_type=F32)
    o_ref[...] = h1 + gate * pp


def _out_proj(h, mix_a, mix_b, w_out, w_gate, p, layer, w_proj):
    n = h.shape[0]
    tm = min(ROW_TILE, n)
    wa = mix_a.shape[1]
    row = lambda w: pl.BlockSpec((tm, w), lambda i: (i, 0))
    full = lambda a: pl.BlockSpec(a.shape, lambda i: (0,) * a.ndim)
    return pl.pallas_call(
        functools.partial(_out_kernel, wa=wa),
        grid=(n // tm,),
        in_specs=[row(D_MODEL), row(wa), row(mix_b.shape[1]), full(w_out), full(w_gate),
                  pl.BlockSpec((None, tm, PLE_DIM), lambda i: (layer, i, 0)), full(w_proj)],
        out_specs=row(D_MODEL),
        out_shape=jax.ShapeDtypeStruct((n, D_MODEL), F32),
        compiler_params=_cparams(("parallel",)),
        name="out_proj_ple",
    )(h, mix_a, mix_b, w_out, w_gate, p, w_proj)


O_CQ = 0
O_CKV = O_CQ + Q_LORA
O_KR = O_CKV + KV_LORA
O_CG = O_KR + LANES
O_QKV = O_CG + W_C
O_DG = O_QKV + CONV_CH
O_BA = O_DG + W_D
O_TOT = O_BA + LANES
HEAD_PAD = 2 * LANES


def _odd_in_kernel(h_ref, g_ref, w_ref, qan_ref, wuq_ref, kvan_ref, wukv_ref, qn_ref, kn_ref,
                   c_ref, s_ref, alog_ref, dtb_ref, cw_ref,
                   q_ref, k_ref, v_ref, cg_ref, qkv_ref, dg_ref, gb_ref, cbuf, *, tm, seq):
    @pl.when((pl.program_id(0) * tm) % seq == 0)
    def _():
        cbuf[0:CONV_PAD, :] = jnp.zeros((CONV_PAD, CONV_CH), F32)

    cw = cw_ref[...]
    x = h_ref[...]
    ms = jnp.mean(x * x, axis=-1, keepdims=True)
    yb = (x * lax.rsqrt(ms + EPS) * g_ref[...]).astype(BF16)
    c_all = c_ref[...]
    s_all = s_ref[...]
    qn = qn_ref[...]
    kn = kn_ref[...]
    scale = DQK_C ** -0.5 * LOG2E

    def project(rs):
        def seg(lo, hi):
            return jnp.dot(yb[rs, :], w_ref[:, lo:hi], preferred_element_type=F32)

        cq = seg(O_CQ, O_CKV)
        ckv = seg(O_CKV, O_KR)
        kr = seg(O_KR, O_CG)
        z_cg = seg(O_CG, O_QKV)
        z_qkv = seg(O_QKV, O_DG)
        z_dg = seg(O_DG, O_BA)
        ba = seg(O_BA, O_TOT)
        cqn = cq * lax.rsqrt(jnp.mean(cq * cq, axis=-1, keepdims=True) + EPS) * qan_ref[...]
        ckvn = ckv * lax.rsqrt(jnp.mean(ckv * ckv, axis=-1, keepdims=True) + EPS) * kvan_ref[...]
        qu = jnp.dot(cqn.astype(BF16), wuq_ref[...], preferred_element_type=F32)
        kvu = jnp.dot(ckvn.astype(BF16), wukv_ref[...], preferred_element_type=F32)
        return kr, z_cg, z_qkv, z_dg, ba, qu, kvu

    def epilogue(rs, zs):
        kr, z_cg, z_qkv, z_dg, ba, qu, kvu = zs
        c = c_all[rs, :]
        s = s_all[rs, :]
        cg_ref[rs, :] = z_cg.astype(BF16)
        dg_ref[rs, :] = z_dg.astype(BF16)
        lo = CONV_PAD + rs.start
        nrow = rs.stop - rs.start
        cbuf[lo:lo + nrow, :] = z_qkv
        ext = cbuf[lo - CONV_PAD:lo + nrow, :]
        y = z_qkv * cw[CONV_K - 1:CONV_K, :]
        for d in range(1, CONV_K):
            shifted = pltpu.roll(ext, d, 0)[CONV_PAD:, :]
            y = y + shifted * cw[CONV_K - 1 - d:CONV_K - d, :]
        qkv_ref[rs, :] = _silu(y).astype(BF16)
        v_ref[rs, :] = kvu[:, H_C * D_NOPE:].astype(BF16)
        beta = _sigmoid(ba)
        g = -jnp.exp(alog_ref[...]) * jax.nn.softplus(ba + dtb_ref[...])
        lane = lax.broadcasted_iota(jnp.int32, ba.shape, 1)
        gb_ref[rs, :] = jnp.where(lane < H_D, beta, g)
        for hh in range(H_C):
            nope = qu[:, hh * D_NOPE:(hh + 1) * D_NOPE]
            rope = qu[:, H_C * D_NOPE + hh * LANES:H_C * D_NOPE + (hh + 1) * LANES]
            ss = jnp.sum(nope * nope + rope * rope, axis=-1, keepdims=True)
            r = lax.rsqrt(ss * (1.0 / DQK_C) + EPS) * scale
            q_ref[rs, hh * HEAD_PAD:hh * HEAD_PAD + LANES] = (nope * r * qn[:, 0:LANES]).astype(BF16)
            q_ref[rs, hh * HEAD_PAD + LANES:(hh + 1) * HEAD_PAD] = _rope_block(
                rope * r * qn[:, LANES:2 * LANES], c, s, D_ROPE // 2).astype(BF16)
        kr_ss = jnp.sum(kr * kr, axis=-1, keepdims=True)
        kr_rot = _rope_block(kr * kn[:, LANES:2 * LANES], c, s, D_ROPE // 2)
        for hh in range(H_C):
            nope = kvu[:, hh * D_NOPE:(hh + 1) * D_NOPE]
            ss = jnp.sum(nope * nope, axis=-1, keepdims=True) + kr_ss
            r = lax.rsqrt(ss * (1.0 / DQK_C) + EPS)
            k_ref[rs, hh * HEAD_PAD:hh * HEAD_PAD + LANES] = (nope * r * kn[:, 0:LANES]).astype(BF16)
            k_ref[rs, hh * HEAD_PAD + LANES:(hh + 1) * HEAD_PAD] = (kr_rot * r).astype(BF16)

    part = x.shape[0] // IN_PROJ_PARTS
    parts = [slice(k * part, (k + 1) * part) for k in range(IN_PROJ_PARTS)]
    z_prev = project(parts[0])
    for k in range(1, IN_PROJ_PARTS):
        z_next = project(parts[k])
        epilogue(parts[k - 1], z_prev)
        z_prev = z_next
    epilogue(parts[-1], z_prev)
    cbuf[0:CONV_PAD, :] = cbuf[tm:tm + CONV_PAD, :]


def _odd_in(h, g, w_in, qan, wuq, kvan, wukv, qn, kn, c_tab, s_tab, alog, dtb, conv_w, seq):
    n = h.shape[0]
    tm = min(ROW_TILE, seq)
    row = lambda w: pl.BlockSpec((tm, w), lambda i: (i, 0))
    full = lambda a: pl.BlockSpec(a.shape, lambda i: (0,) * a.ndim)
    sds = jax.ShapeDtypeStruct
    return pl.pallas_call(
        functools.partial(_odd_in_kernel, tm=tm, seq=seq),
        grid=(n // tm,),
        in_specs=[row(D_MODEL), full(g), full(w_in), full(qan), full(wuq), full(kvan), full(wukv),
                  full(qn), full(kn), row(LANES), row(LANES), full(alog), full(dtb), full(conv_w)],
        out_specs=[row(H_C * HEAD_PAD), row(H_C * HEAD_PAD), row(W_C), row(W_C), row(CONV_CH), row(W_D),
                   row(LANES)],
        out_shape=[sds((n, H_C * HEAD_PAD), BF16), sds((n, H_C * HEAD_PAD), BF16), sds((n, W_C), BF16),
                   sds((n, W_C), BF16), sds((n, CONV_CH), BF16), sds((n, W_D), BF16), sds((n, LANES), F32)],
        scratch_shapes=[pltpu.VMEM((tm + CONV_PAD, CONV_CH), F32)],
        compiler_params=_cparams(("arbitrary",)),
        name="odd_in_proj",
    )(h, g, w_in, qan, wuq, kvan, wukv, qn, kn, c_tab, s_tab, alog, dtb, conv_w)


def _gdn_prep_kernel(y_ref, gbc_ref, lhs1_ref, lhs2_ref, u_ref, gl_ref, *, tm):
    ri = lax.broadcasted_iota(jnp.int32, (tm, tm), 0)
    ci = lax.broadcasted_iota(jnp.int32, (tm, tm), 1)
    low = jnp.where(((ri // CHUNK) == (ci // CHUNK)) & (ri >= ci), 1.0, 0.0).astype(F32)
    i2 = lax.broadcasted_iota(jnp.int32, (LANES, LANES), 0)
    j2 = lax.broadcasted_iota(jnp.int32, (LANES, LANES), 1)
    eye_l = jnp.where(i2 == j2, 1.0, 0.0).astype(BF16)
    gbc = gbc_ref[...]
    gcs = _split3_dot(low, gbc)
    gcs_t = _split3_dot_nt(eye_l, gcs)

    pdim = H_D * CHUNK
    r4 = lax.broadcasted_iota(jnp.int32, (pdim, pdim), 0)
    c4 = lax.broadcasted_iota(jnp.int32, (pdim, pdim), 1)
    same = (r4 // CHUNK) == (c4 // CHUNK)
    incl = same & (r4 >= c4)
    strict = same & (r4 > c4)
    eye = jnp.where(r4 == c4, 1.0, 0.0).astype(F32)
    zblk = jnp.zeros((CHUNK, DK_D), F32)

    def block_diag(blocks):
        return jnp.concatenate(
            [jnp.concatenate([blocks[h] if j == h else zblk for j in range(H_D)], axis=1) for h in range(H_D)],
            axis=0)

    chunks = list(range(tm // CHUNK))
    heads = list(range(H_D))

    def load_qkv(c):
        r0 = c * CHUNK
        qs, ks, vs = [], [], []
        for hh in heads:
            l0 = hh * DK_D
            qh = y_ref[r0:r0 + CHUNK, l0:l0 + DK_D].astype(F32)
            kh = y_ref[r0:r0 + CHUNK, H_D * DK_D + l0:H_D * DK_D + l0 + DK_D].astype(F32)
            vs.append(y_ref[r0:r0 + CHUNK, 2 * H_D * DK_D + l0:2 * H_D * DK_D + l0 + DV_D].astype(F32))
            qs.append(qh * lax.rsqrt(jnp.sum(qh * qh, axis=-1, keepdims=True) + EPS) * (DK_D ** -0.5))
            ks.append(kh * lax.rsqrt(jnp.sum(kh * kh, axis=-1, keepdims=True) + EPS))
        return qs, ks, vs

    def decay_terms(c):
        r0 = c * CHUNK
        beta = jnp.concatenate([gbc[r0:r0 + CHUNK, hh:hh + 1] for hh in heads], axis=0)
        gcol = jnp.concatenate([gcs[r0:r0 + CHUNK, H_D + hh:H_D + hh + 1] for hh in heads], axis=0)
        grow = jnp.concatenate([gcs_t[H_D + hh:H_D + hh + 1, r0:r0 + CHUNK] for hh in heads], axis=1)
        glasts = [gcs[r0 + CHUNK - 1:r0 + CHUNK, H_D + hh:H_D + hh + 1] for hh in heads]
        glast = jnp.concatenate([jnp.broadcast_to(gl, (CHUNK, 1)) for gl in glasts], axis=0)
        gamma = jnp.exp(jnp.where(incl, gcol - grow, -jnp.inf))
        return beta, gcol, glasts, glast, gamma

    qkv = [load_qkv(c) for c in chunks]
    q_st = [jnp.concatenate(x[0], axis=0) for x in qkv]
    k_st = [jnp.concatenate(x[1], axis=0) for x in qkv]
    v_st = [jnp.concatenate(x[2], axis=0) for x in qkv]
    k_bd = [block_diag(x[1]).astype(BF16) for x in qkv]
    q_bd = [block_diag(x[0]).astype(BF16) for x in qkv]
    qkkk = [_bdot_nt(jnp.concatenate([qb, kb], axis=0), kb) for qb, kb in zip(q_bd, k_bd)]
    dec = [decay_terms(c) for c in chunks]
    a = [jnp.where(strict, x[pdim:, :] * d[4] * d[0], 0.0) for x, d in zip(qkkk, dec)]
    t = [eye - x for x in a]
    pw = a
    for _ in range(5):
        pw = [_bdot(x, x) for x in pw]
        t = [x + _bdot(x, y) for x, y in zip(t, pw)]
    egc = [jnp.exp(d[1]) for d in dec]
    uw = [_bdot(tt, jnp.concatenate([v * d[0], k * (d[0] * e)], axis=1))
          for tt, v, k, d, e in zip(t, v_st, k_st, dec, egc)]
    kdt = [_bdot_nt(eye_l, k * jnp.exp(d[3] - d[1])).astype(BF16) for k, d in zip(k_st, dec)]
    for c in chunks:
        qkg = (qkkk[c][0:pdim, :] * dec[c][4]).astype(BF16)
        q_dec = (q_st[c] * egc[c]).astype(BF16)
        for hh in heads:
            idx = c * H_D + hh
            hrows = slice(hh * CHUNK, (hh + 1) * CHUNK)
            lhs1_ref[idx, 0:CHUNK, :] = uw[c][hrows, DV_D:].astype(BF16)
            lhs1_ref[idx, CHUNK:2 * CHUNK, :] = q_dec[hrows, :]
            u_ref[idx] = uw[c][hrows, 0:DV_D].astype(BF16)
            gl_ref[c, hh:hh + 1, :] = jnp.broadcast_to(jnp.exp(dec[c][2][hh]), (1, LANES))
        for pp in range(H_D // 2):
            idx2 = c * (H_D // 2) + pp
            lhs2_ref[idx2, 0:LANES, :] = qkg[pp * LANES:(pp + 1) * LANES, pp * LANES:(pp + 1) * LANES]
            lhs2_ref[idx2, LANES:2 * LANES, :] = kdt[c][:, pp * LANES:(pp + 1) * LANES]


def _gdn_prep(y, gb, seq):
    n = y.shape[0]
    tm = min(GDN_PREP_TILE, seq)
    nch = tm // CHUNK
    row = lambda w: pl.BlockSpec((tm, w), lambda i: (i, 0))
    ch3 = lambda m, r, w: pl.BlockSpec((nch * m, r, w), lambda i: (i, 0, 0))
    sds = jax.ShapeDtypeStruct
    nc = n // CHUNK
    return pl.pallas_call(
        functools.partial(_gdn_prep_kernel, tm=tm),
        grid=(n // tm,),
        in_specs=[row(CONV_CH), row(LANES)],
        out_specs=[ch3(H_D, 2 * CHUNK, DK_D), ch3(H_D // 2, 2 * LANES, LANES), ch3(H_D, CHUNK, DV_D),
                   pl.BlockSpec((nch, H_D, LANES), lambda i: (i, 0, 0))],
        out_shape=[sds((nc * H_D, 2 * CHUNK, DK_D), BF16), sds((nc * H_D // 2, 2 * LANES, LANES), BF16),
                   sds((nc * H_D, CHUNK, DV_D), BF16), sds((nc, H_D, LANES), F32)],
        compiler_params=_cparams(("parallel",)),
        name="gdn_chunk_prep",
    )(y, gb)


def _gdn_scan_kernel(lhs1_ref, lhs2_ref, u_ref, gl_ref, dg_ref, on_ref, o_ref, s_scr, *, nb, tb):
    @pl.when(pl.program_id(0) == 0)
    def _():
        s_scr[...] = jnp.zeros(s_scr.shape, F32)

    on = on_ref[...]
    zb = jnp.zeros((2 * CHUNK, DK_D), BF16)
    left = lax.broadcasted_iota(jnp.int32, (DK_D, LANES), 1) < CHUNK

    def finish(o, b, rows, hh):
        ms = jnp.mean(o * o, axis=-1, keepdims=True)
        o = o * lax.rsqrt(ms + EPS) * on
        gate = dg_ref[b, rows, hh * DV_D:(hh + 1) * DV_D].astype(F32)
        o_ref[b, rows, hh * DV_D:(hh + 1) * DV_D] = (o * _silu(gate)).astype(BF16)

    def chunk_body(c, carry):
        rows = pl.ds(pl.multiple_of(c * CHUNK, CHUNK), CHUNK)
        chains = [(b, 2 * pp) for b in range(nb) for pp in range(H_D // 2)]
        st = [(s_scr[b * H_D + h0], s_scr[b * H_D + h0 + 1]) for b, h0 in chains]
        r1 = [jnp.dot(jnp.concatenate([jnp.concatenate([lhs1_ref[b, c * H_D + h0], zb], axis=1),
                                       jnp.concatenate([zb, lhs1_ref[b, c * H_D + h0 + 1]], axis=1)], axis=0),
                      jnp.concatenate([s0, s1], axis=0).astype(BF16), preferred_element_type=F32)
              for (b, h0), (s0, s1) in zip(chains, st)]
        v2 = [jnp.concatenate([u_ref[b, c * H_D + h0].astype(F32) - x[0:CHUNK, :],
                               u_ref[b, c * H_D + h0 + 1].astype(F32) - x[2 * CHUNK:3 * CHUNK, :]],
                              axis=0).astype(BF16) for (b, h0), x in zip(chains, r1)]
        r2 = []
        for (b, h0), v in zip(chains, v2):
            blk = lhs2_ref[b, c * (H_D // 2) + h0 // 2]
            kd = blk[LANES:2 * LANES, :]
            zk = jnp.zeros_like(kd)
            l2 = jnp.concatenate([blk[0:LANES, :], jnp.where(left, kd, zk), jnp.where(left, zk, kd)], axis=0)
            r2.append(jnp.dot(l2, v, preferred_element_type=F32))
        for (b, h0), (s0, s1), x, y in zip(chains, st, r1, r2):
            s_scr[b * H_D + h0] = s0 * gl_ref[b, c, h0:h0 + 1, :] + y[LANES:LANES + DK_D, :]
            s_scr[b * H_D + h0 + 1] = s1 * gl_ref[b, c, h0 + 1:h0 + 2, :] + y[LANES + DK_D:LANES + 2 * DK_D, :]
            finish(x[CHUNK:2 * CHUNK, :] + y[0:CHUNK, :], b, rows, h0)
            finish(x[3 * CHUNK:4 * CHUNK, :] + y[CHUNK:2 * CHUNK, :], b, rows, h0 + 1)
        return carry

    lax.fori_loop(0, tb // CHUNK, chunk_body, 0)


def _gdn_scan(lhs1, lhs2, u, gl, d_gate, o_norm, batch, seq):
    tb = min(GDN_SCAN_TILE, seq)
    nch = tb // CHUNK
    ncb = seq // CHUNK
    lhs1 = lhs1.reshape(batch, ncb * H_D, 2 * CHUNK, DK_D)
    lhs2 = lhs2.reshape(batch, ncb * H_D // 2, 2 * LANES, LANES)
    u = u.reshape(batch, ncb * H_D, CHUNK, DV_D)
    gl = gl.reshape(batch, ncb, H_D, LANES)
    d_gate = d_gate.reshape(batch, seq, W_D)
    ch4 = lambda r, w: pl.BlockSpec((batch, nch * H_D, r, w), lambda i: (0, i, 0, 0))
    tok = pl.BlockSpec((batch, tb, W_D), lambda i: (0, i, 0))
    out = pl.pallas_call(
        functools.partial(_gdn_scan_kernel, nb=batch, tb=tb),
        grid=(seq // tb,),
        in_specs=[ch4(2 * CHUNK, DK_D),
                  pl.BlockSpec((batch, nch * H_D // 2, 2 * LANES, LANES), lambda i: (0, i, 0, 0)),
                  ch4(CHUNK, DV_D),
                  pl.BlockSpec((batch, nch, H_D, LANES), lambda i: (0, i, 0, 0)), tok,
                  pl.BlockSpec(o_norm.shape, lambda i: (0, 0))],
        out_specs=tok,
        out_shape=jax.ShapeDtypeStruct((batch, seq, W_D), BF16),
        scratch_shapes=[pltpu.VMEM((batch * H_D, DK_D, DV_D), F32)],
        compiler_params=_cparams(("arbitrary",)),
        name="gdn_state_scan",
    )(lhs1, lhs2, u, gl, d_gate, o_norm)
    return out.reshape(batch * seq, W_D)


def _rope_patterns(rot_dim, theta, period):
    half = rot_dim // 2
    inv = jnp.power(jnp.float32(theta), -jnp.arange(half, dtype=F32) * (2.0 / rot_dim))
    lane = np.arange(LANES)
    in_rot = (lane % period) < rot_dim
    idx = jnp.asarray(lane % half)
    inv_l = jnp.where(jnp.asarray(in_rot), inv[idx], 0.0).astype(F32)[None, :]
    sgn = np.where(in_rot, np.where((lane % period) < half, -1.0, 1.0), 0.0).astype(np.float32)[None, :]
    return inv_l, jnp.asarray(sgn)


def _pad_cols(w, width):
    return jnp.pad(w, ((0, 0), (0, width - w.shape[1])))


def _odd_w_in_layout(w):
    cq, ckv, kr, cg, qkv, db, da, dg = jnp.split(
        w, np.cumsum((Q_LORA, KV_LORA, D_ROPE, W_C, CONV_CH, H_D, H_D, W_D))[:-1].tolist(), axis=1)
    return jnp.concatenate([cq, ckv, _pad_cols(kr, LANES), cg, qkv, dg,
                            _pad_cols(jnp.concatenate([db, da], axis=1), LANES)], axis=1)


def _wuq_layout(w):
    w = w.reshape(Q_LORA, H_C, DQK_C)
    nope = w[:, :, :D_NOPE].reshape(Q_LORA, H_C * D_NOPE)
    rope = jnp.pad(w[:, :, D_NOPE:], ((0, 0), (0, 0), (0, LANES - D_ROPE))).reshape(Q_LORA, H_C * LANES)
    return jnp.concatenate([nope, rope], axis=1)


def _wukv_layout(w):
    w = w.reshape(KV_LORA, H_C, D_NOPE + DV_C)
    return jnp.concatenate([w[:, :, :D_NOPE].reshape(KV_LORA, H_C * D_NOPE),
                            w[:, :, D_NOPE:].reshape(KV_LORA, H_C * DV_C)], axis=1)


def _head_gain_layout(g):
    return jnp.pad(g, (0, HEAD_PAD - DQK_C))[None, :].astype(F32)


def kernel(x, p, positions, norm_g, ple_w_gate, ple_w_proj, ev_w_in, ev_pool_w, ev_pool_scale, ev_q_norm, ev_k_norm, ev_lambda, ev_subln, ev_w_out, od_w_in, od_q_a_norm, od_w_uq, od_kv_a_norm, od_w_ukv, od_q_norm, od_k_norm, od_conv_w, od_a_log, od_dt_bias, od_o_norm, od_w_out):
    batch, seq, _ = x.shape
    depth = p.shape[0]
    n = batch * seq
    h = x.reshape(n, D_MODEL)
    pos_b = jnp.broadcast_to(positions.astype(F32).reshape(n, 1), (n, LANES))

    inv_e, sgn_e = _rope_patterns(B_ROT, ROPE_THETA, DK_B)
    ce, se = _rope_tables(pos_b, inv_e, sgn_e)
    inv_o, sgn_o = _rope_patterns(D_ROPE, MLA_THETA, LANES)
    co, so = _rope_tables(pos_b, inv_o, sgn_o)


    for i in range(depth):
        j = i // 2
        g = norm_g[i][None, :]
        if i % 2 == 0:
            w_in = ev_w_in[j].astype(BF16)
            qg = (jnp.tile(ev_q_norm[j], 2 * H_B) * (DK_B ** -0.5 * LOG2E))[None, :]
            kg = jnp.tile(ev_k_norm[j], 2 * H_B)[None, :]
            a_in, a_gate, q, k, v, b_gate = _even_in(h, g, w_in, qg, kg, ce, se)
            lam_init = 0.8 - 0.6 * math.exp(-0.3 * i)
            ob = _diff_attention(q, k, v, b_gate, ev_lambda[j], ev_subln[j][None, :], batch, seq, lam_init)
            pw = jax.scipy.linalg.block_diag(*[ev_pool_w[j][gi] for gi in range(len(POOL_WINDOWS))]).astype(BF16)
            oa = _pool_mixer(a_in, a_gate, pw, ev_pool_scale[j][None, :], seq)
            w_out = ev_w_out[j].astype(BF16)
        else:
            w_in = _odd_w_in_layout(od_w_in[j]).astype(BF16)
            alog = jnp.zeros((LANES,), F32).at[H_D:2 * H_D].set(od_a_log[j])[None, :]
            dtb = jnp.zeros((LANES,), F32).at[H_D:2 * H_D].set(od_dt_bias[j])[None, :]
            q, k, v, c_gate, qkv, d_gate, gb = _odd_in(
                h, g, w_in, od_q_a_norm[j][None, :], _wuq_layout(od_w_uq[j]).astype(BF16),
                od_kv_a_norm[j][None, :], _wukv_layout(od_w_ukv[j]).astype(BF16),
                _head_gain_layout(od_q_norm[j]), _head_gain_layout(od_k_norm[j]), co, so, alog, dtb,
                od_conv_w[j], seq)
            oa = _mla_attention(q, k, v, c_gate, batch, seq)
            lhs1, lhs2, u, gl = _gdn_prep(qkv, gb, seq)
            ob = _gdn_scan(lhs1, lhs2, u, gl, d_gate, od_o_norm[j][None, :], batch, seq)
            w_out = od_w_out[j].astype(BF16)
        h = _out_proj(h, oa, ob, w_out, ple_w_gate[i].astype(BF16), p.reshape(depth, n, PLE_DIM), i,
                      ple_w_proj[i].astype(BF16))
    return h.reshape(batch, seq, D_MODEL)
```

```python
import functools
import math

import numpy as np
import jax
import jax.numpy as jnp
from jax import lax
from jax.experimental import pallas as pl
from jax.experimental.pallas import tpu as pltpu

F32 = jnp.float32
BF16 = jnp.bfloat16

D_MODEL = 1024
PLE_DIM = 256
EPS = 1e-6
W_A = 256
POOL_WINDOWS = (2, 4, 8, 16)
POOL_GC = 64
W_B = 768
DV_B = 128
H_B = 6
DK_B = 64
B_ROT = 16
ROPE_THETA = 500000.0
W_C = 512
DV_C = 128
H_C = 4
D_NOPE = 128
D_ROPE = 64
DQK_C = 192
Q_LORA = 256
KV_LORA = 128
MLA_THETA = 10000.0
W_D = 512
DK_D = 128
DV_D = 128
H_D = 4
CONV_K = 4
CHUNK = 64
CONV_CH = H_D * (2 * DK_D + DV_D)

LANES = 128
VMEM_LIMIT = 48 * 1024 * 1024
NEG = -1e30
LOG2E = math.log2(math.e)

ROW_TILE = 512
IN_PROJ_PARTS = 2
DIFF_TK = 1024
MLA_TK = 1024
DIAG_TK = 512
POOL_HALO = 32
CONV_PAD = 8
GDN_PREP_TILE = 256
GDN_SCAN_TILE = 256
DIFF_HEADS_PER_STEP = 2
MLA_HEADS_PER_STEP = 2
DIFF_TQ = 512
MLA_TQ = 1024


def _cparams(sem):
    return pltpu.CompilerParams(dimension_semantics=sem, vmem_limit_bytes=VMEM_LIMIT)


def _silu(x):
    return x * (1.0 / (1.0 + jnp.exp(-x)))


def _sigmoid(x):
    return 1.0 / (1.0 + jnp.exp(-x))


def _bdot(a, b):
    return jnp.dot(a.astype(BF16), b.astype(BF16), preferred_element_type=F32)


def _bdot_nt(a, b):
    return lax.dot_general(a.astype(BF16), b.astype(BF16), (((1,), (1,)), ((), ())),
                           preferred_element_type=F32)


def _split3_dot(a_exact, b):
    b0 = b.astype(BF16)
    r1 = b - b0.astype(F32)
    b1 = r1.astype(BF16)
    b2 = (r1 - b1.astype(F32)).astype(BF16)
    a = a_exact.astype(BF16)
    return (jnp.dot(a, b0, preferred_element_type=F32) + jnp.dot(a, b1, preferred_element_type=F32)
            + jnp.dot(a, b2, preferred_element_type=F32))


def _split3_dot_nt(a_exact, b):
    b0 = b.astype(BF16)
    r1 = b - b0.astype(F32)
    b1 = r1.astype(BF16)
    b2 = (r1 - b1.astype(F32)).astype(BF16)
    return _bdot_nt(a_exact, b0) + _bdot_nt(a_exact, b1) + _bdot_nt(a_exact, b2)


def _rope_table_kernel(pos_ref, inv_ref, c_ref, s_ref, ns_ref):
    ang = pos_ref[...] * inv_ref[...]
    sn = jnp.sin(ang)
    c_ref[...] = jnp.cos(ang)
    s_ref[...] = sn
    ns_ref[...] = -sn


def _rope_tables(positions, rot_dim, theta, period):
    n = positions.shape[0]
    half = rot_dim // 2
    per_row = LANES // half
    inv = jnp.power(jnp.float32(theta), -jnp.arange(half, dtype=F32) * (2.0 / rot_dim))
    pos_c = jnp.repeat(positions.astype(F32).reshape(n // per_row, per_row), half, axis=1)
    inv_c = jnp.tile(inv, per_row)[None, :]
    rows = n // per_row
    tm = min(1024, rows)
    row = pl.BlockSpec((tm, LANES), lambda i: (i, 0))
    par = pl.BlockSpec((1, LANES), lambda i: (0, 0))
    cs, sn, nsn = pl.pallas_call(
        _rope_table_kernel,
        grid=(rows // tm,),
        in_specs=[row, par],
        out_specs=[row, row, row],
        out_shape=[jax.ShapeDtypeStruct((rows, LANES), F32)] * 3,
        compiler_params=_cparams(("parallel",)),
        name="rope_tables",
    )(pos_c, inv_c)
    cs, sn, nsn = (a.reshape(n, half) for a in (cs, sn, nsn))
    rest = period - rot_dim
    c_pat = jnp.concatenate([cs, cs, jnp.ones((n, rest), F32)], axis=1)
    s_pat = jnp.concatenate([nsn, sn, jnp.zeros((n, rest), F32)], axis=1)
    return jnp.tile(c_pat, (1, LANES // period)), jnp.tile(s_pat, (1, LANES // period))


def _rope_block(x, c, s, half):
    fwd = pltpu.roll(x, LANES - half, 1)
    bwd = pltpu.roll(x, half, 1)
    lane = lax.broadcasted_iota(jnp.int32, x.shape, 1)
    rot = jnp.where((lane % (2 * half)) < half, fwd, bwd)
    return x * c + rot * s


def _even_in_kernel(h_ref, g_ref, w_ref, qg_ref, kg_ref, c_ref, s_ref,
                    ain_ref, agate_ref, q_ref, k_ref, v_ref, bgate_ref):
    x = h_ref[...]
    ms = jnp.mean(x * x, axis=-1, keepdims=True)
    yb = (x * lax.rsqrt(ms + EPS) * g_ref[...]).astype(BF16)

    c_all = c_ref[...]
    s_all = s_ref[...]
    o = 2 * W_A
    bounds = [(0, W_A), (W_A, 2 * W_A), (o, o + W_B), (o + W_B, o + 2 * W_B), (o + 2 * W_B, o + 3 * W_B),
              (o + 3 * W_B, o + 4 * W_B)]

    def project(rs):
        return [jnp.dot(yb[rs, :], w_ref[:, lo:hi], preferred_element_type=F32) for lo, hi in bounds]

    def qk_prep(rs, z, gain_ref, out_ref, c, s):
        gain = gain_ref[...]
        first = lax.broadcasted_iota(jnp.int32, (z.shape[0], LANES), 1) < DK_B
        for j in range(W_B // LANES):
            blk = z[:, j * LANES:(j + 1) * LANES]
            sq = blk * blk
            ss0 = jnp.sum(jnp.where(first, sq, 0.0), axis=-1, keepdims=True)
            ss1 = jnp.sum(jnp.where(first, 0.0, sq), axis=-1, keepdims=True)
            r = jnp.where(first, lax.rsqrt(ss0 * (1.0 / DK_B) + EPS), lax.rsqrt(ss1 * (1.0 / DK_B) + EPS))
            zn = blk * r * gain[:, j * LANES:(j + 1) * LANES]
            out_ref[rs, j * LANES:(j + 1) * LANES] = _rope_block(zn, c, s, B_ROT // 2).astype(BF16)

    def epilogue(rs, zs):
        z_ain, z_agate, z_q, z_k, z_v, z_bg = zs
        ain_ref[rs, :] = z_ain
        agate_ref[rs, :] = z_agate.astype(BF16)
        v_ref[rs, :] = z_v.astype(BF16)
        bgate_ref[rs, :] = z_bg.astype(BF16)
        qk_prep(rs, z_q, qg_ref, q_ref, c_all[rs, :], s_all[rs, :])
        qk_prep(rs, z_k, kg_ref, k_ref, c_all[rs, :], s_all[rs, :])

    part = x.shape[0] // IN_PROJ_PARTS
    parts = [slice(k * part, (k + 1) * part) for k in range(IN_PROJ_PARTS)]
    z_prev = project(parts[0])
    for k in range(1, IN_PROJ_PARTS):
        z_next = project(parts[k])
        epilogue(parts[k - 1], z_prev)
        z_prev = z_next
    epilogue(parts[-1], z_prev)


def _even_in(h, g, w_in, qg, kg, c_tab, s_tab):
    n = h.shape[0]
    tm = min(ROW_TILE, n)
    row = lambda w: pl.BlockSpec((tm, w), lambda i: (i, 0))
    full = lambda a: pl.BlockSpec(a.shape, lambda i: (0,) * a.ndim)
    sds = jax.ShapeDtypeStruct
    return pl.pallas_call(
        _even_in_kernel,
        grid=(n // tm,),
        in_specs=[row(D_MODEL), full(g), full(w_in), full(qg), full(kg), row(LANES), row(LANES)],
        out_specs=[row(W_A), row(W_A), row(W_B), row(W_B), row(W_B), row(W_B)],
        out_shape=[sds((n, W_A), F32), sds((n, W_A), BF16), sds((n, W_B), BF16), sds((n, W_B), BF16),
                   sds((n, W_B), BF16), sds((n, W_B), BF16)],
        compiler_params=_cparams(("parallel",)),
        name="even_in_proj",
    )(h, g, w_in, qg, kg, c_tab, s_tab)


def _flash_loop(qs_ref, k_ref, v_ref, m_scr, l_scr, acc_scr, qi, tq, tk, heads, dqk, dv):
    q0 = qi * tq
    rows = qs_ref.shape[1]
    tkt = min(tq, tk, DIAG_TK)
    n_main = q0 // tk
    base = n_main * tk
    n_tail_full = (q0 - base) // tkt
    n_tail_diag = tq // tkt
    assert n_tail_diag == 1 or rows == tq
    m_scr[...] = jnp.full(m_scr.shape, NEG, F32)
    l_scr[...] = jnp.zeros(l_scr.shape, F32)
    acc_scr[...] = jnp.zeros(acc_scr.shape, F32)

    def step(ks, tk, masked, row_lo=0):
        hs = range(heads)
        rs = slice(row_lo, rows)
        s = [lax.dot_general(qs_ref[g, rs, :], k_ref[pl.ds(ks, tk), g * dqk:(g + 1) * dqk],
                             (((1,), (1,)), ((), ())), preferred_element_type=F32) for g in hs]
        if masked:
            row = q0 + (row_lo + lax.broadcasted_iota(jnp.int32, s[0].shape, 0)) % tq
            col = ks + lax.broadcasted_iota(jnp.int32, s[0].shape, 1)
            s = [jnp.where(col <= row, x, NEG) for x in s]
        m_prev = [m_scr[g, rs, :] for g in hs]
        m_new = [jnp.maximum(mp, jnp.max(x, axis=1, keepdims=True)) for mp, x in zip(m_prev, s)]
        alpha = [jnp.exp2(mp - mn) for mp, mn in zip(m_prev, m_new)]
        p = [jnp.exp2(x - jnp.tile(mn, (1, tk // LANES))) for x, mn in zip(s, m_new)]
        ones = jnp.ones((tk, LANES), BF16)
        pv = [jnp.dot(p[g].astype(BF16),
                      jnp.concatenate([v_ref[pl.ds(ks, tk), g * dv:(g + 1) * dv], ones], axis=1),
                      preferred_element_type=F32) for g in hs]
        for g in hs:
            m_scr[g, rs, :] = m_new[g]
            l_scr[g, rs, :] = alpha[g] * l_scr[g, rs, :] + pv[g][:, dv:]
            acc_scr[g, rs, :] = alpha[g] * acc_scr[g, rs, :] + pv[g][:, 0:dv]

    def main_body(j, carry):
        step(pl.multiple_of(j * tk, tk), tk, False)
        return carry

    def tail_body(j, carry):
        step(pl.multiple_of(base + j * tkt, tkt), tkt, False)
        return carry

    lax.fori_loop(0, n_main, main_body, 0)
    if tq % tk:
        lax.fori_loop(0, n_tail_full, tail_body, 0)
    for j in range(n_tail_diag):
        step(pl.multiple_of(base + (n_tail_full + j) * tkt, tkt), tkt, True, row_lo=j * tkt)


def _diff_attn_kernel(q_ref, k_ref, v_ref, bg_ref, lam_ref, sub_ref, o_ref,
                      qs_scr, m_scr, l_scr, acc_scr, *, tq, tk, heads, lam_init):
    qi = pl.program_id(2)
    for g in range(heads):
        q = q_ref[:, g * DV_B:(g + 1) * DV_B]
        lane = lax.broadcasted_iota(jnp.int32, q.shape, 1)
        zero = jnp.zeros_like(q)
        qs_scr[g, 0:tq, :] = jnp.where(lane < DK_B, q, zero)
        qs_scr[g, tq:2 * tq, :] = jnp.where(lane >= DK_B, q, zero)
    _flash_loop(qs_scr, k_ref, v_ref, m_scr, l_scr, acc_scr, qi, tq, tk, heads, DV_B, DV_B)
    lv = lam_ref[...]
    lam = (jnp.exp(jnp.sum(lv[0:1] * lv[1:2], axis=1, keepdims=True))
           - jnp.exp(jnp.sum(lv[2:3] * lv[3:4], axis=1, keepdims=True)) + lam_init)
    for g in range(heads):
        o1 = acc_scr[g, 0:tq, :] / l_scr[g, 0:tq, :]
        o2 = acc_scr[g, tq:2 * tq, :] / l_scr[g, tq:2 * tq, :]
        o = o1 - lam * o2
        ms = jnp.mean(o * o, axis=-1, keepdims=True)
        o = o * lax.rsqrt(ms + EPS) * sub_ref[...] * (1.0 - lam_init)
        gate = bg_ref[:, g * DV_B:(g + 1) * DV_B].astype(F32)
        o_ref[:, g * DV_B:(g + 1) * DV_B] = (o * _silu(gate)).astype(BF16)


def _diff_attention(q, k, v, bgate, lam_vec, subln, batch, seq, lam_init):
    n = q.shape[0]
    tq = min(DIFF_TQ, seq)
    tk = min(DIFF_TK, seq)
    nq = seq // tq
    heads = DIFF_HEADS_PER_STEP
    wid = heads * DV_B
    qspec = pl.BlockSpec((tq, wid), lambda b, h, i: (b * nq + i, h))
    kvspec = pl.BlockSpec((seq, wid), lambda b, h, i: (b, h))
    full = lambda a: pl.BlockSpec(a.shape, lambda b, h, i: (0,) * a.ndim)
    return pl.pallas_call(
        functools.partial(_diff_attn_kernel, tq=tq, tk=tk, heads=heads, lam_init=lam_init),
        grid=(batch, H_B // heads, nq),
        in_specs=[qspec, kvspec, kvspec, qspec, full(lam_vec), full(subln)],
        out_specs=qspec,
        out_shape=jax.ShapeDtypeStruct((n, W_B), BF16),
        scratch_shapes=[pltpu.VMEM((heads, 2 * tq, DV_B), BF16), pltpu.VMEM((heads, 2 * tq, LANES), F32),
                        pltpu.VMEM((heads, 2 * tq, LANES), F32), pltpu.VMEM((heads, 2 * tq, DV_B), F32)],
        compiler_params=_cparams(("parallel", "parallel", "arbitrary")),
        name="diff_attention",
    )(q, k, v, bgate, lam_vec, subln)


def _mla_attn_kernel(q_ref, k_ref, v_ref, cg_ref, o_ref, qs_scr, m_scr, l_scr, acc_scr, *, tq, tk, heads):
    qi = pl.program_id(2)
    for g in range(heads):
        qs_scr[g] = q_ref[:, g * HEAD_PAD:(g + 1) * HEAD_PAD]
    _flash_loop(qs_scr, k_ref, v_ref, m_scr, l_scr, acc_scr, qi, tq, tk, heads, HEAD_PAD, DV_C)
    for g in range(heads):
        o = acc_scr[g] / l_scr[g]
        gate = cg_ref[:, g * DV_C:(g + 1) * DV_C].astype(F32)
        o_ref[:, g * DV_C:(g + 1) * DV_C] = (o * _silu(gate)).astype(BF16)


def _mla_attention(q, k, v, cgate, batch, seq):
    n = q.shape[0]
    tq = min(MLA_TQ, seq)
    tk = min(MLA_TK, seq)
    nq = seq // tq
    heads = MLA_HEADS_PER_STEP
    qspec = pl.BlockSpec((tq, heads * HEAD_PAD), lambda b, h, i: (b * nq + i, h))
    kspec = pl.BlockSpec((seq, heads * HEAD_PAD), lambda b, h, i: (b, h))
    vspec = pl.BlockSpec((seq, heads * DV_C), lambda b, h, i: (b, h))
    ospec = pl.BlockSpec((tq, heads * DV_C), lambda b, h, i: (b * nq + i, h))
    return pl.pallas_call(
        functools.partial(_mla_attn_kernel, tq=tq, tk=tk, heads=heads),
        grid=(batch, H_C // heads, nq),
        in_specs=[qspec, kspec, vspec, ospec],
        out_specs=ospec,
        out_shape=jax.ShapeDtypeStruct((n, W_C), BF16),
        scratch_shapes=[pltpu.VMEM((heads, tq, HEAD_PAD), BF16), pltpu.VMEM((heads, tq, LANES), F32),
                        pltpu.VMEM((heads, tq, LANES), F32), pltpu.VMEM((heads, tq, DV_C), F32)],
        compiler_params=_cparams(("parallel", "parallel", "arbitrary")),
        name="mla_attention",
    )(q, k, v, cgate)


def _pool_values(x_ref, halo_ref, gate_ref, pw_ref, ps_ref, buf_a, buf_b, tm, seq):
    i = pl.program_id(0)
    t0 = (i * tm) % seq
    x = x_ref[...]
    keep = (t0 > 0).astype(F32)
    buf_a[0:POOL_HALO, :] = halo_ref[...] * keep
    buf_a[POOL_HALO:POOL_HALO + tm, :] = x
    tot = tm + POOL_HALO
    src, dst = buf_a, buf_b
    levels = {}
    start = 0
    for w in (1, 2, 4, 8):
        start += 8
        cur = src[start:tot, :] + src[start - w:tot - w, :]
        dst[start:tot, :] = cur
        levels[2 * w] = cur[POOL_HALO - start:, :]
        src, dst = dst, src
    lane = lax.broadcasted_iota(jnp.int32, (tm, W_A), 1)
    grp = lane // POOL_GC
    win = jnp.where(grp == 0, levels[2], jnp.where(grp == 1, levels[4],
                    jnp.where(grp == 2, levels[8], levels[16])))
    wlane = jnp.where(grp == 0, 2, jnp.where(grp == 1, 4, jnp.where(grp == 2, 8, 16)))
    tpos = t0 + lax.broadcasted_iota(jnp.int32, (tm, W_A), 0)
    cnt = jnp.minimum(tpos + 1, wlane).astype(F32)
    pooled = win / cnt - x
    a = jnp.dot(pooled.astype(BF16), pw_ref[...], preferred_element_type=F32) * ps_ref[...]
    return (a * _silu(gate_ref[...].astype(F32))).astype(BF16)


def _out_tail(h_ref, mix_a, mb_ref, wo_ref, wg_ref, p_ref, wp_ref, o_ref):
    wa = mix_a.shape[1]
    m = (jnp.dot(mix_a, wo_ref[0:wa, :], preferred_element_type=F32)
         + jnp.dot(mb_ref[...], wo_ref[wa:, :], preferred_element_type=F32))
    h1 = h_ref[...] + m
    gate = _sigmoid(jnp.dot(h1.astype(BF16), wg_ref[...], preferred_element_type=F32))
    pp = jnp.dot(p_ref[...].astype(BF16), wp_ref[...], preferred_element_type=F32)
    o_ref[...] = h1 + gate * pp


def _out_kernel(h_ref, ma_ref, mb_ref, wo_ref, wg_ref, p_ref, wp_ref, o_ref):
    _out_tail(h_ref, ma_ref[...], mb_ref, wo_ref, wg_ref, p_ref, wp_ref, o_ref)


def _even_out_kernel(h_ref, x_ref, halo_ref, gate_ref, pw_ref, ps_ref, mb_ref, wo_ref, wg_ref, p_ref, wp_ref,
                     o_ref, buf_a, buf_b, *, tm, seq):
    mix_a = _pool_values(x_ref, halo_ref, gate_ref, pw_ref, ps_ref, buf_a, buf_b, tm, seq)
    _out_tail(h_ref, mix_a, mb_ref, wo_ref, wg_ref, p_ref, wp_ref, o_ref)


def _even_out_proj(h, a_in, a_gate, pool_w_bd, pool_scale, mix_b, w_out, w_gate, p, layer, w_proj, seq):
    n = h.shape[0]
    tm = min(ROW_TILE, seq)
    hb = tm // POOL_HALO
    row = lambda w: pl.BlockSpec((tm, w), lambda i: (i, 0))
    halo = pl.BlockSpec((POOL_HALO, W_A), lambda i: (jnp.maximum(i * hb - 1, 0), 0))
    full = lambda a: pl.BlockSpec(a.shape, lambda i: (0,) * a.ndim)
    return pl.pallas_call(
        functools.partial(_even_out_kernel, tm=tm, seq=seq),
        grid=(n // tm,),
        in_specs=[row(D_MODEL), row(W_A), halo, row(W_A), full(pool_w_bd), full(pool_scale),
                  row(mix_b.shape[1]), full(w_out), full(w_gate),
                  pl.BlockSpec((None, tm, PLE_DIM), lambda i: (layer, i, 0)), full(w_proj)],
        out_specs=row(D_MODEL),
        out_shape=jax.ShapeDtypeStruct((n, D_MODEL), F32),
        scratch_shapes=[pltpu.VMEM((tm + POOL_HALO, W_A), F32)] * 2,
        compiler_params=_cparams(("parallel",)),
        name="pool_out_proj_ple",
    )(h, a_in, a_in, a_gate, pool_w_bd, pool_scale, mix_b, w_out, w_gate, p, w_proj)


def _out_proj(h, mix_a, mix_b, w_out, w_gate, p, layer, w_proj):
    n = h.shape[0]
    tm = min(ROW_TILE, n)
    wa = mix_a.shape[1]
    row = lambda w: pl.BlockSpec((tm, w), lambda i: (i, 0))
    full = lambda a: pl.BlockSpec(a.shape, lambda i: (0,) * a.ndim)
    return pl.pallas_call(
        _out_kernel,
        grid=(n // tm,),
        in_specs=[row(D_MODEL), row(wa), row(mix_b.shape[1]), full(w_out), full(w_gate),
                  pl.BlockSpec((None, tm, PLE_DIM), lambda i: (layer, i, 0)), full(w_proj)],
        out_specs=row(D_MODEL),
        out_shape=jax.ShapeDtypeStruct((n, D_MODEL), F32),
        compiler_params=_cparams(("parallel",)),
        name="out_proj_ple",
    )(h, mix_a, mix_b, w_out, w_gate, p, w_proj)


O_CQ = 0
O_CKV = O_CQ + Q_LORA
O_KR = O_CKV + KV_LORA
O_CG = O_KR + LANES
O_QKV = O_CG + W_C
O_DG = O_QKV + CONV_CH
O_BA = O_DG + W_D
O_TOT = O_BA + LANES
HEAD_PAD = 2 * LANES


def _odd_in_kernel(h_ref, g_ref, w_ref, qan_ref, wuq_ref, kvan_ref, wukv_ref, qn_ref, kn_ref,
                   c_ref, s_ref, alog_ref, dtb_ref, cw_ref,
                   q_ref, k_ref, v_ref, cg_ref, qkv_ref, dg_ref, gb_ref, cbuf, *, tm, seq):
    @pl.when((pl.program_id(0) * tm) % seq == 0)
    def _():
        cbuf[0:CONV_PAD, :] = jnp.zeros((CONV_PAD, CONV_CH), F32)

    cw = cw_ref[...]
    x = h_ref[...]
    ms = jnp.mean(x * x, axis=-1, keepdims=True)
    yb = (x * lax.rsqrt(ms + EPS) * g_ref[...]).astype(BF16)
    c_all = c_ref[...]
    s_all = s_ref[...]
    qn = qn_ref[...]
    kn = kn_ref[...]
    scale = DQK_C ** -0.5 * LOG2E

    def project(rs):
        def seg(lo, hi):
            return jnp.dot(yb[rs, :], w_ref[:, lo:hi], preferred_element_type=F32)

        cq = seg(O_CQ, O_CKV)
        ckv = seg(O_CKV, O_KR)
        kr = seg(O_KR, O_CG)
        z_cg = seg(O_CG, O_QKV)
        z_qkv = seg(O_QKV, O_DG)
        z_dg = seg(O_DG, O_BA)
        ba = seg(O_BA, O_TOT)
        cqn = cq * lax.rsqrt(jnp.mean(cq * cq, axis=-1, keepdims=True) + EPS) * qan_ref[...]
        ckvn = ckv * lax.rsqrt(jnp.mean(ckv * ckv, axis=-1, keepdims=True) + EPS) * kvan_ref[...]
        qu = jnp.dot(cqn.astype(BF16), wuq_ref[...], preferred_element_type=F32)
        kvu = jnp.dot(ckvn.astype(BF16), wukv_ref[...], preferred_element_type=F32)
        return kr, z_cg, z_qkv, z_dg, ba, qu, kvu

    def epilogue(rs, zs):
        kr, z_cg, z_qkv, z_dg, ba, qu, kvu = zs
        c = c_all[rs, :]
        s = s_all[rs, :]
        cg_ref[rs, :] = z_cg.astype(BF16)
        dg_ref[rs, :] = z_dg.astype(BF16)
        lo = CONV_PAD + rs.start
        nrow = rs.stop - rs.start
        cbuf[lo:lo + nrow, :] = z_qkv
        ext = cbuf[lo - CONV_PAD:lo + nrow, :]
        y = z_qkv * cw[CONV_K - 1:CONV_K, :]
        for d in range(1, CONV_K):
            shifted = pltpu.roll(ext, d, 0)[CONV_PAD:, :]
            y = y + shifted * cw[CONV_K - 1 - d:CONV_K - d, :]
        qkv_ref[rs, :] = _silu(y).astype(BF16)
        v_ref[rs, :] = kvu[:, H_C * D_NOPE:].astype(BF16)
        beta = _sigmoid(ba)
        g = -jnp.exp(alog_ref[...]) * jax.nn.softplus(ba + dtb_ref[...])
        lane = lax.broadcasted_iota(jnp.int32, ba.shape, 1)
        gb_ref[rs, :] = jnp.where(lane < H_D, beta, g)
        for hh in range(H_C):
            nope = qu[:, hh * D_NOPE:(hh + 1) * D_NOPE]
            rope = qu[:, H_C * D_NOPE + hh * LANES:H_C * D_NOPE + (hh + 1) * LANES]
            ss = jnp.sum(nope * nope + rope * rope, axis=-1, keepdims=True)
            r = lax.rsqrt(ss * (1.0 / DQK_C) + EPS) * scale
            q_ref[rs, hh * HEAD_PAD:hh * HEAD_PAD + LANES] = (nope * r * qn[:, 0:LANES]).astype(BF16)
            q_ref[rs, hh * HEAD_PAD + LANES:(hh + 1) * HEAD_PAD] = _rope_block(
                rope * r * qn[:, LANES:2 * LANES], c, s, D_ROPE // 2).astype(BF16)
        kr_ss = jnp.sum(kr * kr, axis=-1, keepdims=True)
        kr_rot = _rope_block(kr * kn[:, LANES:2 * LANES], c, s, D_ROPE // 2)
        for hh in range(H_C):
            nope = kvu[:, hh * D_NOPE:(hh + 1) * D_NOPE]
            ss = jnp.sum(nope * nope, axis=-1, keepdims=True) + kr_ss
            r = lax.rsqrt(ss * (1.0 / DQK_C) + EPS)
            k_ref[rs, hh * HEAD_PAD:hh * HEAD_PAD + LANES] = (nope * r * kn[:, 0:LANES]).astype(BF16)
            k_ref[rs, hh * HEAD_PAD + LANES:(hh + 1) * HEAD_PAD] = (kr_rot * r).astype(BF16)

    part = x.shape[0] // IN_PROJ_PARTS
    parts = [slice(k * part, (k + 1) * part) for k in range(IN_PROJ_PARTS)]
    z_prev = project(parts[0])
    for k in range(1, IN_PROJ_PARTS):
        z_next = project(parts[k])
        epilogue(parts[k - 1], z_prev)
        z_prev = z_next
    epilogue(parts[-1], z_prev)
    cbuf[0:CONV_PAD, :] = cbuf[tm:tm + CONV_PAD, :]


def _odd_in(h, g, w_in, qan, wuq, kvan, wukv, qn, kn, c_tab, s_tab, alog, dtb, conv_w, seq):
    n = h.shape[0]
    tm = min(ROW_TILE, seq)
    row = lambda w: pl.BlockSpec((tm, w), lambda i: (i, 0))
    full = lambda a: pl.BlockSpec(a.shape, lambda i: (0,) * a.ndim)
    sds = jax.ShapeDtypeStruct
    return pl.pallas_call(
        functools.partial(_odd_in_kernel, tm=tm, seq=seq),
        grid=(n // tm,),
        in_specs=[row(D_MODEL), full(g), full(w_in), full(qan), full(wuq), full(kvan), full(wukv),
                  full(qn), full(kn), row(LANES), row(LANES), full(alog), full(dtb), full(conv_w)],
        out_specs=[row(H_C * HEAD_PAD), row(H_C * HEAD_PAD), row(W_C), row(W_C), row(CONV_CH), row(W_D),
                   row(LANES)],
        out_shape=[sds((n, H_C * HEAD_PAD), BF16), sds((n, H_C * HEAD_PAD), BF16), sds((n, W_C), BF16),
                   sds((n, W_C), BF16), sds((n, CONV_CH), BF16), sds((n, W_D), BF16), sds((n, LANES), F32)],
        scratch_shapes=[pltpu.VMEM((tm + CONV_PAD, CONV_CH), F32)],
        compiler_params=_cparams(("arbitrary",)),
        name="odd_in_proj",
    )(h, g, w_in, qan, wuq, kvan, wukv, qn, kn, c_tab, s_tab, alog, dtb, conv_w)


def _gdn_prep_kernel(y_ref, gbc_ref, lhs1_ref, lhs2_ref, u_ref, gl_ref, *, tm):
    ri = lax.broadcasted_iota(jnp.int32, (tm, tm), 0)
    ci = lax.broadcasted_iota(jnp.int32, (tm, tm), 1)
    low = jnp.where(((ri // CHUNK) == (ci // CHUNK)) & (ri >= ci), 1.0, 0.0).astype(F32)
    i2 = lax.broadcasted_iota(jnp.int32, (LANES, LANES), 0)
    j2 = lax.broadcasted_iota(jnp.int32, (LANES, LANES), 1)
    eye_l = jnp.where(i2 == j2, 1.0, 0.0).astype(BF16)
    gbc = gbc_ref[...]
    gcs = _split3_dot(low, gbc)
    gcs_t = _split3_dot_nt(eye_l, gcs)

    pdim = H_D * CHUNK
    r4 = lax.broadcasted_iota(jnp.int32, (pdim, pdim), 0)
    c4 = lax.broadcasted_iota(jnp.int32, (pdim, pdim), 1)
    same = (r4 // CHUNK) == (c4 // CHUNK)
    incl = same & (r4 >= c4)
    strict = same & (r4 > c4)
    eye = jnp.where(r4 == c4, 1.0, 0.0).astype(F32)
    zblk = jnp.zeros((CHUNK, DK_D), F32)

    def block_diag(blocks):
        return jnp.concatenate(
            [jnp.concatenate([blocks[h] if j == h else zblk for j in range(H_D)], axis=1) for h in range(H_D)],
            axis=0)

    chunks = list(range(tm // CHUNK))
    heads = list(range(H_D))

    def load_qkv(c):
        r0 = c * CHUNK
        qs, ks, vs = [], [], []
        for hh in heads:
            l0 = hh * DK_D
            qh = y_ref[r0:r0 + CHUNK, l0:l0 + DK_D].astype(F32)
            kh = y_ref[r0:r0 + CHUNK, H_D * DK_D + l0:H_D * DK_D + l0 + DK_D].astype(F32)
            vs.append(y_ref[r0:r0 + CHUNK, 2 * H_D * DK_D + l0:2 * H_D * DK_D + l0 + DV_D].astype(F32))
            qs.append(qh * lax.rsqrt(jnp.sum(qh * qh, axis=-1, keepdims=True) + EPS) * (DK_D ** -0.5))
            ks.append(kh * lax.rsqrt(jnp.sum(kh * kh, axis=-1, keepdims=True) + EPS))
        return qs, ks, vs

    def decay_terms(c):
        r0 = c * CHUNK
        beta = jnp.concatenate([gbc[r0:r0 + CHUNK, hh:hh + 1] for hh in heads], axis=0)
        gcol = jnp.concatenate([gcs[r0:r0 + CHUNK, H_D + hh:H_D + hh + 1] for hh in heads], axis=0)
        grow = jnp.concatenate([gcs_t[H_D + hh:H_D + hh + 1, r0:r0 + CHUNK] for hh in heads], axis=1)
        glasts = [gcs[r0 + CHUNK - 1:r0 + CHUNK, H_D + hh:H_D + hh + 1] for hh in heads]
        glast = jnp.concatenate([jnp.broadcast_to(gl, (CHUNK, 1)) for gl in glasts], axis=0)
        gamma = jnp.exp(jnp.where(incl, gcol - grow, -jnp.inf))
        return beta, gcol, glasts, glast, gamma

    qkv = [load_qkv(c) for c in chunks]
    q_st = [jnp.concatenate(x[0], axis=0) for x in qkv]
    k_st = [jnp.concatenate(x[1], axis=0) for x in qkv]
    v_st = [jnp.concatenate(x[2], axis=0) for x in qkv]
    k_bd = [block_diag(x[1]).astype(BF16) for x in qkv]
    q_bd = [block_diag(x[0]).astype(BF16) for x in qkv]
    qkkk = [_bdot_nt(jnp.concatenate([qb, kb], axis=0), kb) for qb, kb in zip(q_bd, k_bd)]
    dec = [decay_terms(c) for c in chunks]
    a = [jnp.where(strict, x[pdim:, :] * d[4] * d[0], 0.0) for x, d in zip(qkkk, dec)]
    t = [eye - x for x in a]
    pw = a
    for _ in range(5):
        pw = [_bdot(x, x) for x in pw]
        t = [x + _bdot(x, y) for x, y in zip(t, pw)]
    egc = [jnp.exp(d[1]) for d in dec]
    uw = [_bdot(tt, jnp.concatenate([v * d[0], k * (d[0] * e)], axis=1))
          for tt, v, k, d, e in zip(t, v_st, k_st, dec, egc)]
    kdt = [_bdot_nt(eye_l, k * jnp.exp(d[3] - d[1])).astype(BF16) for k, d in zip(k_st, dec)]
    for c in chunks:
        qkg = (qkkk[c][0:pdim, :] * dec[c][4]).astype(BF16)
        q_dec = (q_st[c] * egc[c]).astype(BF16)
        for hh in heads:
            idx = c * H_D + hh
            hrows = slice(hh * CHUNK, (hh + 1) * CHUNK)
            lhs1_ref[idx, 0:CHUNK, :] = uw[c][hrows, DV_D:].astype(BF16)
            lhs1_ref[idx, CHUNK:2 * CHUNK, :] = q_dec[hrows, :]
            u_ref[idx] = uw[c][hrows, 0:DV_D].astype(BF16)
            gl_ref[c, hh:hh + 1, :] = jnp.broadcast_to(jnp.exp(dec[c][2][hh]), (1, LANES))
        for pp in range(H_D // 2):
            idx2 = c * (H_D // 2) + pp
            lhs2_ref[idx2, 0:LANES, :] = qkg[pp * LANES:(pp + 1) * LANES, pp * LANES:(pp + 1) * LANES]
            lhs2_ref[idx2, LANES:2 * LANES, :] = kdt[c][:, pp * LANES:(pp + 1) * LANES]


def _gdn_prep(y, gb, seq):
    n = y.shape[0]
    tm = min(GDN_PREP_TILE, seq)
    nch = tm // CHUNK
    row = lambda w: pl.BlockSpec((tm, w), lambda i: (i, 0))
    ch3 = lambda m, r, w: pl.BlockSpec((nch * m, r, w), lambda i: (i, 0, 0))
    sds = jax.ShapeDtypeStruct
    nc = n // CHUNK
    return pl.pallas_call(
        functools.partial(_gdn_prep_kernel, tm=tm),
        grid=(n // tm,),
        in_specs=[row(CONV_CH), row(LANES)],
        out_specs=[ch3(H_D, 2 * CHUNK, DK_D), ch3(H_D // 2, 2 * LANES, LANES), ch3(H_D, CHUNK, DV_D),
                   pl.BlockSpec((nch, H_D, LANES), lambda i: (i, 0, 0))],
        out_shape=[sds((nc * H_D, 2 * CHUNK, DK_D), BF16), sds((nc * H_D // 2, 2 * LANES, LANES), BF16),
                   sds((nc * H_D, CHUNK, DV_D), BF16), sds((nc, H_D, LANES), F32)],
        compiler_params=_cparams(("parallel",)),
        name="gdn_chunk_prep",
    )(y, gb)


def _gdn_scan_kernel(lhs1_ref, lhs2_ref, u_ref, gl_ref, dg_ref, on_ref, o_ref, s_scr, *, nb, tb):
    @pl.when(pl.program_id(0) == 0)
    def _():
        s_scr[...] = jnp.zeros(s_scr.shape, F32)

    on = on_ref[...]
    zb = jnp.zeros((2 * CHUNK, DK_D), BF16)
    left = lax.broadcasted_iota(jnp.int32, (DK_D, LANES), 1) < CHUNK

    def finish(o, b, rows, hh):
        ms = jnp.mean(o * o, axis=-1, keepdims=True)
        o = o * lax.rsqrt(ms + EPS) * on
        gate = dg_ref[b, rows, hh * DV_D:(hh + 1) * DV_D].astype(F32)
        o_ref[b, rows, hh * DV_D:(hh + 1) * DV_D] = (o * _silu(gate)).astype(BF16)

    def chunk_body(c, carry):
        rows = pl.ds(pl.multiple_of(c * CHUNK, CHUNK), CHUNK)
        chains = [(b, 2 * pp) for b in range(nb) for pp in range(H_D // 2)]
        st = [(s_scr[b * H_D + h0], s_scr[b * H_D + h0 + 1]) for b, h0 in chains]
        r1 = [jnp.dot(jnp.concatenate([jnp.concatenate([lhs1_ref[b, c * H_D + h0], zb], axis=1),
                                       jnp.concatenate([zb, lhs1_ref[b, c * H_D + h0 + 1]], axis=1)], axis=0),
                      jnp.concatenate([s0, s1], axis=0).astype(BF16), preferred_element_type=F32)
              for (b, h0), (s0, s1) in zip(chains, st)]
        v2 = [jnp.concatenate([u_ref[b, c * H_D + h0].astype(F32) - x[0:CHUNK, :],
                               u_ref[b, c * H_D + h0 + 1].astype(F32) - x[2 * CHUNK:3 * CHUNK, :]],
                              axis=0).astype(BF16) for (b, h0), x in zip(chains, r1)]
        r2 = []
        for (b, h0), v in zip(chains, v2):
            blk = lhs2_ref[b, c * (H_D // 2) + h0 // 2]
            kd = blk[LANES:2 * LANES, :]
            zk = jnp.zeros_like(kd)
            l2 = jnp.concatenate([blk[0:LANES, :], jnp.where(left, kd, zk), jnp.where(left, zk, kd)], axis=0)
            r2.append(jnp.dot(l2, v, preferred_element_type=F32))
        for (b, h0), (s0, s1), x, y in zip(chains, st, r1, r2):
            s_scr[b * H_D + h0] = s0 * gl_ref[b, c, h0:h0 + 1, :] + y[LANES:LANES + DK_D, :]
            s_scr[b * H_D + h0 + 1] = s1 * gl_ref[b, c, h0 + 1:h0 + 2, :] + y[LANES + DK_D:LANES + 2 * DK_D, :]
            finish(x[CHUNK:2 * CHUNK, :] + y[0:CHUNK, :], b, rows, h0)
            finish(x[3 * CHUNK:4 * CHUNK, :] + y[CHUNK:2 * CHUNK, :], b, rows, h0 + 1)
        return carry

    lax.fori_loop(0, tb // CHUNK, chunk_body, 0)


def _gdn_scan(lhs1, lhs2, u, gl, d_gate, o_norm, batch, seq):
    tb = min(GDN_SCAN_TILE, seq)
    nch = tb // CHUNK
    ncb = seq // CHUNK
    lhs1 = lhs1.reshape(batch, ncb * H_D, 2 * CHUNK, DK_D)
    lhs2 = lhs2.reshape(batch, ncb * H_D // 2, 2 * LANES, LANES)
    u = u.reshape(batch, ncb * H_D, CHUNK, DV_D)
    gl = gl.reshape(batch, ncb, H_D, LANES)
    d_gate = d_gate.reshape(batch, seq, W_D)
    ch4 = lambda r, w: pl.BlockSpec((batch, nch * H_D, r, w), lambda i: (0, i, 0, 0))
    tok = pl.BlockSpec((batch, tb, W_D), lambda i: (0, i, 0))
    out = pl.pallas_call(
        functools.partial(_gdn_scan_kernel, nb=batch, tb=tb),
        grid=(seq // tb,),
        in_specs=[ch4(2 * CHUNK, DK_D),
                  pl.BlockSpec((batch, nch * H_D // 2, 2 * LANES, LANES), lambda i: (0, i, 0, 0)),
                  ch4(CHUNK, DV_D),
                  pl.BlockSpec((batch, nch, H_D, LANES), lambda i: (0, i, 0, 0)), tok,
                  pl.BlockSpec(o_norm.shape, lambda i: (0, 0))],
        out_specs=tok,
        out_shape=jax.ShapeDtypeStruct((batch, seq, W_D), BF16),
        scratch_shapes=[pltpu.VMEM((batch * H_D, DK_D, DV_D), F32)],
        compiler_params=_cparams(("arbitrary",)),
        name="gdn_state_scan",
    )(lhs1, lhs2, u, gl, d_gate, o_norm)
    return out.reshape(batch * seq, W_D)


def _pad_cols(w, width):
    return jnp.pad(w, ((0, 0), (0, width - w.shape[1])))


def _odd_w_in_layout(w):
    cq, ckv, kr, cg, qkv, db, da, dg = jnp.split(
        w, np.cumsum((Q_LORA, KV_LORA, D_ROPE, W_C, CONV_CH, H_D, H_D, W_D))[:-1].tolist(), axis=1)
    return jnp.concatenate([cq, ckv, _pad_cols(kr, LANES), cg, qkv, dg,
                            _pad_cols(jnp.concatenate([db, da], axis=1), LANES)], axis=1)


def _wuq_layout(w):
    w = w.reshape(Q_LORA, H_C, DQK_C)
    nope = w[:, :, :D_NOPE].reshape(Q_LORA, H_C * D_NOPE)
    rope = jnp.pad(w[:, :, D_NOPE:], ((0, 0), (0, 0), (0, LANES - D_ROPE))).reshape(Q_LORA, H_C * LANES)
    return jnp.concatenate([nope, rope], axis=1)


def _wukv_layout(w):
    w = w.reshape(KV_LORA, H_C, D_NOPE + DV_C)
    return jnp.concatenate([w[:, :, :D_NOPE].reshape(KV_LORA, H_C * D_NOPE),
                            w[:, :, D_NOPE:].reshape(KV_LORA, H_C * DV_C)], axis=1)


def _head_gain_layout(g):
    return jnp.pad(g, (0, HEAD_PAD - DQK_C))[None, :].astype(F32)


def kernel(x, p, positions, norm_g, ple_w_gate, ple_w_proj, ev_w_in, ev_pool_w, ev_pool_scale, ev_q_norm, ev_k_norm, ev_lambda, ev_subln, ev_w_out, od_w_in, od_q_a_norm, od_w_uq, od_kv_a_norm, od_w_ukv, od_q_norm, od_k_norm, od_conv_w, od_a_log, od_dt_bias, od_o_norm, od_w_out):
    batch, seq, _ = x.shape
    depth = p.shape[0]
    n = batch * seq
    h = x.reshape(n, D_MODEL)
    pos = positions.reshape(n)
    ce, se = _rope_tables(pos, B_ROT, ROPE_THETA, DK_B)
    co, so = _rope_tables(pos, D_ROPE, MLA_THETA, LANES)


    p_all = p.reshape(depth, n, PLE_DIM)
    for i in range(depth):
        j = i // 2
        g = norm_g[i][None, :]
        w_gate = ple_w_gate[i].astype(BF16)
        w_proj = ple_w_proj[i].astype(BF16)
        if i % 2 == 0:
            w_in = ev_w_in[j].astype(BF16)
            qg = (jnp.tile(ev_q_norm[j], 2 * H_B) * (DK_B ** -0.5 * LOG2E))[None, :]
            kg = jnp.tile(ev_k_norm[j], 2 * H_B)[None, :]
            a_in, a_gate, q, k, v, b_gate = _even_in(h, g, w_in, qg, kg, ce, se)
            lam_init = 0.8 - 0.6 * math.exp(-0.3 * i)
            ob = _diff_attention(q, k, v, b_gate, ev_lambda[j], ev_subln[j][None, :], batch, seq, lam_init)
            pw = jax.scipy.linalg.block_diag(*[ev_pool_w[j][gi] for gi in range(len(POOL_WINDOWS))]).astype(BF16)
            h = _even_out_proj(h, a_in, a_gate, pw, ev_pool_scale[j][None, :], ob, ev_w_out[j].astype(BF16),
                               w_gate, p_all, i, w_proj, seq)
        else:
            w_in = _odd_w_in_layout(od_w_in[j]).astype(BF16)
            alog = jnp.zeros((LANES,), F32).at[H_D:2 * H_D].set(od_a_log[j])[None, :]
            dtb = jnp.zeros((LANES,), F32).at[H_D:2 * H_D].set(od_dt_bias[j])[None, :]
            q, k, v, c_gate, qkv, d_gate, gb = _odd_in(
                h, g, w_in, od_q_a_norm[j][None, :], _wuq_layout(od_w_uq[j]).astype(BF16),
                od_kv_a_norm[j][None, :], _wukv_layout(od_w_ukv[j]).astype(BF16),
                _head_gain_layout(od_q_norm[j]), _head_gain_layout(od_k_norm[j]), co, so, alog, dtb,
                od_conv_w[j], seq)
            oa = _mla_attention(q, k, v, c_gate, batch, seq)
            lhs1, lhs2, u, gl = _gdn_prep(qkv, gb, seq)
            ob = _gdn_scan(lhs1, lhs2, u, gl, d_gate, od_o_norm[j][None, :], batch, seq)
            h = _out_proj(h, oa, ob, od_w_out[j].astype(BF16), w_gate, p_all, i, w_proj)
    return h.reshape(batch, seq, D_MODEL)
```

```python
import functools
import math

import numpy as np
import jax
import jax.numpy as jnp
from jax import lax
from jax.experimental import pallas as pl
from jax.experimental.pallas import tpu as pltpu

F32 = jnp.float32
BF16 = jnp.bfloat16

D_MODEL = 1024
PLE_DIM = 256
EPS = 1e-6
W_A = 256
POOL_WINDOWS = (2, 4, 8, 16)
POOL_GC = 64
W_B = 768
DV_B = 128
H_B = 6
DK_B = 64
B_ROT = 16
ROPE_THETA = 500000.0
W_C = 512
DV_C = 128
H_C = 4
D_NOPE = 128
D_ROPE = 64
DQK_C = 192
Q_LORA = 256
KV_LORA = 128
MLA_THETA = 10000.0
W_D = 512
DK_D = 128
DV_D = 128
H_D = 4
CONV_K = 4
CHUNK = 64
CONV_CH = H_D * (2 * DK_D + DV_D)

LANES = 128
VMEM_LIMIT = 48 * 1024 * 1024
NEG = -1e30
LOG2E = math.log2(math.e)

ROW_TILE = 512
IN_PROJ_PARTS = 2
DIFF_TK = 1024
MLA_TK = 1024
DIAG_TK = 512
POOL_HALO = 32
CONV_PAD = 8
GDN_PREP_TILE = 256
GDN_SCAN_TILE = 256
DIFF_HEADS_PER_STEP = 2
MLA_HEADS_PER_STEP = 2
DIFF_TQ = 512
MLA_TQ = 1024


def _cparams(sem):
    return pltpu.CompilerParams(dimension_semantics=sem, vmem_limit_bytes=VMEM_LIMIT)


def _silu(x):
    return x * (1.0 / (1.0 + jnp.exp(-x)))


def _sigmoid(x):
    return 1.0 / (1.0 + jnp.exp(-x))


def _bdot(a, b):
    return jnp.dot(a.astype(BF16), b.astype(BF16), preferred_element_type=F32)


def _bdot_nt(a, b):
    return lax.dot_general(a.astype(BF16), b.astype(BF16), (((1,), (1,)), ((), ())),
                           preferred_element_type=F32)


def _split3_dot(a_exact, b):
    b0 = b.astype(BF16)
    r1 = b - b0.astype(F32)
    b1 = r1.astype(BF16)
    b2 = (r1 - b1.astype(F32)).astype(BF16)
    a = a_exact.astype(BF16)
    return (jnp.dot(a, b0, preferred_element_type=F32) + jnp.dot(a, b1, preferred_element_type=F32)
            + jnp.dot(a, b2, preferred_element_type=F32))


def _split3_dot_nt(a_exact, b):
    b0 = b.astype(BF16)
    r1 = b - b0.astype(F32)
    b1 = r1.astype(BF16)
    b2 = (r1 - b1.astype(F32)).astype(BF16)
    return _bdot_nt(a_exact, b0) + _bdot_nt(a_exact, b1) + _bdot_nt(a_exact, b2)


def _rope_table_kernel(pos_ref, inv_ref, sgn_ref, c_ref, s_ref):
    ang = pos_ref[...] * inv_ref[...]
    c_ref[...] = jnp.cos(ang)
    s_ref[...] = jnp.sin(ang) * sgn_ref[...]


def _rope_tables(pos_b, inv, sgn):
    n = pos_b.shape[0]
    tm = min(1024, n)
    row = pl.BlockSpec((tm, LANES), lambda i: (i, 0))
    par = pl.BlockSpec((1, LANES), lambda i: (0, 0))
    return pl.pallas_call(
        _rope_table_kernel,
        grid=(n // tm,),
        in_specs=[row, par, par],
        out_specs=[row, row],
        out_shape=[jax.ShapeDtypeStruct((n, LANES), F32)] * 2,
        compiler_params=_cparams(("parallel",)),
        name="rope_tables",
    )(pos_b, inv, sgn)


def _rope_block(x, c, s, half):
    fwd = pltpu.roll(x, LANES - half, 1)
    bwd = pltpu.roll(x, half, 1)
    lane = lax.broadcasted_iota(jnp.int32, x.shape, 1)
    rot = jnp.where((lane % (2 * half)) < half, fwd, bwd)
    return x * c + rot * s


def _even_in_kernel(h_ref, g_ref, w_ref, qg_ref, kg_ref, c_ref, s_ref,
                    ain_ref, agate_ref, q_ref, k_ref, v_ref, bgate_ref):
    x = h_ref[...]
    ms = jnp.mean(x * x, axis=-1, keepdims=True)
    yb = (x * lax.rsqrt(ms + EPS) * g_ref[...]).astype(BF16)

    c_all = c_ref[...]
    s_all = s_ref[...]
    o = 2 * W_A
    bounds = [(0, W_A), (W_A, 2 * W_A), (o, o + W_B), (o + W_B, o + 2 * W_B), (o + 2 * W_B, o + 3 * W_B),
              (o + 3 * W_B, o + 4 * W_B)]

    def project(rs):
        return [jnp.dot(yb[rs, :], w_ref[:, lo:hi], preferred_element_type=F32) for lo, hi in bounds]

    def qk_prep(rs, z, gain_ref, out_ref, c, s):
        gain = gain_ref[...]
        first = lax.broadcasted_iota(jnp.int32, (z.shape[0], LANES), 1) < DK_B
        for j in range(W_B // LANES):
            blk = z[:, j * LANES:(j + 1) * LANES]
            sq = blk * blk
            ss0 = jnp.sum(jnp.where(first, sq, 0.0), axis=-1, keepdims=True)
            ss1 = jnp.sum(jnp.where(first, 0.0, sq), axis=-1, keepdims=True)
            r = jnp.where(first, lax.rsqrt(ss0 * (1.0 / DK_B) + EPS), lax.rsqrt(ss1 * (1.0 / DK_B) + EPS))
            zn = blk * r * gain[:, j * LANES:(j + 1) * LANES]
            out_ref[rs, j * LANES:(j + 1) * LANES] = _rope_block(zn, c, s, B_ROT // 2).astype(BF16)

    def epilogue(rs, zs):
        z_ain, z_agate, z_q, z_k, z_v, z_bg = zs
        ain_ref[rs, :] = z_ain
        agate_ref[rs, :] = z_agate.astype(BF16)
        v_ref[rs, :] = z_v.astype(BF16)
        bgate_ref[rs, :] = z_bg.astype(BF16)
        qk_prep(rs, z_q, qg_ref, q_ref, c_all[rs, :], s_all[rs, :])
        qk_prep(rs, z_k, kg_ref, k_ref, c_all[rs, :], s_all[rs, :])

    part = x.shape[0] // IN_PROJ_PARTS
    parts = [slice(k * part, (k + 1) * part) for k in range(IN_PROJ_PARTS)]
    z_prev = project(parts[0])
    for k in range(1, IN_PROJ_PARTS):
        z_next = project(parts[k])
        epilogue(parts[k - 1], z_prev)
        z_prev = z_next
    epilogue(parts[-1], z_prev)


def _even_in(h, g, w_in, qg, kg, c_tab, s_tab):
    n = h.shape[0]
    tm = min(ROW_TILE, n)
    row = lambda w: pl.BlockSpec((tm, w), lambda i: (i, 0))
    full = lambda a: pl.BlockSpec(a.shape, lambda i: (0,) * a.ndim)
    sds = jax.ShapeDtypeStruct
    return pl.pallas_call(
        _even_in_kernel,
        grid=(n // tm,),
        in_specs=[row(D_MODEL), full(g), full(w_in), full(qg), full(kg), row(LANES), row(LANES)],
        out_specs=[row(W_A), row(W_A), row(W_B), row(W_B), row(W_B), row(W_B)],
        out_shape=[sds((n, W_A), F32), sds((n, W_A), BF16), sds((n, W_B), BF16), sds((n, W_B), BF16),
                   sds((n, W_B), BF16), sds((n, W_B), BF16)],
        compiler_params=_cparams(("parallel",)),
        name="even_in_proj",
    )(h, g, w_in, qg, kg, c_tab, s_tab)


def _flash_loop(qs_ref, k_ref, v_ref, m_scr, l_scr, acc_scr, qi, tq, tk, heads, dqk, dv):
    q0 = qi * tq
    rows = qs_ref.shape[1]
    tkt = min(tq, tk, DIAG_TK)
    n_main = q0 // tk
    base = n_main * tk
    n_tail_full = (q0 - base) // tkt
    n_tail_diag = tq // tkt
    assert n_tail_diag == 1 or rows == tq
    m_scr[...] = jnp.full(m_scr.shape, NEG, F32)
    l_scr[...] = jnp.zeros(l_scr.shape, F32)
    acc_scr[...] = jnp.zeros(acc_scr.shape, F32)

    def step(ks, tk, masked, row_lo=0):
        hs = range(heads)
        rs = slice(row_lo, rows)
        s = [lax.dot_general(qs_ref[g, rs, :], k_ref[pl.ds(ks, tk), g * dqk:(g + 1) * dqk],
                             (((1,), (1,)), ((), ())), preferred_element_type=F32) for g in hs]
        if masked:
            row = q0 + (row_lo + lax.broadcasted_iota(jnp.int32, s[0].shape, 0)) % tq
            col = ks + lax.broadcasted_iota(jnp.int32, s[0].shape, 1)
            s = [jnp.where(col <= row, x, NEG) for x in s]
        m_prev = [m_scr[g, rs, :] for g in hs]
        m_new = [jnp.maximum(mp, jnp.max(x, axis=1, keepdims=True)) for mp, x in zip(m_prev, s)]
        alpha = [jnp.exp2(mp - mn) for mp, mn in zip(m_prev, m_new)]
        p = [jnp.exp2(x - jnp.tile(mn, (1, tk // LANES))) for x, mn in zip(s, m_new)]
        ones = jnp.ones((tk, LANES), BF16)
        pv = [jnp.dot(p[g].astype(BF16),
                      jnp.concatenate([v_ref[pl.ds(ks, tk), g * dv:(g + 1) * dv], ones], axis=1),
                      preferred_element_type=F32) for g in hs]
        for g in hs:
            m_scr[g, rs, :] = m_new[g]
            l_scr[g, rs, :] = alpha[g] * l_scr[g, rs, :] + pv[g][:, dv:]
            acc_scr[g, rs, :] = alpha[g] * acc_scr[g, rs, :] + pv[g][:, 0:dv]

    def main_body(j, carry):
        step(pl.multiple_of(j * tk, tk), tk, False)
        return carry

    def tail_body(j, carry):
        step(pl.multiple_of(base + j * tkt, tkt), tkt, False)
        return carry

    lax.fori_loop(0, n_main, main_body, 0)
    if tq % tk:
        lax.fori_loop(0, n_tail_full, tail_body, 0)
    for j in range(n_tail_diag):
        step(pl.multiple_of(base + (n_tail_full + j) * tkt, tkt), tkt, True, row_lo=j * tkt)


def _diff_attn_kernel(q_ref, k_ref, v_ref, bg_ref, lam_ref, sub_ref, o_ref,
                      qs_scr, m_scr, l_scr, acc_scr, *, tq, tk, heads, lam_init):
    qi = pl.program_id(2)
    for g in range(heads):
        q = q_ref[:, g * DV_B:(g + 1) * DV_B]
        lane = lax.broadcasted_iota(jnp.int32, q.shape, 1)
        zero = jnp.zeros_like(q)
        qs_scr[g, 0:tq, :] = jnp.where(lane < DK_B, q, zero)
        qs_scr[g, tq:2 * tq, :] = jnp.where(lane >= DK_B, q, zero)
    _flash_loop(qs_scr, k_ref, v_ref, m_scr, l_scr, acc_scr, qi, tq, tk, heads, DV_B, DV_B)
    lv = lam_ref[...]
    lam = (jnp.exp(jnp.sum(lv[0:1] * lv[1:2], axis=1, keepdims=True))
           - jnp.exp(jnp.sum(lv[2:3] * lv[3:4], axis=1, keepdims=True)) + lam_init)
    for g in range(heads):
        o1 = acc_scr[g, 0:tq, :] / l_scr[g, 0:tq, :]
        o2 = acc_scr[g, tq:2 * tq, :] / l_scr[g, tq:2 * tq, :]
        o = o1 - lam * o2
        ms = jnp.mean(o * o, axis=-1, keepdims=True)
        o = o * lax.rsqrt(ms + EPS) * sub_ref[...] * (1.0 - lam_init)
        gate = bg_ref[:, g * DV_B:(g + 1) * DV_B].astype(F32)
        o_ref[:, g * DV_B:(g + 1) * DV_B] = (o * _silu(gate)).astype(BF16)


def _diff_attention(q, k, v, bgate, lam_vec, subln, batch, seq, lam_init):
    n = q.shape[0]
    tq = min(DIFF_TQ, seq)
    tk = min(DIFF_TK, seq)
    nq = seq // tq
    heads = DIFF_HEADS_PER_STEP
    wid = heads * DV_B
    qspec = pl.BlockSpec((tq, wid), lambda b, h, i: (b * nq + i, h))
    kvspec = pl.BlockSpec((seq, wid), lambda b, h, i: (b, h))
    full = lambda a: pl.BlockSpec(a.shape, lambda b, h, i: (0,) * a.ndim)
    return pl.pallas_call(
        functools.partial(_diff_attn_kernel, tq=tq, tk=tk, heads=heads, lam_init=lam_init),
        grid=(batch, H_B // heads, nq),
        in_specs=[qspec, kvspec, kvspec, qspec, full(lam_vec), full(subln)],
        out_specs=qspec,
        out_shape=jax.ShapeDtypeStruct((n, W_B), BF16),
        scratch_shapes=[pltpu.VMEM((heads, 2 * tq, DV_B), BF16), pltpu.VMEM((heads, 2 * tq, LANES), F32),
                        pltpu.VMEM((heads, 2 * tq, LANES), F32), pltpu.VMEM((heads, 2 * tq, DV_B), F32)],
        compiler_params=_cparams(("parallel", "parallel", "arbitrary")),
        name="diff_attention",
    )(q, k, v, bgate, lam_vec, subln)


def _mla_attn_kernel(q_ref, k_ref, v_ref, cg_ref, o_ref, qs_scr, m_scr, l_scr, acc_scr, *, tq, tk, heads):
    qi = pl.program_id(2)
    for g in range(heads):
        qs_scr[g] = q_ref[:, g * HEAD_PAD:(g + 1) * HEAD_PAD]
    _flash_loop(qs_scr, k_ref, v_ref, m_scr, l_scr, acc_scr, qi, tq, tk, heads, HEAD_PAD, DV_C)
    for g in range(heads):
        o = acc_scr[g] / l_scr[g]
        gate = cg_ref[:, g * DV_C:(g + 1) * DV_C].astype(F32)
        o_ref[:, g * DV_C:(g + 1) * DV_C] = (o * _silu(gate)).astype(BF16)


def _mla_attention(q, k, v, cgate, batch, seq):
    n = q.shape[0]
    tq = min(MLA_TQ, seq)
    tk = min(MLA_TK, seq)
    nq = seq // tq
    heads = MLA_HEADS_PER_STEP
    qspec = pl.BlockSpec((tq, heads * HEAD_PAD), lambda b, h, i: (b * nq + i, h))
    kspec = pl.BlockSpec((seq, heads * HEAD_PAD), lambda b, h, i: (b, h))
    vspec = pl.BlockSpec((seq, heads * DV_C), lambda b, h, i: (b, h))
    ospec = pl.BlockSpec((tq, heads * DV_C), lambda b, h, i: (b * nq + i, h))
    return pl.pallas_call(
        functools.partial(_mla_attn_kernel, tq=tq, tk=tk, heads=heads),
        grid=(batch, H_C // heads, nq),
        in_specs=[qspec, kspec, vspec, ospec],
        out_specs=ospec,
        out_shape=jax.ShapeDtypeStruct((n, W_C), BF16),
        scratch_shapes=[pltpu.VMEM((heads, tq, HEAD_PAD), BF16), pltpu.VMEM((heads, tq, LANES), F32),
                        pltpu.VMEM((heads, tq, LANES), F32), pltpu.VMEM((heads, tq, DV_C), F32)],
        compiler_params=_cparams(("parallel", "parallel", "arbitrary")),
        name="mla_attention",
    )(q, k, v, cgate)


def _pool_values(x_ref, halo_ref, gate_ref, pw_ref, ps_ref, buf_a, buf_b, tm, seq):
    i = pl.program_id(0)
    t0 = (i * tm) % seq
    x = x_ref[...]
    keep = (t0 > 0).astype(F32)
    buf_a[0:POOL_HALO, :] = halo_ref[...] * keep
    buf_a[POOL_HALO:POOL_HALO + tm, :] = x
    tot = tm + POOL_HALO
    src, dst = buf_a, buf_b
    levels = {}
    start = 0
    for w in (1, 2, 4, 8):
        start += 8
        cur = src[start:tot, :] + src[start - w:tot - w, :]
        dst[start:tot, :] = cur
        levels[2 * w] = cur[POOL_HALO - start:, :]
        src, dst = dst, src
    lane = lax.broadcasted_iota(jnp.int32, (tm, W_A), 1)
    grp = lane // POOL_GC
    win = jnp.where(grp == 0, levels[2], jnp.where(grp == 1, levels[4],
                    jnp.where(grp == 2, levels[8], levels[16])))
    wlane = jnp.where(grp == 0, 2, jnp.where(grp == 1, 4, jnp.where(grp == 2, 8, 16)))
    tpos = t0 + lax.broadcasted_iota(jnp.int32, (tm, W_A), 0)
    cnt = jnp.minimum(tpos + 1, wlane).astype(F32)
    pooled = win / cnt - x
    a = jnp.dot(pooled.astype(BF16), pw_ref[...], preferred_element_type=F32) * ps_ref[...]
    return (a * _silu(gate_ref[...].astype(F32))).astype(BF16)


def _out_tail(h_ref, mix_a, mb_ref, wo_ref, wg_ref, p_ref, wp_ref, o_ref):
    wa = mix_a.shape[1]
    m = (jnp.dot(mix_a, wo_ref[0:wa, :], preferred_element_type=F32)
         + jnp.dot(mb_ref[...], wo_ref[wa:, :], preferred_element_type=F32))
    h1 = h_ref[...] + m
    gate = _sigmoid(jnp.dot(h1.astype(BF16), wg_ref[...], preferred_element_type=F32))
    pp = jnp.dot(p_ref[...].astype(BF16), wp_ref[...], preferred_element_type=F32)
    o_ref[...] = h1 + gate * pp


def _out_kernel(h_ref, ma_ref, mb_ref, wo_ref, wg_ref, p_ref, wp_ref, o_ref):
    _out_tail(h_ref, ma_ref[...], mb_ref, wo_ref, wg_ref, p_ref, wp_ref, o_ref)


def _even_out_kernel(h_ref, x_ref, halo_ref, gate_ref, pw_ref, ps_ref, mb_ref, wo_ref, wg_ref, p_ref, wp_ref,
                     o_ref, buf_a, buf_b, *, tm, seq):
    mix_a = _pool_values(x_ref, halo_ref, gate_ref, pw_ref, ps_ref, buf_a, buf_b, tm, seq)
    _out_tail(h_ref, mix_a, mb_ref, wo_ref, wg_ref, p_ref, wp_ref, o_ref)


def _even_out_proj(h, a_in, a_gate, pool_w_bd, pool_scale, mix_b, w_out, w_gate, p, layer, w_proj, seq):
    n = h.shape[0]
    tm = min(ROW_TILE, seq)
    hb = tm // POOL_HALO
    row = lambda w: pl.BlockSpec((tm, w), lambda i: (i, 0))
    halo = pl.BlockSpec((POOL_HALO, W_A), lambda i: (jnp.maximum(i * hb - 1, 0), 0))
    full = lambda a: pl.BlockSpec(a.shape, lambda i: (0,) * a.ndim)
    return pl.pallas_call(
        functools.partial(_even_out_kernel, tm=tm, seq=seq),
        grid=(n // tm,),
        in_specs=[row(D_MODEL), row(W_A), halo, row(W_A), full(pool_w_bd), full(pool_scale),
                  row(mix_b.shape[1]), full(w_out), full(w_gate),
                  pl.BlockSpec((None, tm, PLE_DIM), lambda i: (layer, i, 0)), full(w_proj)],
        out_specs=row(D_MODEL),
        out_shape=jax.ShapeDtypeStruct((n, D_MODEL), F32),
        scratch_shapes=[pltpu.VMEM((tm + POOL_HALO, W_A), F32)] * 2,
        compiler_params=_cparams(("parallel",)),
        name="pool_out_proj_ple",
    )(h, a_in, a_in, a_gate, pool_w_bd, pool_scale, mix_b, w_out, w_gate, p, w_proj)


def _out_proj(h, mix_a, mix_b, w_out, w_gate, p, layer, w_proj):
    n = h.shape[0]
    tm = min(ROW_TILE, n)
    wa = mix_a.shape[1]
    row = lambda w: pl.BlockSpec((tm, w), lambda i: (i, 0))
    full = lambda a: pl.BlockSpec(a.shape, lambda i: (0,) * a.ndim)
    return pl.pallas_call(
        _out_kernel,
        grid=(n // tm,),
        in_specs=[row(D_MODEL), row(wa), row(mix_b.shape[1]), full(w_out), full(w_gate),
                  pl.BlockSpec((None, tm, PLE_DIM), lambda i: (layer, i, 0)), full(w_proj)],
        out_specs=row(D_MODEL),
        out_shape=jax.ShapeDtypeStruct((n, D_MODEL), F32),
        compiler_params=_cparams(("parallel",)),
        name="out_proj_ple",
    )(h, mix_a, mix_b, w_out, w_gate, p, w_proj)


O_CQ = 0
O_CKV = O_CQ + Q_LORA
O_KR = O_CKV + KV_LORA
O_CG = O_KR + LANES
O_QKV = O_CG + W_C
O_DG = O_QKV + CONV_CH
O_BA = O_DG + W_D
O_TOT = O_BA + LANES
HEAD_PAD = 2 * LANES


def _odd_in_kernel(h_ref, g_ref, w_ref, qan_ref, wuq_ref, kvan_ref, wukv_ref, qn_ref, kn_ref,
                   c_ref, s_ref, alog_ref, dtb_ref, cw_ref,
                   q_ref, k_ref, v_ref, cg_ref, qkv_ref, dg_ref, gb_ref, cbuf, *, tm, seq):
    @pl.when((pl.program_id(0) * tm) % seq == 0)
    def _():
        cbuf[0:CONV_PAD, :] = jnp.zeros((CONV_PAD, CONV_CH), F32)

    cw = cw_ref[...]
    x = h_ref[...]
    ms = jnp.mean(x * x, axis=-1, keepdims=True)
    yb = (x * lax.rsqrt(ms + EPS) * g_ref[...]).astype(BF16)
    c_all = c_ref[...]
    s_all = s_ref[...]
    qn = qn_ref[...]
    kn = kn_ref[...]
    scale = DQK_C ** -0.5 * LOG2E

    def project(rs):
        def seg(lo, hi):
            return jnp.dot(yb[rs, :], w_ref[:, lo:hi], preferred_element_type=F32)

        cq = seg(O_CQ, O_CKV)
        ckv = seg(O_CKV, O_KR)
        kr = seg(O_KR, O_CG)
        z_cg = seg(O_CG, O_QKV)
        z_qkv = seg(O_QKV, O_DG)
        z_dg = seg(O_DG, O_BA)
        ba = seg(O_BA, O_TOT)
        cqn = cq * lax.rsqrt(jnp.mean(cq * cq, axis=-1, keepdims=True) + EPS) * qan_ref[...]
        ckvn = ckv * lax.rsqrt(jnp.mean(ckv * ckv, axis=-1, keepdims=True) + EPS) * kvan_ref[...]
        qu = jnp.dot(cqn.astype(BF16), wuq_ref[...], preferred_element_type=F32)
        kvu = jnp.dot(ckvn.astype(BF16), wukv_ref[...], preferred_element_type=F32)
        return kr, z_cg, z_qkv, z_dg, ba, qu, kvu

    def epilogue(rs, zs):
        kr, z_cg, z_qkv, z_dg, ba, qu, kvu = zs
        c = c_all[rs, :]
        s = s_all[rs, :]
        cg_ref[rs, :] = z_cg.astype(BF16)
        dg_ref[rs, :] = z_dg.astype(BF16)
        lo = CONV_PAD + rs.start
        nrow = rs.stop - rs.start
        cbuf[lo:lo + nrow, :] = z_qkv
        ext = cbuf[lo - CONV_PAD:lo + nrow, :]
        y = z_qkv * cw[CONV_K - 1:CONV_K, :]
        for d in range(1, CONV_K):
            shifted = pltpu.roll(ext, d, 0)[CONV_PAD:, :]
            y = y + shifted * cw[CONV_K - 1 - d:CONV_K - d, :]
        qkv_ref[rs, :] = _silu(y).astype(BF16)
        v_ref[rs, :] = kvu[:, H_C * D_NOPE:].astype(BF16)
        beta = _sigmoid(ba)
        g = -jnp.exp(alog_ref[...]) * jax.nn.softplus(ba + dtb_ref[...])
        lane = lax.broadcasted_iota(jnp.int32, ba.shape, 1)
        gb_ref[rs, :] = jnp.where(lane < H_D, beta, g)
        for hh in range(H_C):
            nope = qu[:, hh * D_NOPE:(hh + 1) * D_NOPE]
            rope = qu[:, H_C * D_NOPE + hh * LANES:H_C * D_NOPE + (hh + 1) * LANES]
            ss = jnp.sum(nope * nope + rope * rope, axis=-1, keepdims=True)
            r = lax.rsqrt(ss * (1.0 / DQK_C) + EPS) * scale
            q_ref[rs, hh * HEAD_PAD:hh * HEAD_PAD + LANES] = (nope * r * qn[:, 0:LANES]).astype(BF16)
            q_ref[rs, hh * HEAD_PAD + LANES:(hh + 1) * HEAD_PAD] = _rope_block(
                rope * r * qn[:, LANES:2 * LANES], c, s, D_ROPE // 2).astype(BF16)
        kr_ss = jnp.sum(kr * kr, axis=-1, keepdims=True)
        kr_rot = _rope_block(kr * kn[:, LANES:2 * LANES], c, s, D_ROPE // 2)
        for hh in range(H_C):
            nope = kvu[:, hh * D_NOPE:(hh + 1) * D_NOPE]
            ss = jnp.sum(nope * nope, axis=-1, keepdims=True) + kr_ss
            r = lax.rsqrt(ss * (1.0 / DQK_C) + EPS)
            k_ref[rs, hh * HEAD_PAD:hh * HEAD_PAD + LANES] = (nope * r * kn[:, 0:LANES]).astype(BF16)
            k_ref[rs, hh * HEAD_PAD + LANES:(hh + 1) * HEAD_PAD] = (kr_rot * r).astype(BF16)

    part = x.shape[0] // IN_PROJ_PARTS
    parts = [slice(k * part, (k + 1) * part) for k in range(IN_PROJ_PARTS)]
    z_prev = project(parts[0])
    for k in range(1, IN_PROJ_PARTS):
        z_next = project(parts[k])
        epilogue(parts[k - 1], z_prev)
        z_prev = z_next
    epilogue(parts[-1], z_prev)
    cbuf[0:CONV_PAD, :] = cbuf[tm:tm + CONV_PAD, :]


def _odd_in(h, g, w_in, qan, wuq, kvan, wukv, qn, kn, c_tab, s_tab, alog, dtb, conv_w, seq):
    n = h.shape[0]
    tm = min(ROW_TILE, seq)
    row = lambda w: pl.BlockSpec((tm, w), lambda i: (i, 0))
    full = lambda a: pl.BlockSpec(a.shape, lambda i: (0,) * a.ndim)
    sds = jax.ShapeDtypeStruct
    return pl.pallas_call(
        functools.partial(_odd_in_kernel, tm=tm, seq=seq),
        grid=(n // tm,),
        in_specs=[row(D_MODEL), full(g), full(w_in), full(qan), full(wuq), full(kvan), full(wukv),
                  full(qn), full(kn), row(LANES), row(LANES), full(alog), full(dtb), full(conv_w)],
        out_specs=[row(H_C * HEAD_PAD), row(H_C * HEAD_PAD), row(W_C), row(W_C), row(CONV_CH), row(W_D),
                   row(LANES)],
        out_shape=[sds((n, H_C * HEAD_PAD), BF16), sds((n, H_C * HEAD_PAD), BF16), sds((n, W_C), BF16),
                   sds((n, W_C), BF16), sds((n, CONV_CH), BF16), sds((n, W_D), BF16), sds((n, LANES), F32)],
        scratch_shapes=[pltpu.VMEM((tm + CONV_PAD, CONV_CH), F32)],
        compiler_params=_cparams(("arbitrary",)),
        name="odd_in_proj",
    )(h, g, w_in, qan, wuq, kvan, wukv, qn, kn, c_tab, s_tab, alog, dtb, conv_w)


def _gdn_prep_kernel(y_ref, gbc_ref, lhs1_ref, lhs2_ref, u_ref, gl_ref, *, tm):
    ri = lax.broadcasted_iota(jnp.int32, (tm, tm), 0)
    ci = lax.broadcasted_iota(jnp.int32, (tm, tm), 1)
    low = jnp.where(((ri // CHUNK) == (ci // CHUNK)) & (ri >= ci), 1.0, 0.0).astype(F32)
    i2 = lax.broadcasted_iota(jnp.int32, (LANES, LANES), 0)
    j2 = lax.broadcasted_iota(jnp.int32, (LANES, LANES), 1)
    eye_l = jnp.where(i2 == j2, 1.0, 0.0).astype(BF16)
    gbc = gbc_ref[...]
    gcs = _split3_dot(low, gbc)
    gcs_t = _split3_dot_nt(eye_l, gcs)

    pdim = H_D * CHUNK
    r4 = lax.broadcasted_iota(jnp.int32, (pdim, pdim), 0)
    c4 = lax.broadcasted_iota(jnp.int32, (pdim, pdim), 1)
    same = (r4 // CHUNK) == (c4 // CHUNK)
    incl = same & (r4 >= c4)
    strict = same & (r4 > c4)
    eye = jnp.where(r4 == c4, 1.0, 0.0).astype(F32)
    zblk = jnp.zeros((CHUNK, DK_D), F32)

    def block_diag(blocks):
        return jnp.concatenate(
            [jnp.concatenate([blocks[h] if j == h else zblk for j in range(H_D)], axis=1) for h in range(H_D)],
            axis=0)

    chunks = list(range(tm // CHUNK))
    heads = list(range(H_D))

    def load_qkv(c):
        r0 = c * CHUNK
        qs, ks, vs = [], [], []
        for hh in heads:
            l0 = hh * DK_D
            qh = y_ref[r0:r0 + CHUNK, l0:l0 + DK_D].astype(F32)
            kh = y_ref[r0:r0 + CHUNK, H_D * DK_D + l0:H_D * DK_D + l0 + DK_D].astype(F32)
            vs.append(y_ref[r0:r0 + CHUNK, 2 * H_D * DK_D + l0:2 * H_D * DK_D + l0 + DV_D].astype(F32))
            qs.append(qh * lax.rsqrt(jnp.sum(qh * qh, axis=-1, keepdims=True) + EPS) * (DK_D ** -0.5))
            ks.append(kh * lax.rsqrt(jnp.sum(kh * kh, axis=-1, keepdims=True) + EPS))
        return qs, ks, vs

    def decay_terms(c):
        r0 = c * CHUNK
        beta = jnp.concatenate([gbc[r0:r0 + CHUNK, hh:hh + 1] for hh in heads], axis=0)
        gcol = jnp.concatenate([gcs[r0:r0 + CHUNK, H_D + hh:H_D + hh + 1] for hh in heads], axis=0)
        grow = jnp.concatenate([gcs_t[H_D + hh:H_D + hh + 1, r0:r0 + CHUNK] for hh in heads], axis=1)
        glasts = [gcs[r0 + CHUNK - 1:r0 + CHUNK, H_D + hh:H_D + hh + 1] for hh in heads]
        glast = jnp.concatenate([jnp.broadcast_to(gl, (CHUNK, 1)) for gl in glasts], axis=0)
        gamma = jnp.exp(jnp.where(incl, gcol - grow, -jnp.inf))
        return beta, gcol, glasts, glast, gamma

    qkv = [load_qkv(c) for c in chunks]
    q_st = [jnp.concatenate(x[0], axis=0) for x in qkv]
    k_st = [jnp.concatenate(x[1], axis=0) for x in qkv]
    v_st = [jnp.concatenate(x[2], axis=0) for x in qkv]
    k_bd = [block_diag(x[1]).astype(BF16) for x in qkv]
    q_bd = [block_diag(x[0]).astype(BF16) for x in qkv]
    qkkk = [_bdot_nt(jnp.concatenate([qb, kb], axis=0), kb) for qb, kb in zip(q_bd, k_bd)]
    dec = [decay_terms(c) for c in chunks]
    a = [jnp.where(strict, x[pdim:, :] * d[4] * d[0], 0.0) for x, d in zip(qkkk, dec)]
    t = [eye - x for x in a]
    pw = a
    for _ in range(5):
        pw = [_bdot(x, x) for x in pw]
        t = [x + _bdot(x, y) for x, y in zip(t, pw)]
    egc = [jnp.exp(d[1]) for d in dec]
    uw = [_bdot(tt, jnp.concatenate([v * d[0], k * (d[0] * e)], axis=1))
          for tt, v, k, d, e in zip(t, v_st, k_st, dec, egc)]
    kdt = [_bdot_nt(eye_l, k * jnp.exp(d[3] - d[1])).astype(BF16) for k, d in zip(k_st, dec)]
    for c in chunks:
        qkg = (qkkk[c][0:pdim, :] * dec[c][4]).astype(BF16)
        q_dec = (q_st[c] * egc[c]).astype(BF16)
        for hh in heads:
            idx = c * H_D + hh
            hrows = slice(hh * CHUNK, (hh + 1) * CHUNK)
            lhs1_ref[idx, 0:CHUNK, :] = uw[c][hrows, DV_D:].astype(BF16)
            lhs1_ref[idx, CHUNK:2 * CHUNK, :] = q_dec[hrows, :]
            u_ref[idx] = uw[c][hrows, 0:DV_D].astype(BF16)
            gl_ref[c, hh:hh + 1, :] = jnp.broadcast_to(jnp.exp(dec[c][2][hh]), (1, LANES))
        for pp in range(H_D // 2):
            idx2 = c * (H_D // 2) + pp
            lhs2_ref[idx2, 0:LANES, :] = qkg[pp * LANES:(pp + 1) * LANES, pp * LANES:(pp + 1) * LANES]
            lhs2_ref[idx2, LANES:2 * LANES, :] = kdt[c][:, pp * LANES:(pp + 1) * LANES]


def _gdn_prep(y, gb, seq):
    n = y.shape[0]
    tm = min(GDN_PREP_TILE, seq)
    nch = tm // CHUNK
    row = lambda w: pl.BlockSpec((tm, w), lambda i: (i, 0))
    ch3 = lambda m, r, w: pl.BlockSpec((nch * m, r, w), lambda i: (i, 0, 0))
    sds = jax.ShapeDtypeStruct
    nc = n // CHUNK
    return pl.pallas_call(
        functools.partial(_gdn_prep_kernel, tm=tm),
        grid=(n // tm,),
        in_specs=[row(CONV_CH), row(LANES)],
        out_specs=[ch3(H_D, 2 * CHUNK, DK_D), ch3(H_D // 2, 2 * LANES, LANES), ch3(H_D, CHUNK, DV_D),
                   pl.BlockSpec((nch, H_D, LANES), lambda i: (i, 0, 0))],
        out_shape=[sds((nc * H_D, 2 * CHUNK, DK_D), BF16), sds((nc * H_D // 2, 2 * LANES, LANES), BF16),
                   sds((nc * H_D, CHUNK, DV_D), BF16), sds((nc, H_D, LANES), F32)],
        compiler_params=_cparams(("parallel",)),
        name="gdn_chunk_prep",
    )(y, gb)


def _gdn_scan_kernel(lhs1_ref, lhs2_ref, u_ref, gl_ref, dg_ref, on_ref, o_ref, s_scr, *, nb, tb):
    @pl.when(pl.program_id(0) == 0)
    def _():
        s_scr[...] = jnp.zeros(s_scr.shape, F32)

    on = on_ref[...]
    zb = jnp.zeros((2 * CHUNK, DK_D), BF16)
    left = lax.broadcasted_iota(jnp.int32, (DK_D, LANES), 1) < CHUNK

    def finish(o, b, rows, hh):
        ms = jnp.mean(o * o, axis=-1, keepdims=True)
        o = o * lax.rsqrt(ms + EPS) * on
        gate = dg_ref[b, rows, hh * DV_D:(hh + 1) * DV_D].astype(F32)
        o_ref[b, rows, hh * DV_D:(hh + 1) * DV_D] = (o * _silu(gate)).astype(BF16)

    def chunk_body(c, carry):
        rows = pl.ds(pl.multiple_of(c * CHUNK, CHUNK), CHUNK)
        chains = [(b, 2 * pp) for b in range(nb) for pp in range(H_D // 2)]
        st = [(s_scr[b * H_D + h0], s_scr[b * H_D + h0 + 1]) for b, h0 in chains]
        r1 = [jnp.dot(jnp.concatenate([jnp.concatenate([lhs1_ref[b, c * H_D + h0], zb], axis=1),
                                       jnp.concatenate([zb, lhs1_ref[b, c * H_D + h0 + 1]], axis=1)], axis=0),
                      jnp.concatenate([s0, s1], axis=0).astype(BF16), preferred_element_type=F32)
              for (b, h0), (s0, s1) in zip(chains, st)]
        v2 = [jnp.concatenate([u_ref[b, c * H_D + h0].astype(F32) - x[0:CHUNK, :],
                               u_ref[b, c * H_D + h0 + 1].astype(F32) - x[2 * CHUNK:3 * CHUNK, :]],
                              axis=0).astype(BF16) for (b, h0), x in zip(chains, r1)]
        r2 = []
        for (b, h0), v in zip(chains, v2):
            blk = lhs2_ref[b, c * (H_D // 2) + h0 // 2]
            kd = blk[LANES:2 * LANES, :]
            zk = jnp.zeros_like(kd)
            l2 = jnp.concatenate([blk[0:LANES, :], jnp.where(left, kd, zk), jnp.where(left, zk, kd)], axis=0)
            r2.append(jnp.dot(l2, v, preferred_element_type=F32))
        for (b, h0), (s0, s1), x, y in zip(chains, st, r1, r2):
            s_scr[b * H_D + h0] = s0 * gl_ref[b, c, h0:h0 + 1, :] + y[LANES:LANES + DK_D, :]
            s_scr[b * H_D + h0 + 1] = s1 * gl_ref[b, c, h0 + 1:h0 + 2, :] + y[LANES + DK_D:LANES + 2 * DK_D, :]
            finish(x[CHUNK:2 * CHUNK, :] + y[0:CHUNK, :], b, rows, h0)
            finish(x[3 * CHUNK:4 * CHUNK, :] + y[CHUNK:2 * CHUNK, :], b, rows, h0 + 1)
        return carry

    lax.fori_loop(0, tb // CHUNK, chunk_body, 0)


def _gdn_scan(lhs1, lhs2, u, gl, d_gate, o_norm, batch, seq):
    tb = min(GDN_SCAN_TILE, seq)
    nch = tb // CHUNK
    ncb = seq // CHUNK
    lhs1 = lhs1.reshape(batch, ncb * H_D, 2 * CHUNK, DK_D)
    lhs2 = lhs2.reshape(batch, ncb * H_D // 2, 2 * LANES, LANES)
    u = u.reshape(batch, ncb * H_D, CHUNK, DV_D)
    gl = gl.reshape(batch, ncb, H_D, LANES)
    d_gate = d_gate.reshape(batch, seq, W_D)
    ch4 = lambda r, w: pl.BlockSpec((batch, nch * H_D, r, w), lambda i: (0, i, 0, 0))
    tok = pl.BlockSpec((batch, tb, W_D), lambda i: (0, i, 0))
    out = pl.pallas_call(
        functools.partial(_gdn_scan_kernel, nb=batch, tb=tb),
        grid=(seq // tb,),
        in_specs=[ch4(2 * CHUNK, DK_D),
                  pl.BlockSpec((batch, nch * H_D // 2, 2 * LANES, LANES), lambda i: (0, i, 0, 0)),
                  ch4(CHUNK, DV_D),
                  pl.BlockSpec((batch, nch, H_D, LANES), lambda i: (0, i, 0, 0)), tok,
                  pl.BlockSpec(o_norm.shape, lambda i: (0, 0))],
        out_specs=tok,
        out_shape=jax.ShapeDtypeStruct((batch, seq, W_D), BF16),
        scratch_shapes=[pltpu.VMEM((batch * H_D, DK_D, DV_D), F32)],
        compiler_params=_cparams(("arbitrary",)),
        name="gdn_state_scan",
    )(lhs1, lhs2, u, gl, d_gate, o_norm)
    return out.reshape(batch * seq, W_D)


def _rope_patterns(rot_dim, theta, period):
    half = rot_dim // 2
    inv = jnp.power(jnp.float32(theta), -jnp.arange(half, dtype=F32) * (2.0 / rot_dim))
    lane = np.arange(LANES)
    in_rot = (lane % period) < rot_dim
    idx = jnp.asarray(lane % half)
    inv_l = jnp.where(jnp.asarray(in_rot), inv[idx], 0.0).astype(F32)[None, :]
    sgn = np.where(in_rot, np.where((lane % period) < half, -1.0, 1.0), 0.0).astype(np.float32)[None, :]
    return inv_l, jnp.asarray(sgn)


def _pad_cols(w, width):
    return jnp.pad(w, ((0, 0), (0, width - w.shape[1])))


def _odd_w_in_layout(w):
    cq, ckv, kr, cg, qkv, db, da, dg = jnp.split(
        w, np.cumsum((Q_LORA, KV_LORA, D_ROPE, W_C, CONV_CH, H_D, H_D, W_D))[:-1].tolist(), axis=1)
    return jnp.concatenate([cq, ckv, _pad_cols(kr, LANES), cg, qkv, dg,
                            _pad_cols(jnp.concatenate([db, da], axis=1), LANES)], axis=1)


def _wuq_layout(w):
    w = w.reshape(Q_LORA, H_C, DQK_C)
    nope = w[:, :, :D_NOPE].reshape(Q_LORA, H_C * D_NOPE)
    rope = jnp.pad(w[:, :, D_NOPE:], ((0, 0), (0, 0), (0, LANES - D_ROPE))).reshape(Q_LORA, H_C * LANES)
    return jnp.concatenate([nope, rope], axis=1)


def _wukv_layout(w):
    w = w.reshape(KV_LORA, H_C, D_NOPE + DV_C)
    return jnp.concatenate([w[:, :, :D_NOPE].reshape(KV_LORA, H_C * D_NOPE),
                            w[:, :, D_NOPE:].reshape(KV_LORA, H_C * DV_C)], axis=1)


def _head_gain_layout(g):
    return jnp.pad(g, (0, HEAD_PAD - DQK_C))[None, :].astype(F32)


def kernel(x, p, positions, norm_g, ple_w_gate, ple_w_proj, ev_w_in, ev_pool_w, ev_pool_scale, ev_q_norm, ev_k_norm, ev_lambda, ev_subln, ev_w_out, od_w_in, od_q_a_norm, od_w_uq, od_kv_a_norm, od_w_ukv, od_q_norm, od_k_norm, od_conv_w, od_a_log, od_dt_bias, od_o_norm, od_w_out):
    batch, seq, _ = x.shape
    depth = p.shape[0]
    n = batch * seq
    h = x.reshape(n, D_MODEL)
    pos_b = jnp.broadcast_to(positions.astype(F32).reshape(n, 1), (n, LANES))

    inv_e, sgn_e = _rope_patterns(B_ROT, ROPE_THETA, DK_B)
    ce, se = _rope_tables(pos_b, inv_e, sgn_e)
    inv_o, sgn_o = _rope_patterns(D_ROPE, MLA_THETA, LANES)
    co, so = _rope_tables(pos_b, inv_o, sgn_o)


    p_all = p.reshape(depth, n, PLE_DIM)
    for i in range(depth):
        j = i // 2
        g = norm_g[i][None, :]
        w_gate = ple_w_gate[i].astype(BF16)
        w_proj = ple_w_proj[i].astype(BF16)
        if i % 2 == 0:
            w_in = ev_w_in[j].astype(BF16)
            qg = (jnp.tile(ev_q_norm[j], 2 * H_B) * (DK_B ** -0.5 * LOG2E))[None, :]
            kg = jnp.tile(ev_k_norm[j], 2 * H_B)[None, :]
            a_in, a_gate, q, k, v, b_gate = _even_in(h, g, w_in, qg, kg, ce, se)
            lam_init = 0.8 - 0.6 * math.exp(-0.3 * i)
            ob = _diff_attention(q, k, v, b_gate, ev_lambda[j], ev_subln[j][None, :], batch, seq, lam_init)
            pw = jax.scipy.linalg.block_diag(*[ev_pool_w[j][gi] for gi in range(len(POOL_WINDOWS))]).astype(BF16)
            h = _even_out_proj(h, a_in, a_gate, pw, ev_pool_scale[j][None, :], ob, ev_w_out[j].astype(BF16),
                               w_gate, p_all, i, w_proj, seq)
        else:
            w_in = _odd_w_in_layout(od_w_in[j]).astype(BF16)
            alog = jnp.zeros((LANES,), F32).at[H_D:2 * H_D].set(od_a_log[j])[None, :]
            dtb = jnp.zeros((LANES,), F32).at[H_D:2 * H_D].set(od_dt_bias[j])[None, :]
            q, k, v, c_gate, qkv, d_gate, gb = _odd_in(
                h, g, w_in, od_q_a_norm[j][None, :], _wuq_layout(od_w_uq[j]).astype(BF16),
                od_kv_a_norm[j][None, :], _wukv_layout(od_w_ukv[j]).astype(BF16),
                _head_gain_layout(od_q_norm[j]), _head_gain_layout(od_k_norm[j]), co, so, alog, dtb,
                od_conv_w[j], seq)
            oa = _mla_attention(q, k, v, c_gate, batch, seq)
            lhs1, lhs2, u, gl = _gdn_prep(qkv, gb, seq)
            ob = _gdn_scan(lhs1, lhs2, u, gl, d_gate, od_o_norm[j][None, :], batch, seq)
            h = _out_proj(h, oa, ob, od_w_out[j].astype(BF16), w_gate, p_all, i, w_proj)
    return h.reshape(batch, seq, D_MODEL)
```

```python
import functools
import math

import numpy as np
import jax
import jax.numpy as jnp
from jax import lax
from jax.experimental import pallas as pl
from jax.experimental.pallas import tpu as pltpu

F32 = jnp.float32
BF16 = jnp.bfloat16

D_MODEL = 1024
PLE_DIM = 256
EPS = 1e-6
W_A = 256
POOL_WINDOWS = (2, 4, 8, 16)
POOL_GC = 64
W_B = 768
DV_B = 128
H_B = 6
DK_B = 64
B_ROT = 16
ROPE_THETA = 500000.0
W_C = 512
DV_C = 128
H_C = 4
D_NOPE = 128
D_ROPE = 64
DQK_C = 192
Q_LORA = 256
KV_LORA = 128
MLA_THETA = 10000.0
W_D = 512
DK_D = 128
DV_D = 128
H_D = 4
CONV_K = 4
CHUNK = 64
CONV_CH = H_D * (2 * DK_D + DV_D)

LANES = 128
VMEM_LIMIT = 48 * 1024 * 1024
NEG = -1e30
LOG2E = math.log2(math.e)

ROW_TILE = 512
IN_PROJ_PARTS = 2
DIFF_TK = 1024
MLA_TK = 1024
DIAG_TK = 512
POOL_HALO = 32
CONV_PAD = 8
GDN_PREP_TILE = 256
GDN_SCAN_TILE = 256
DIFF_HEADS_PER_STEP = 3
MLA_HEADS_PER_STEP = 2
DIFF_TQ = 512
MLA_TQ = 1024


def _cparams(sem):
    return pltpu.CompilerParams(dimension_semantics=sem, vmem_limit_bytes=VMEM_LIMIT)


def _silu(x):
    return x * (1.0 / (1.0 + jnp.exp(-x)))


def _sigmoid(x):
    return 1.0 / (1.0 + jnp.exp(-x))


def _bdot(a, b):
    return jnp.dot(a.astype(BF16), b.astype(BF16), preferred_element_type=F32)


def _bdot_nt(a, b):
    return lax.dot_general(a.astype(BF16), b.astype(BF16), (((1,), (1,)), ((), ())),
                           preferred_element_type=F32)


def _split3_dot(a_exact, b):
    b0 = b.astype(BF16)
    r1 = b - b0.astype(F32)
    b1 = r1.astype(BF16)
    b2 = (r1 - b1.astype(F32)).astype(BF16)
    a = a_exact.astype(BF16)
    return (jnp.dot(a, b0, preferred_element_type=F32) + jnp.dot(a, b1, preferred_element_type=F32)
            + jnp.dot(a, b2, preferred_element_type=F32))


def _split3_dot_nt(a_exact, b):
    b0 = b.astype(BF16)
    r1 = b - b0.astype(F32)
    b1 = r1.astype(BF16)
    b2 = (r1 - b1.astype(F32)).astype(BF16)
    return _bdot_nt(a_exact, b0) + _bdot_nt(a_exact, b1) + _bdot_nt(a_exact, b2)


def _rope_table_kernel(pos_ref, inv_ref, sgn_ref, c_ref, s_ref):
    ang = pos_ref[...] * inv_ref[...]
    c_ref[...] = jnp.cos(ang)
    s_ref[...] = jnp.sin(ang) * sgn_ref[...]


def _rope_tables(pos_b, inv, sgn):
    n = pos_b.shape[0]
    tm = min(1024, n)
    row = pl.BlockSpec((tm, LANES), lambda i: (i, 0))
    par = pl.BlockSpec((1, LANES), lambda i: (0, 0))
    return pl.pallas_call(
        _rope_table_kernel,
        grid=(n // tm,),
        in_specs=[row, par, par],
        out_specs=[row, row],
        out_shape=[jax.ShapeDtypeStruct((n, LANES), F32)] * 2,
        compiler_params=_cparams(("parallel",)),
        name="rope_tables",
    )(pos_b, inv, sgn)


def _rope_block(x, c, s, half):
    fwd = pltpu.roll(x, LANES - half, 1)
    bwd = pltpu.roll(x, half, 1)
    lane = lax.broadcasted_iota(jnp.int32, x.shape, 1)
    rot = jnp.where((lane % (2 * half)) < half, fwd, bwd)
    return x * c + rot * s


def _even_in_kernel(h_ref, g_ref, w_ref, qg_ref, kg_ref, c_ref, s_ref,
                    ain_ref, agate_ref, q_ref, k_ref, v_ref, bgate_ref):
    x = h_ref[...]
    ms = jnp.mean(x * x, axis=-1, keepdims=True)
    yb = (x * lax.rsqrt(ms + EPS) * g_ref[...]).astype(BF16)

    c_all = c_ref[...]
    s_all = s_ref[...]
    o = 2 * W_A
    bounds = [(0, W_A), (W_A, 2 * W_A), (o, o + W_B), (o + W_B, o + 2 * W_B), (o + 2 * W_B, o + 3 * W_B),
              (o + 3 * W_B, o + 4 * W_B)]

    def project(rs):
        return [jnp.dot(yb[rs, :], w_ref[:, lo:hi], preferred_element_type=F32) for lo, hi in bounds]

    def qk_prep(rs, z, gain_ref, out_ref, c, s):
        gain = gain_ref[...]
        first = lax.broadcasted_iota(jnp.int32, (z.shape[0], LANES), 1) < DK_B
        for j in range(W_B // LANES):
            blk = z[:, j * LANES:(j + 1) * LANES]
            sq = blk * blk
            ss0 = jnp.sum(jnp.where(first, sq, 0.0), axis=-1, keepdims=True)
            ss1 = jnp.sum(jnp.where(first, 0.0, sq), axis=-1, keepdims=True)
            r = jnp.where(first, lax.rsqrt(ss0 * (1.0 / DK_B) + EPS), lax.rsqrt(ss1 * (1.0 / DK_B) + EPS))
            zn = blk * r * gain[:, j * LANES:(j + 1) * LANES]
            out_ref[rs, j * LANES:(j + 1) * LANES] = _rope_block(zn, c, s, B_ROT // 2).astype(BF16)

    def epilogue(rs, zs):
        z_ain, z_agate, z_q, z_k, z_v, z_bg = zs
        ain_ref[rs, :] = z_ain
        agate_ref[rs, :] = z_agate.astype(BF16)
        v_ref[rs, :] = z_v.astype(BF16)
        bgate_ref[rs, :] = z_bg.astype(BF16)
        qk_prep(rs, z_q, qg_ref, q_ref, c_all[rs, :], s_all[rs, :])
        qk_prep(rs, z_k, kg_ref, k_ref, c_all[rs, :], s_all[rs, :])

    part = x.shape[0] // IN_PROJ_PARTS
    parts = [slice(k * part, (k + 1) * part) for k in range(IN_PROJ_PARTS)]
    z_prev = project(parts[0])
    for k in range(1, IN_PROJ_PARTS):
        z_next = project(parts[k])
        epilogue(parts[k - 1], z_prev)
        z_prev = z_next
    epilogue(parts[-1], z_prev)


def _even_in(h, g, w_in, qg, kg, c_tab, s_tab):
    n = h.shape[0]
    tm = min(ROW_TILE, n)
    row = lambda w: pl.BlockSpec((tm, w), lambda i: (i, 0))
    full = lambda a: pl.BlockSpec(a.shape, lambda i: (0,) * a.ndim)
    sds = jax.ShapeDtypeStruct
    return pl.pallas_call(
        _even_in_kernel,
        grid=(n // tm,),
        in_specs=[row(D_MODEL), full(g), full(w_in), full(qg), full(kg), row(LANES), row(LANES)],
        out_specs=[row(W_A), row(W_A), row(W_B), row(W_B), row(W_B), row(W_B)],
        out_shape=[sds((n, W_A), F32), sds((n, W_A), BF16), sds((n, W_B), BF16), sds((n, W_B), BF16),
                   sds((n, W_B), BF16), sds((n, W_B), BF16)],
        compiler_params=_cparams(("parallel",)),
        name="even_in_proj",
    )(h, g, w_in, qg, kg, c_tab, s_tab)


def _flash_loop(qs_ref, k_ref, v_ref, m_scr, l_scr, acc_scr, qi, tq, tk, heads, dqk, dv):
    q0 = qi * tq
    rows = qs_ref.shape[1]
    tkt = min(tq, tk, DIAG_TK)
    n_main = q0 // tk
    base = n_main * tk
    n_tail_full = (q0 - base) // tkt
    n_tail_diag = tq // tkt
    assert n_tail_diag == 1 or rows == tq
    m_scr[...] = jnp.full(m_scr.shape, NEG, F32)
    l_scr[...] = jnp.zeros(l_scr.shape, F32)
    acc_scr[...] = jnp.zeros(acc_scr.shape, F32)

    def step(ks, tk, masked, row_lo=0):
        hs = range(heads)
        rs = slice(row_lo, rows)
        s = [lax.dot_general(qs_ref[g, rs, :], k_ref[pl.ds(ks, tk), g * dqk:(g + 1) * dqk],
                             (((1,), (1,)), ((), ())), preferred_element_type=F32) for g in hs]
        if masked:
            row = q0 + (row_lo + lax.broadcasted_iota(jnp.int32, s[0].shape, 0)) % tq
            col = ks + lax.broadcasted_iota(jnp.int32, s[0].shape, 1)
            s = [jnp.where(col <= row, x, NEG) for x in s]
        m_prev = [m_scr[g, rs, :] for g in hs]
        m_new = [jnp.maximum(mp, jnp.max(x, axis=1, keepdims=True)) for mp, x in zip(m_prev, s)]
        alpha = [jnp.exp2(mp - mn) for mp, mn in zip(m_prev, m_new)]
        p = [jnp.exp2(x - jnp.tile(mn, (1, tk // LANES))) for x, mn in zip(s, m_new)]
        ones = jnp.ones((tk, LANES), BF16)
        pv = [jnp.dot(p[g].astype(BF16),
                      jnp.concatenate([v_ref[pl.ds(ks, tk), g * dv:(g + 1) * dv], ones], axis=1),
                      preferred_element_type=F32) for g in hs]
        for g in hs:
            m_scr[g, rs, :] = m_new[g]
            l_scr[g, rs, :] = alpha[g] * l_scr[g, rs, :] + pv[g][:, dv:]
            acc_scr[g, rs, :] = alpha[g] * acc_scr[g, rs, :] + pv[g][:, 0:dv]

    def main_body(j, carry):
        step(pl.multiple_of(j * tk, tk), tk, False)
        return carry

    def tail_body(j, carry):
        step(pl.multiple_of(base + j * tkt, tkt), tkt, False)
        return carry

    lax.fori_loop(0, n_main, main_body, 0)
    if tq % tk:
        lax.fori_loop(0, n_tail_full, tail_body, 0)
    for j in range(n_tail_diag):
        step(pl.multiple_of(base + (n_tail_full + j) * tkt, tkt), tkt, True, row_lo=j * tkt)


def _diff_attn_kernel(q_ref, k_ref, v_ref, bg_ref, lam_ref, sub_ref, o_ref,
                      qs_scr, m_scr, l_scr, acc_scr, *, tq, tk, heads, lam_init):
    qi = pl.program_id(2)
    for g in range(heads):
        q = q_ref[:, g * DV_B:(g + 1) * DV_B]
        lane = lax.broadcasted_iota(jnp.int32, q.shape, 1)
        zero = jnp.zeros_like(q)
        qs_scr[g, 0:tq, :] = jnp.where(lane < DK_B, q, zero)
        qs_scr[g, tq:2 * tq, :] = jnp.where(lane >= DK_B, q, zero)
    _flash_loop(qs_scr, k_ref, v_ref, m_scr, l_scr, acc_scr, qi, tq, tk, heads, DV_B, DV_B)
    lv = lam_ref[...]
    lam = (jnp.exp(jnp.sum(lv[0:1] * lv[1:2], axis=1, keepdims=True))
           - jnp.exp(jnp.sum(lv[2:3] * lv[3:4], axis=1, keepdims=True)) + lam_init)
    for g in range(heads):
        o1 = acc_scr[g, 0:tq, :] / l_scr[g, 0:tq, :]
        o2 = acc_scr[g, tq:2 * tq, :] / l_scr[g, tq:2 * tq, :]
        o = o1 - lam * o2
        ms = jnp.mean(o * o, axis=-1, keepdims=True)
        o = o * lax.rsqrt(ms + EPS) * sub_ref[...] * (1.0 - lam_init)
        gate = bg_ref[:, g * DV_B:(g + 1) * DV_B].astype(F32)
        o_ref[:, g * DV_B:(g + 1) * DV_B] = (o * _silu(gate)).astype(BF16)


def _diff_attention(q, k, v, bgate, lam_vec, subln, batch, seq, lam_init):
    n = q.shape[0]
    tq = min(DIFF_TQ, seq)
    tk = min(DIFF_TK, seq)
    nq = seq // tq
    heads = DIFF_HEADS_PER_STEP
    wid = heads * DV_B
    qspec = pl.BlockSpec((tq, wid), lambda b, h, i: (b * nq + i, h))
    kvspec = pl.BlockSpec((seq, wid), lambda b, h, i: (b, h), pipeline_mode=pl.Buffered(1))
    full = lambda a: pl.BlockSpec(a.shape, lambda b, h, i: (0,) * a.ndim)
    return pl.pallas_call(
        functools.partial(_diff_attn_kernel, tq=tq, tk=tk, heads=heads, lam_init=lam_init),
        grid=(batch, H_B // heads, nq),
        in_specs=[qspec, kvspec, kvspec, qspec, full(lam_vec), full(subln)],
        out_specs=qspec,
        out_shape=jax.ShapeDtypeStruct((n, W_B), BF16),
        scratch_shapes=[pltpu.VMEM((heads, 2 * tq, DV_B), BF16), pltpu.VMEM((heads, 2 * tq, LANES), F32),
                        pltpu.VMEM((heads, 2 * tq, LANES), F32), pltpu.VMEM((heads, 2 * tq, DV_B), F32)],
        compiler_params=_cparams(("parallel", "parallel", "arbitrary")),
        name="diff_attention",
    )(q, k, v, bgate, lam_vec, subln)


def _mla_attn_kernel(q_ref, k_ref, v_ref, cg_ref, o_ref, qs_scr, m_scr, l_scr, acc_scr, *, tq, tk, heads):
    qi = pl.program_id(2)
    for g in range(heads):
        qs_scr[g] = q_ref[:, g * HEAD_PAD:(g + 1) * HEAD_PAD]
    _flash_loop(qs_scr, k_ref, v_ref, m_scr, l_scr, acc_scr, qi, tq, tk, heads, HEAD_PAD, DV_C)
    for g in range(heads):
        o = acc_scr[g] / l_scr[g]
        gate = cg_ref[:, g * DV_C:(g + 1) * DV_C].astype(F32)
        o_ref[:, g * DV_C:(g + 1) * DV_C] = (o * _silu(gate)).astype(BF16)


def _mla_attention(q, k, v, cgate, batch, seq):
    n = q.shape[0]
    tq = min(MLA_TQ, seq)
    tk = min(MLA_TK, seq)
    nq = seq // tq
    heads = MLA_HEADS_PER_STEP
    qspec = pl.BlockSpec((tq, heads * HEAD_PAD), lambda b, h, i: (b * nq + i, h))
    kspec = pl.BlockSpec((seq, heads * HEAD_PAD), lambda b, h, i: (b, h))
    vspec = pl.BlockSpec((seq, heads * DV_C), lambda b, h, i: (b, h))
    ospec = pl.BlockSpec((tq, heads * DV_C), lambda b, h, i: (b * nq + i, h))
    return pl.pallas_call(
        functools.partial(_mla_attn_kernel, tq=tq, tk=tk, heads=heads),
        grid=(batch, H_C // heads, nq),
        in_specs=[qspec, kspec, vspec, ospec],
        out_specs=ospec,
        out_shape=jax.ShapeDtypeStruct((n, W_C), BF16),
        scratch_shapes=[pltpu.VMEM((heads, tq, HEAD_PAD), BF16), pltpu.VMEM((heads, tq, LANES), F32),
                        pltpu.VMEM((heads, tq, LANES), F32), pltpu.VMEM((heads, tq, DV_C), F32)],
        compiler_params=_cparams(("parallel", "parallel", "arbitrary")),
        name="mla_attention",
    )(q, k, v, cgate)


def _pool_values(x_ref, halo_ref, gate_ref, pw_ref, ps_ref, buf_a, buf_b, tm, seq):
    i = pl.program_id(0)
    t0 = (i * tm) % seq
    x = x_ref[...]
    keep = (t0 > 0).astype(F32)
    buf_a[0:POOL_HALO, :] = halo_ref[...] * keep
    buf_a[POOL_HALO:POOL_HALO + tm, :] = x
    tot = tm + POOL_HALO
    src, dst = buf_a, buf_b
    levels = {}
    start = 0
    for w in (1, 2, 4, 8):
        start += 8
        cur = src[start:tot, :] + src[start - w:tot - w, :]
        dst[start:tot, :] = cur
        levels[2 * w] = cur[POOL_HALO - start:, :]
        src, dst = dst, src
    lane = lax.broadcasted_iota(jnp.int32, (tm, W_A), 1)
    grp = lane // POOL_GC
    win = jnp.where(grp == 0, levels[2], jnp.where(grp == 1, levels[4],
                    jnp.where(grp == 2, levels[8], levels[16])))
    wlane = jnp.where(grp == 0, 2, jnp.where(grp == 1, 4, jnp.where(grp == 2, 8, 16)))
    tpos = t0 + lax.broadcasted_iota(jnp.int32, (tm, W_A), 0)
    cnt = jnp.minimum(tpos + 1, wlane).astype(F32)
    pooled = win / cnt - x
    a = jnp.dot(pooled.astype(BF16), pw_ref[...], preferred_element_type=F32) * ps_ref[...]
    return (a * _silu(gate_ref[...].astype(F32))).astype(BF16)


def _out_tail(h_ref, mix_a, mb_ref, wo_ref, wg_ref, p_ref, wp_ref, o_ref):
    wa = mix_a.shape[1]
    m = (jnp.dot(mix_a, wo_ref[0:wa, :], preferred_element_type=F32)
         + jnp.dot(mb_ref[...], wo_ref[wa:, :], preferred_element_type=F32))
    h1 = h_ref[...] + m
    gate = _sigmoid(jnp.dot(h1.astype(BF16), wg_ref[...], preferred_element_type=F32))
    pp = jnp.dot(p_ref[...].astype(BF16), wp_ref[...], preferred_element_type=F32)
    o_ref[...] = h1 + gate * pp


def _out_kernel(h_ref, ma_ref, mb_ref, wo_ref, wg_ref, p_ref, wp_ref, o_ref):
    _out_tail(h_ref, ma_ref[...], mb_ref, wo_ref, wg_ref, p_ref, wp_ref, o_ref)


def _even_out_kernel(h_ref, x_ref, halo_ref, gate_ref, pw_ref, ps_ref, mb_ref, wo_ref, wg_ref, p_ref, wp_ref,
                     o_ref, buf_a, buf_b, *, tm, seq):
    mix_a = _pool_values(x_ref, halo_ref, gate_ref, pw_ref, ps_ref, buf_a, buf_b, tm, seq)
    _out_tail(h_ref, mix_a, mb_ref, wo_ref, wg_ref, p_ref, wp_ref, o_ref)


def _even_out_proj(h, a_in, a_gate, pool_w_bd, pool_scale, mix_b, w_out, w_gate, p, layer, w_proj, seq):
    n = h.shape[0]
    tm = min(ROW_TILE, seq)
    hb = tm // POOL_HALO
    row = lambda w: pl.BlockSpec((tm, w), lambda i: (i, 0))
    halo = pl.BlockSpec((POOL_HALO, W_A), lambda i: (jnp.maximum(i * hb - 1, 0), 0))
    full = lambda a: pl.BlockSpec(a.shape, lambda i: (0,) * a.ndim)
    return pl.pallas_call(
        functools.partial(_even_out_kernel, tm=tm, seq=seq),
        grid=(n // tm,),
        in_specs=[row(D_MODEL), row(W_A), halo, row(W_A), full(pool_w_bd), full(pool_scale),
                  row(mix_b.shape[1]), full(w_out), full(w_gate),
                  pl.BlockSpec((None, tm, PLE_DIM), lambda i: (layer, i, 0)), full(w_proj)],
        out_specs=row(D_MODEL),
        out_shape=jax.ShapeDtypeStruct((n, D_MODEL), F32),
        scratch_shapes=[pltpu.VMEM((tm + POOL_HALO, W_A), F32)] * 2,
        compiler_params=_cparams(("parallel",)),
        name="pool_out_proj_ple",
    )(h, a_in, a_in, a_gate, pool_w_bd, pool_scale, mix_b, w_out, w_gate, p, w_proj)


def _out_proj(h, mix_a, mix_b, w_out, w_gate, p, layer, w_proj):
    n = h.shape[0]
    tm = min(ROW_TILE, n)
    wa = mix_a.shape[1]
    row = lambda w: pl.BlockSpec((tm, w), lambda i: (i, 0))
    full = lambda a: pl.BlockSpec(a.shape, lambda i: (0,) * a.ndim)
    return pl.pallas_call(
        _out_kernel,
        grid=(n // tm,),
        in_specs=[row(D_MODEL), row(wa), row(mix_b.shape[1]), full(w_out), full(w_gate),
                  pl.BlockSpec((None, tm, PLE_DIM), lambda i: (layer, i, 0)), full(w_proj)],
        out_specs=row(D_MODEL),
        out_shape=jax.ShapeDtypeStruct((n, D_MODEL), F32),
        compiler_params=_cparams(("parallel",)),
        name="out_proj_ple",
    )(h, mix_a, mix_b, w_out, w_gate, p, w_proj)


O_CQ = 0
O_CKV = O_CQ + Q_LORA
O_KR = O_CKV + KV_LORA
O_CG = O_KR + LANES
O_QKV = O_CG + W_C
O_DG = O_QKV + CONV_CH
O_BA = O_DG + W_D
O_TOT = O_BA + LANES
HEAD_PAD = 2 * LANES


def _odd_in_kernel(h_ref, g_ref, w_ref, qan_ref, wuq_ref, kvan_ref, wukv_ref, qn_ref, kn_ref,
                   c_ref, s_ref, alog_ref, dtb_ref, cw_ref,
                   q_ref, k_ref, v_ref, cg_ref, qkv_ref, dg_ref, gb_ref, cbuf, *, tm, seq):
    @pl.when((pl.program_id(0) * tm) % seq == 0)
    def _():
        cbuf[0:CONV_PAD, :] = jnp.zeros((CONV_PAD, CONV_CH), F32)

    cw = cw_ref[...]
    x = h_ref[...]
    ms = jnp.mean(x * x, axis=-1, keepdims=True)
    yb = (x * lax.rsqrt(ms + EPS) * g_ref[...]).astype(BF16)
    c_all = c_ref[...]
    s_all = s_ref[...]
    qn = qn_ref[...]
    kn = kn_ref[...]
    scale = DQK_C ** -0.5 * LOG2E

    def project(rs):
        def seg(lo, hi):
            return jnp.dot(yb[rs, :], w_ref[:, lo:hi], preferred_element_type=F32)

        cq = seg(O_CQ, O_CKV)
        ckv = seg(O_CKV, O_KR)
        kr = seg(O_KR, O_CG)
        z_cg = seg(O_CG, O_QKV)
        z_qkv = seg(O_QKV, O_DG)
        z_dg = seg(O_DG, O_BA)
        ba = seg(O_BA, O_TOT)
        cqn = cq * lax.rsqrt(jnp.mean(cq * cq, axis=-1, keepdims=True) + EPS) * qan_ref[...]
        ckvn = ckv * lax.rsqrt(jnp.mean(ckv * ckv, axis=-1, keepdims=True) + EPS) * kvan_ref[...]
        qu = jnp.dot(cqn.astype(BF16), wuq_ref[...], preferred_element_type=F32)
        kvu = jnp.dot(ckvn.astype(BF16), wukv_ref[...], preferred_element_type=F32)
        return kr, z_cg, z_qkv, z_dg, ba, qu, kvu

    def epilogue(rs, zs):
        kr, z_cg, z_qkv, z_dg, ba, qu, kvu = zs
        c = c_all[rs, :]
        s = s_all[rs, :]
        cg_ref[rs, :] = z_cg.astype(BF16)
        dg_ref[rs, :] = z_dg.astype(BF16)
        lo = CONV_PAD + rs.start
        nrow = rs.stop - rs.start
        cbuf[lo:lo + nrow, :] = z_qkv
        ext = cbuf[lo - CONV_PAD:lo + nrow, :]
        y = z_qkv * cw[CONV_K - 1:CONV_K, :]
        for d in range(1, CONV_K):
            shifted = pltpu.roll(ext, d, 0)[CONV_PAD:, :]
            y = y + shifted * cw[CONV_K - 1 - d:CONV_K - d, :]
        qkv_ref[rs, :] = _silu(y).astype(BF16)
        v_ref[rs, :] = kvu[:, H_C * D_NOPE:].astype(BF16)
        beta = _sigmoid(ba)
        g = -jnp.exp(alog_ref[...]) * jax.nn.softplus(ba + dtb_ref[...])
        lane = lax.broadcasted_iota(jnp.int32, ba.shape, 1)
        gb_ref[rs, :] = jnp.where(lane < H_D, beta, g)
        for hh in range(H_C):
            nope = qu[:, hh * D_NOPE:(hh + 1) * D_NOPE]
            rope = qu[:, H_C * D_NOPE + hh * LANES:H_C * D_NOPE + (hh + 1) * LANES]
            ss = jnp.sum(nope * nope + rope * rope, axis=-1, keepdims=True)
            r = lax.rsqrt(ss * (1.0 / DQK_C) + EPS) * scale
            q_ref[rs, hh * HEAD_PAD:hh * HEAD_PAD + LANES] = (nope * r * qn[:, 0:LANES]).astype(BF16)
            q_ref[rs, hh * HEAD_PAD + LANES:(hh + 1) * HEAD_PAD] = _rope_block(
                rope * r * qn[:, LANES:2 * LANES], c, s, D_ROPE // 2).astype(BF16)
        kr_ss = jnp.sum(kr * kr, axis=-1, keepdims=True)
        kr_rot = _rope_block(kr * kn[:, LANES:2 * LANES], c, s, D_ROPE // 2)
        for hh in range(H_C):
            nope = kvu[:, hh * D_NOPE:(hh + 1) * D_NOPE]
            ss = jnp.sum(nope * nope, axis=-1, keepdims=True) + kr_ss
            r = lax.rsqrt(ss * (1.0 / DQK_C) + EPS)
            k_ref[rs, hh * HEAD_PAD:hh * HEAD_PAD + LANES] = (nope * r * kn[:, 0:LANES]).astype(BF16)
            k_ref[rs, hh * HEAD_PAD + LANES:(hh + 1) * HEAD_PAD] = (kr_rot * r).astype(BF16)

    part = x.shape[0] // IN_PROJ_PARTS
    parts = [slice(k * part, (k + 1) * part) for k in range(IN_PROJ_PARTS)]
    z_prev = project(parts[0])
    for k in range(1, IN_PROJ_PARTS):
        z_next = project(parts[k])
        epilogue(parts[k - 1], z_prev)
        z_prev = z_next
    epilogue(parts[-1], z_prev)
    cbuf[0:CONV_PAD, :] = cbuf[tm:tm + CONV_PAD, :]


def _odd_in(h, g, w_in, qan, wuq, kvan, wukv, qn, kn, c_tab, s_tab, alog, dtb, conv_w, seq):
    n = h.shape[0]
    tm = min(ROW_TILE, seq)
    row = lambda w: pl.BlockSpec((tm, w), lambda i: (i, 0))
    full = lambda a: pl.BlockSpec(a.shape, lambda i: (0,) * a.ndim)
    sds = jax.ShapeDtypeStruct
    return pl.pallas_call(
        functools.partial(_odd_in_kernel, tm=tm, seq=seq),
        grid=(n // tm,),
        in_specs=[row(D_MODEL), full(g), full(w_in), full(qan), full(wuq), full(kvan), full(wukv),
                  full(qn), full(kn), row(LANES), row(LANES), full(alog), full(dtb), full(conv_w)],
        out_specs=[row(H_C * HEAD_PAD), row(H_C * HEAD_PAD), row(W_C), row(W_C), row(CONV_CH), row(W_D),
                   row(LANES)],
        out_shape=[sds((n, H_C * HEAD_PAD), BF16), sds((n, H_C * HEAD_PAD), BF16), sds((n, W_C), BF16),
                   sds((n, W_C), BF16), sds((n, CONV_CH), BF16), sds((n, W_D), BF16), sds((n, LANES), F32)],
        scratch_shapes=[pltpu.VMEM((tm + CONV_PAD, CONV_CH), F32)],
        compiler_params=_cparams(("arbitrary",)),
        name="odd_in_proj",
    )(h, g, w_in, qan, wuq, kvan, wukv, qn, kn, c_tab, s_tab, alog, dtb, conv_w)


def _gdn_prep_kernel(y_ref, gbc_ref, lhs1_ref, lhs2_ref, u_ref, gl_ref, *, tm):
    ri = lax.broadcasted_iota(jnp.int32, (tm, tm), 0)
    ci = lax.broadcasted_iota(jnp.int32, (tm, tm), 1)
    low = jnp.where(((ri // CHUNK) == (ci // CHUNK)) & (ri >= ci), 1.0, 0.0).astype(F32)
    i2 = lax.broadcasted_iota(jnp.int32, (LANES, LANES), 0)
    j2 = lax.broadcasted_iota(jnp.int32, (LANES, LANES), 1)
    eye_l = jnp.where(i2 == j2, 1.0, 0.0).astype(BF16)
    gbc = gbc_ref[...]
    gcs = _split3_dot(low, gbc)
    gcs_t = _split3_dot_nt(eye_l, gcs)

    pdim = H_D * CHUNK
    r4 = lax.broadcasted_iota(jnp.int32, (pdim, pdim), 0)
    c4 = lax.broadcasted_iota(jnp.int32, (pdim, pdim), 1)
    same = (r4 // CHUNK) == (c4 // CHUNK)
    incl = same & (r4 >= c4)
    strict = same & (r4 > c4)
    eye = jnp.where(r4 == c4, 1.0, 0.0).astype(F32)
    zblk = jnp.zeros((CHUNK, DK_D), F32)

    def block_diag(blocks):
        return jnp.concatenate(
            [jnp.concatenate([blocks[h] if j == h else zblk for j in range(H_D)], axis=1) for h in range(H_D)],
            axis=0)

    chunks = list(range(tm // CHUNK))
    heads = list(range(H_D))

    def load_qkv(c):
        r0 = c * CHUNK
        qs, ks, vs = [], [], []
        for hh in heads:
            l0 = hh * DK_D
            qh = y_ref[r0:r0 + CHUNK, l0:l0 + DK_D].astype(F32)
            kh = y_ref[r0:r0 + CHUNK, H_D * DK_D + l0:H_D * DK_D + l0 + DK_D].astype(F32)
            vs.append(y_ref[r0:r0 + CHUNK, 2 * H_D * DK_D + l0:2 * H_D * DK_D + l0 + DV_D].astype(F32))
            qs.append(qh * lax.rsqrt(jnp.sum(qh * qh, axis=-1, keepdims=True) + EPS) * (DK_D ** -0.5))
            ks.append(kh * lax.rsqrt(jnp.sum(kh * kh, axis=-1, keepdims=True) + EPS))
        return qs, ks, vs

    def decay_terms(c):
        r0 = c * CHUNK
        beta = jnp.concatenate([gbc[r0:r0 + CHUNK, hh:hh + 1] for hh in heads], axis=0)
        gcol = jnp.concatenate([gcs[r0:r0 + CHUNK, H_D + hh:H_D + hh + 1] for hh in heads], axis=0)
        grow = jnp.concatenate([gcs_t[H_D + hh:H_D + hh + 1, r0:r0 + CHUNK] for hh in heads], axis=1)
        glasts = [gcs[r0 + CHUNK - 1:r0 + CHUNK, H_D + hh:H_D + hh + 1] for hh in heads]
        glast = jnp.concatenate([jnp.broadcast_to(gl, (CHUNK, 1)) for gl in glasts], axis=0)
        gamma = jnp.exp(jnp.where(incl, gcol - grow, -jnp.inf))
        return beta, gcol, glasts, glast, gamma

    qkv = [load_qkv(c) for c in chunks]
    q_st = [jnp.concatenate(x[0], axis=0) for x in qkv]
    k_st = [jnp.concatenate(x[1], axis=0) for x in qkv]
    v_st = [jnp.concatenate(x[2], axis=0) for x in qkv]
    k_bd = [block_diag(x[1]).astype(BF16) for x in qkv]
    q_bd = [block_diag(x[0]).astype(BF16) for x in qkv]
    qkkk = [_bdot_nt(jnp.concatenate([qb, kb], axis=0), kb) for qb, kb in zip(q_bd, k_bd)]
    dec = [decay_terms(c) for c in chunks]
    a = [jnp.where(strict, x[pdim:, :] * d[4] * d[0], 0.0) for x, d in zip(qkkk, dec)]
    t = [eye - x for x in a]
    pw = a
    for _ in range(5):
        pw = [_bdot(x, x) for x in pw]
        t = [x + _bdot(x, y) for x, y in zip(t, pw)]
    egc = [jnp.exp(d[1]) for d in dec]
    uw = [_bdot(tt, jnp.concatenate([v * d[0], k * (d[0] * e)], axis=1))
          for tt, v, k, d, e in zip(t, v_st, k_st, dec, egc)]
    kdt = [_bdot_nt(eye_l, k * jnp.exp(d[3] - d[1])).astype(BF16) for k, d in zip(k_st, dec)]
    for c in chunks:
        qkg = (qkkk[c][0:pdim, :] * dec[c][4]).astype(BF16)
        q_dec = (q_st[c] * egc[c]).astype(BF16)
        for hh in heads:
            idx = c * H_D + hh
            hrows = slice(hh * CHUNK, (hh + 1) * CHUNK)
            lhs1_ref[idx, 0:CHUNK, :] = uw[c][hrows, DV_D:].astype(BF16)
            lhs1_ref[idx, CHUNK:2 * CHUNK, :] = q_dec[hrows, :]
            u_ref[idx] = uw[c][hrows, 0:DV_D].astype(BF16)
            gl_ref[c, hh:hh + 1, :] = jnp.broadcast_to(jnp.exp(dec[c][2][hh]), (1, LANES))
        for pp in range(H_D // 2):
            idx2 = c * (H_D // 2) + pp
            lhs2_ref[idx2, 0:LANES, :] = qkg[pp * LANES:(pp + 1) * LANES, pp * LANES:(pp + 1) * LANES]
            lhs2_ref[idx2, LANES:2 * LANES, :] = kdt[c][:, pp * LANES:(pp + 1) * LANES]


def _gdn_prep(y, gb, seq):
    n = y.shape[0]
    tm = min(GDN_PREP_TILE, seq)
    nch = tm // CHUNK
    row = lambda w: pl.BlockSpec((tm, w), lambda i: (i, 0))
    ch3 = lambda m, r, w: pl.BlockSpec((nch * m, r, w), lambda i: (i, 0, 0))
    sds = jax.ShapeDtypeStruct
    nc = n // CHUNK
    return pl.pallas_call(
        functools.partial(_gdn_prep_kernel, tm=tm),
        grid=(n // tm,),
        in_specs=[row(CONV_CH), row(LANES)],
        out_specs=[ch3(H_D, 2 * CHUNK, DK_D), ch3(H_D // 2, 2 * LANES, LANES), ch3(H_D, CHUNK, DV_D),
                   pl.BlockSpec((nch, H_D, LANES), lambda i: (i, 0, 0))],
        out_shape=[sds((nc * H_D, 2 * CHUNK, DK_D), BF16), sds((nc * H_D // 2, 2 * LANES, LANES), BF16),
                   sds((nc * H_D, CHUNK, DV_D), BF16), sds((nc, H_D, LANES), F32)],
        compiler_params=_cparams(("parallel",)),
        name="gdn_chunk_prep",
    )(y, gb)


def _gdn_scan_kernel(lhs1_ref, lhs2_ref, u_ref, gl_ref, dg_ref, on_ref, o_ref, s_scr, *, nb, tb):
    @pl.when(pl.program_id(0) == 0)
    def _():
        s_scr[...] = jnp.zeros(s_scr.shape, F32)

    on = on_ref[...]
    zb = jnp.zeros((2 * CHUNK, DK_D), BF16)
    left = lax.broadcasted_iota(jnp.int32, (DK_D, LANES), 1) < CHUNK

    def finish(o, b, rows, hh):
        ms = jnp.mean(o * o, axis=-1, keepdims=True)
        o = o * lax.rsqrt(ms + EPS) * on
        gate = dg_ref[b, rows, hh * DV_D:(hh + 1) * DV_D].astype(F32)
        o_ref[b, rows, hh * DV_D:(hh + 1) * DV_D] = (o * _silu(gate)).astype(BF16)

    def chunk_body(c, carry):
        rows = pl.ds(pl.multiple_of(c * CHUNK, CHUNK), CHUNK)
        chains = [(b, 2 * pp) for b in range(nb) for pp in range(H_D // 2)]
        st = [(s_scr[b * H_D + h0], s_scr[b * H_D + h0 + 1]) for b, h0 in chains]
        r1 = [jnp.dot(jnp.concatenate([jnp.concatenate([lhs1_ref[b, c * H_D + h0], zb], axis=1),
                                       jnp.concatenate([zb, lhs1_ref[b, c * H_D + h0 + 1]], axis=1)], axis=0),
                      jnp.concatenate([s0, s1], axis=0).astype(BF16), preferred_element_type=F32)
              for (b, h0), (s0, s1) in zip(chains, st)]
        v2 = [jnp.concatenate([u_ref[b, c * H_D + h0].astype(F32) - x[0:CHUNK, :],
                               u_ref[b, c * H_D + h0 + 1].astype(F32) - x[2 * CHUNK:3 * CHUNK, :]],
                              axis=0).astype(BF16) for (b, h0), x in zip(chains, r1)]
        r2 = []
        for (b, h0), v in zip(chains, v2):
            blk = lhs2_ref[b, c * (H_D // 2) + h0 // 2]
            kd = blk[LANES:2 * LANES, :]
            zk = jnp.zeros_like(kd)
            l2 = jnp.concatenate([blk[0:LANES, :], jnp.where(left, kd, zk), jnp.where(left, zk, kd)], axis=0)
            r2.append(jnp.dot(l2, v, preferred_element_type=F32))
        for (b, h0), (s0, s1), x, y in zip(chains, st, r1, r2):
            s_scr[b * H_D + h0] = s0 * gl_ref[b, c, h0:h0 + 1, :] + y[LANES:LANES + DK_D, :]
            s_scr[b * H_D + h0 + 1] = s1 * gl_ref[b, c, h0 + 1:h0 + 2, :] + y[LANES + DK_D:LANES + 2 * DK_D, :]
            finish(x[CHUNK:2 * CHUNK, :] + y[0:CHUNK, :], b, rows, h0)
            finish(x[3 * CHUNK:4 * CHUNK, :] + y[CHUNK:2 * CHUNK, :], b, rows, h0 + 1)
        return carry

    lax.fori_loop(0, tb // CHUNK, chunk_body, 0)


def _gdn_scan(lhs1, lhs2, u, gl, d_gate, o_norm, batch, seq):
    tb = min(GDN_SCAN_TILE, seq)
    nch = tb // CHUNK
    ncb = seq // CHUNK
    lhs1 = lhs1.reshape(batch, ncb * H_D, 2 * CHUNK, DK_D)
    lhs2 = lhs2.reshape(batch, ncb * H_D // 2, 2 * LANES, LANES)
    u = u.reshape(batch, ncb * H_D, CHUNK, DV_D)
    gl = gl.reshape(batch, ncb, H_D, LANES)
    d_gate = d_gate.reshape(batch, seq, W_D)
    ch4 = lambda r, w: pl.BlockSpec((batch, nch * H_D, r, w), lambda i: (0, i, 0, 0))
    tok = pl.BlockSpec((batch, tb, W_D), lambda i: (0, i, 0))
    out = pl.pallas_call(
        functools.partial(_gdn_scan_kernel, nb=batch, tb=tb),
        grid=(seq // tb,),
        in_specs=[ch4(2 * CHUNK, DK_D),
                  pl.BlockSpec((batch, nch * H_D // 2, 2 * LANES, LANES), lambda i: (0, i, 0, 0)),
                  ch4(CHUNK, DV_D),
                  pl.BlockSpec((batch, nch, H_D, LANES), lambda i: (0, i, 0, 0)), tok,
                  pl.BlockSpec(o_norm.shape, lambda i: (0, 0))],
        out_specs=tok,
        out_shape=jax.ShapeDtypeStruct((batch, seq, W_D), BF16),
        scratch_shapes=[pltpu.VMEM((batch * H_D, DK_D, DV_D), F32)],
        compiler_params=_cparams(("arbitrary",)),
        name="gdn_state_scan",
    )(lhs1, lhs2, u, gl, d_gate, o_norm)
    return out.reshape(batch * seq, W_D)


def _rope_patterns(rot_dim, theta, period):
    half = rot_dim // 2
    inv = jnp.power(jnp.float32(theta), -jnp.arange(half, dtype=F32) * (2.0 / rot_dim))
    lane = np.arange(LANES)
    in_rot = (lane % period) < rot_dim
    idx = jnp.asarray(lane % half)
    inv_l = jnp.where(jnp.asarray(in_rot), inv[idx], 0.0).astype(F32)[None, :]
    sgn = np.where(in_rot, np.where((lane % period) < half, -1.0, 1.0), 0.0).astype(np.float32)[None, :]
    return inv_l, jnp.asarray(sgn)


def _pad_cols(w, width):
    return jnp.pad(w, ((0, 0), (0, width - w.shape[1])))


def _odd_w_in_layout(w):
    cq, ckv, kr, cg, qkv, db, da, dg = jnp.split(
        w, np.cumsum((Q_LORA, KV_LORA, D_ROPE, W_C, CONV_CH, H_D, H_D, W_D))[:-1].tolist(), axis=1)
    return jnp.concatenate([cq, ckv, _pad_cols(kr, LANES), cg, qkv, dg,
                            _pad_cols(jnp.concatenate([db, da], axis=1), LANES)], axis=1)


def _wuq_layout(w):
    w = w.reshape(Q_LORA, H_C, DQK_C)
    nope = w[:, :, :D_NOPE].reshape(Q_LORA, H_C * D_NOPE)
    rope = jnp.pad(w[:, :, D_NOPE:], ((0, 0), (0, 0), (0, LANES - D_ROPE))).reshape(Q_LORA, H_C * LANES)
    return jnp.concatenate([nope, rope], axis=1)


def _wukv_layout(w):
    w = w.reshape(KV_LORA, H_C, D_NOPE + DV_C)
    return jnp.concatenate([w[:, :, :D_NOPE].reshape(KV_LORA, H_C * D_NOPE),
                            w[:, :, D_NOPE:].reshape(KV_LORA, H_C * DV_C)], axis=1)


def _head_gain_layout(g):
    return jnp.pad(g, (0, HEAD_PAD - DQK_C))[None, :].astype(F32)


def kernel(x, p, positions, norm_g, ple_w_gate, ple_w_proj, ev_w_in, ev_pool_w, ev_pool_scale, ev_q_norm, ev_k_norm, ev_lambda, ev_subln, ev_w_out, od_w_in, od_q_a_norm, od_w_uq, od_kv_a_norm, od_w_ukv, od_q_norm, od_k_norm, od_conv_w, od_a_log, od_dt_bias, od_o_norm, od_w_out):
    batch, seq, _ = x.shape
    depth = p.shape[0]
    n = batch * seq
    h = x.reshape(n, D_MODEL)
    pos_b = jnp.broadcast_to(positions.astype(F32).reshape(n, 1), (n, LANES))

    inv_e, sgn_e = _rope_patterns(B_ROT, ROPE_THETA, DK_B)
    ce, se = _rope_tables(pos_b, inv_e, sgn_e)
    inv_o, sgn_o = _rope_patterns(D_ROPE, MLA_THETA, LANES)
    co, so = _rope_tables(pos_b, inv_o, sgn_o)


    p_all = p.reshape(depth, n, PLE_DIM)
    for i in range(depth):
        j = i // 2
        g = norm_g[i][None, :]
        w_gate = ple_w_gate[i].astype(BF16)
        w_proj = ple_w_proj[i].astype(BF16)
        if i % 2 == 0:
            w_in = ev_w_in[j].astype(BF16)
            qg = (jnp.tile(ev_q_norm[j], 2 * H_B) * (DK_B ** -0.5 * LOG2E))[None, :]
            kg = jnp.tile(ev_k_norm[j], 2 * H_B)[None, :]
            a_in, a_gate, q, k, v, b_gate = _even_in(h, g, w_in, qg, kg, ce, se)
            lam_init = 0.8 - 0.6 * math.exp(-0.3 * i)
            ob = _diff_attention(q, k, v, b_gate, ev_lambda[j], ev_subln[j][None, :], batch, seq, lam_init)
            pw = jax.scipy.linalg.block_diag(*[ev_pool_w[j][gi] for gi in range(len(POOL_WINDOWS))]).astype(BF16)
            h = _even_out_proj(h, a_in, a_gate, pw, ev_pool_scale[j][None, :], ob, ev_w_out[j].astype(BF16),
                               w_gate, p_all, i, w_proj, seq)
        else:
            w_in = _odd_w_in_layout(od_w_in[j]).astype(BF16)
            alog = jnp.zeros((LANES,), F32).at[H_D:2 * H_D].set(od_a_log[j])[None, :]
            dtb = jnp.zeros((LANES,), F32).at[H_D:2 * H_D].set(od_dt_bias[j])[None, :]
            q, k, v, c_gate, qkv, d_gate, gb = _odd_in(
                h, g, w_in, od_q_a_norm[j][None, :], _wuq_layout(od_w_uq[j]).astype(BF16),
                od_kv_a_norm[j][None, :], _wukv_layout(od_w_ukv[j]).astype(BF16),
                _head_gain_layout(od_q_norm[j]), _head_gain_layout(od_k_norm[j]), co, so, alog, dtb,
                od_conv_w[j], seq)
            oa = _mla_attention(q, k, v, c_gate, batch, seq)
            lhs1, lhs2, u, gl = _gdn_prep(qkv, gb, seq)
            ob = _gdn_scan(lhs1, lhs2, u, gl, d_gate, od_o_norm[j][None, :], batch, seq)
            h = _out_proj(h, oa, ob, od_w_out[j].astype(BF16), w_gate, p_all, i, w_proj)
    return h.reshape(batch, seq, D_MODEL)
```
